```python
import math, functools
import jax, jax.numpy as jnp
from jax import lax
import numpy as np

D_MODEL = 1024
BATCH = 4
SEQ = 4096
DEPTH = 1
DEC_BATCH = 128
DEC_SEQ = 1
PAST_LEN = 8192
PAGE_SIZE = 128

N_HEADS = 8
N_KV_HEADS = 4
HEAD_DIM = 128
KV_REP = N_HEADS // N_KV_HEADS
IDX_HEADS = 8
IDX_DIM = 64
TOPK_MAX = 256
TOPK_DIV = 4
N_BUCKETS = 32
MAX_DISTANCE = 128
Q_BLOCK = 128
SSM_HEADS = 16
SSM_HEAD_DIM = 64
SSM_GROUPS = 4
SSM_STATE = 128
SSM_CONV = 4
SSD_CHUNK = 128
D_ATT = N_HEADS * HEAD_DIM
D_SSM = SSM_HEADS * SSM_HEAD_DIM
D_MIX = D_ATT + D_SSM
CONV_DIM = D_SSM + 2 * SSM_GROUPS * SSM_STATE
D_FF = 2816
FFN_CONV = 3
EPS = 1e-6
IN_SPLITS = (D_ATT, N_KV_HEADS * HEAD_DIM, N_KV_HEADS * HEAD_DIM, IDX_HEADS * IDX_DIM, IDX_DIM, IDX_HEADS, D_SSM, CONV_DIM, SSM_HEADS)
D_IN = sum(IN_SPLITS)

kernel_name = 'hybrid_dsa_ssd_convffn_step'


def rmsnorm(x, g):
    xf = x.astype(jnp.float32)
    y = xf * lax.rsqrt(jnp.mean(xf * xf, axis=-1, keepdims=True) + EPS)
    return (y * g.astype(jnp.float32)).astype(x.dtype)


def ada_modulation(c, w_ada, b_ada):
    m = jax.nn.silu(c) @ w_ada + b_ada
    return [p[:, None, :] for p in jnp.split(m, 6, axis=-1)]


def causal_dwconv(x_ext, w, b):
    y = lax.conv_general_dilated(x_ext, w[:, None, :], window_strides=(1,), padding='VALID',
                                 dimension_numbers=('NWC', 'WIO', 'NWC'), feature_group_count=x_ext.shape[-1])
    return y + b


def t5_bucket(dist):
    dist = jnp.maximum(dist, 0)
    max_exact = N_BUCKETS // 2
    d = jnp.maximum(dist, 1).astype(jnp.float32)
    large = max_exact + (jnp.log(d / max_exact) / math.log(MAX_DISTANCE / max_exact) * (N_BUCKETS - max_exact)).astype(jnp.int32)
    large = jnp.minimum(large, N_BUCKETS - 1)
    return jnp.where(dist < max_exact, dist, large)


def indexer_topk(qi, wi, ki_all, q_pos, topk):
    dots = jnp.einsum('bthd,bld->bthl', qi, ki_all, preferred_element_type=jnp.float32)
    score = jnp.einsum('bth,bthl->btl', wi.astype(jnp.float32), jax.nn.relu(dots))
    key_pos = jnp.arange(ki_all.shape[1], dtype=jnp.int32)
    score = jnp.where(key_pos[None, None, :] <= q_pos[None, :, None], score, -jnp.inf)
    _, idx = lax.top_k(score, topk)
    return idx, idx <= q_pos[None, :, None]


def sparse_attend(q, k_sel, v_sel, idx, valid, q_pos, rel_bias):
    b, t = q.shape[:2]
    kk = idx.shape[-1]
    qg = q.reshape(b, t, N_KV_HEADS, KV_REP, HEAD_DIM)
    logits = jnp.einsum('btgrd,btkgd->btgrk', qg, k_sel, preferred_element_type=jnp.float32) * (HEAD_DIM ** -0.5)
    bias = rel_bias[t5_bucket(q_pos[None, :, None] - idx)].astype(jnp.float32)
    bias = bias.reshape(b, t, kk, N_KV_HEADS, KV_REP).transpose(0, 1, 3, 4, 2)
    logits = jnp.where(valid[:, :, None, None, :], logits + bias, -jnp.inf)
    p = jax.nn.softmax(logits, axis=-1)
    o = jnp.einsum('btgrk,btkgd->btgrd', p.astype(v_sel.dtype), v_sel)
    return o.reshape(b, t, D_ATT)


def dsa_prompt(rel_bias, q, k, v, qi, ki, wi):
    b, s = q.shape[:2]
    topk = min(TOPK_MAX, s // TOPK_DIV)
    nblk = s // Q_BLOCK

    def to_blocks(a):
        return a.reshape(b, nblk, Q_BLOCK, *a.shape[2:]).swapaxes(0, 1)

    def block_fn(args):
        blk, qb, qib, wib = args
        q_pos = blk * Q_BLOCK + jnp.arange(Q_BLOCK, dtype=jnp.int32)
        idx, valid = indexer_topk(qib, wib, ki, q_pos, topk)
        k_sel = jax.vmap(lambda kr, ir: kr[ir])(k, idx)
        v_sel = jax.vmap(lambda vr, ir: vr[ir])(v, idx)
        return sparse_attend(qb, k_sel, v_sel, idx, valid, q_pos, rel_bias)

    o = lax.map(block_fn, (jnp.arange(nblk, dtype=jnp.int32), to_blocks(q), to_blocks(qi), to_blocks(wi)))
    return o.swapaxes(0, 1).reshape(b, s, D_ATT)


def dsa_sample(cache_k, cache_v, cache_kidx, page_table, rel_bias, layer, q, k_new, v_new, qi, ki_new, wi):
    b, t = q.shape[:2]
    page = cache_k.shape[2]
    past = page_table.shape[1] * page
    topk = min(TOPK_MAX, (past + t) // TOPK_DIV)
    ki_past = cache_kidx[layer, page_table].reshape(b, past, IDX_DIM)
    ki_all = jnp.concatenate([ki_past, ki_new.astype(ki_past.dtype)], axis=1)
    q_pos = past + jnp.arange(t, dtype=jnp.int32)
    idx, valid = indexer_topk(qi, wi, ki_all, q_pos, topk)
    in_past = idx < past
    pidx = jnp.minimum(idx, past - 1)
    phys = jnp.take_along_axis(page_table, (pidx // page).reshape(b, -1), axis=1).reshape(idx.shape)
    off = pidx % page
    nidx = jnp.clip(idx - past, 0, t - 1)

    def gather(pool, new):
        old = pool[layer, phys, off]
        cur = jax.vmap(lambda nr, ir: nr[ir])(new, nidx)
        return jnp.where(in_past[..., None, None], old, cur.astype(old.dtype))

    return sparse_attend(q, gather(cache_k, k_new), gather(cache_v, v_new), idx, valid, q_pos, rel_bias)


def ssd_chunked(xs, dt, a, bm, cm, h0):
    f32 = jnp.float32
    b_, l_, h_, p_ = xs.shape
    g_, n_ = bm.shape[2], bm.shape[3]
    r_ = h_ // g_
    q_ = min(SSD_CHUNK, l_)
    pad = (-l_) % q_
    xs, dt, bm, cm = xs.astype(f32), dt.astype(f32), bm.astype(f32), cm.astype(f32)
    if pad:
        padw = lambda arr: jnp.pad(arr, [(0, 0), (0, pad)] + [(0, 0)] * (arr.ndim - 2))
        xs, dt, bm, cm = padw(xs), padw(dt), padw(bm), padw(cm)
    c_ = (l_ + pad) // q_
    xdt = (xs * dt[..., None]).reshape(b_, c_, q_, g_, r_, p_)
    acs = jnp.cumsum((dt * a).reshape(b_, c_, q_, g_, r_), axis=2)
    bm = bm.reshape(b_, c_, q_, g_, n_)
    cm = cm.reshape(b_, c_, q_, g_, n_)
    seg = acs[:, :, :, None] - acs[:, :, None, :]
    causal = jnp.tril(jnp.ones((q_, q_), bool))[None, None, :, :, None, None]
    lmat = jnp.exp(jnp.where(causal, seg, -jnp.inf))
    cb = jnp.einsum('bcqgn,bcsgn->bcqsg', cm, bm)
    y_diag = jnp.einsum('bcqsg,bcqsgr,bcsgrp->bcqgrp', cb, lmat, xdt)
    decay_to_end = jnp.exp(acs[:, :, -1:] - acs)
    chunk_states = jnp.einsum('bcsgn,bcsgr,bcsgrp->bcgrpn', bm, decay_to_end, xdt)
    chunk_decay = jnp.exp(acs[:, :, -1])

    def step(h, inp):
        st, dec = inp
        return h * dec[..., None, None] + st, h

    h_final, h_starts = lax.scan(step, h0.astype(f32).reshape(b_, g_, r_, p_, n_),
                                 (chunk_states.swapaxes(0, 1), chunk_decay.swapaxes(0, 1)))
    h_starts = h_starts.swapaxes(0, 1)
    y_off = jnp.einsum('bcqgn,bcgrpn,bcqgr->bcqgrp', cm, h_starts, jnp.exp(acs))
    y = (y_diag + y_off).reshape(b_, c_ * q_, h_, p_)[:, :l_]
    return y, h_final.reshape(b_, h_, p_, n_).astype(h0.dtype)


def ssd_branch(xbc_ext, dt_raw, zg, h0, conv_w, conv_b, dt_bias, a_log, d_skip, norm_g):
    b, t, _ = dt_raw.shape
    xbc = jax.nn.silu(causal_dwconv(xbc_ext, conv_w, conv_b))
    xs, bm, cm = jnp.split(xbc, [D_SSM, D_SSM + SSM_GROUPS * SSM_STATE], axis=-1)
    xs = xs.reshape(b, t, SSM_HEADS, SSM_HEAD_DIM)
    bm = bm.reshape(b, t, SSM_GROUPS, SSM_STATE)
    cm = cm.reshape(b, t, SSM_GROUPS, SSM_STATE)
    dt = jax.nn.softplus(dt_raw.astype(jnp.float32) + dt_bias.astype(jnp.float32))
    a = -jnp.exp(a_log.astype(jnp.float32))
    y, h_new = ssd_chunked(xs, dt, a, bm, cm, h0)
    y = y + d_skip.astype(jnp.float32)[:, None] * xs.astype(jnp.float32)
    y = y.reshape(b, t, D_SSM).astype(zg.dtype) * jax.nn.silu(zg)
    return rmsnorm(y, norm_g), h_new


def conv_ffn(h, hist, w_up, conv_w, conv_b, w_down):
    u = h @ w_up
    u_ext = jnp.concatenate([hist.astype(u.dtype), u], axis=1)
    uc = causal_dwconv(u_ext, conv_w, conv_b)
    gate, val = jnp.split(uc, 2, axis=-1)
    return (jax.nn.silu(gate) * val) @ w_down, u_ext[:, -(FFN_CONV - 1):]


def trunk_layer(x, c, attend, xbc_hist, h0, ffn_hist, norm1_g, w_ada, b_ada, w_in, q_norm_g, k_norm_g,
                conv_ssd_w, conv_ssd_b, dt_bias, a_log, d_skip, ssd_norm_g, w_out, norm2_g, w_up,
                conv_ffn_w, conv_ffn_b, w_down):
    b, t, _ = x.shape
    sh1, sc1, g1, sh2, sc2, g2 = ada_modulation(c, w_ada, b_ada)
    h = rmsnorm(x, norm1_g) * (1 + sc1) + sh1
    q, k, v, qi, ki, wi, zg, xbc, dt = jnp.split(h @ w_in, np.cumsum(IN_SPLITS)[:-1].tolist(), axis=-1)
    q = rmsnorm(q.reshape(b, t, N_HEADS, HEAD_DIM), q_norm_g)
    k = rmsnorm(k.reshape(b, t, N_KV_HEADS, HEAD_DIM), k_norm_g)
    v = v.reshape(b, t, N_KV_HEADS, HEAD_DIM)
    qi = qi.reshape(b, t, IDX_HEADS, IDX_DIM)
    att = attend(q, k, v, qi, ki, wi)
    xbc_ext = jnp.concatenate([xbc_hist.astype(xbc.dtype), xbc], axis=1)
    ssd, h_new = ssd_branch(xbc_ext, dt, zg, h0, conv_ssd_w, conv_ssd_b, dt_bias, a_log, d_skip, ssd_norm_g)
    x = x + g1 * (jnp.concatenate([att, ssd.astype(att.dtype)], axis=-1) @ w_out)
    h2 = rmsnorm(x, norm2_g) * (1 + sc2) + sh2
    f, ffn_state = conv_ffn(h2, ffn_hist, w_up, conv_ffn_w, conv_ffn_b, w_down)
    x = x + g2 * f
    return x, (k, v, ki, h_new, xbc_ext[:, -(SSM_CONV - 1):], ffn_state)


def setup_inputs(seed: int = 0) -> dict:
    key = jax.random.key(seed)
    ks = jax.random.split(key, 40)
    f32 = jnp.float32
    n_pages = PAST_LEN // PAGE_SIZE
    n_pool = (DEC_BATCH * n_pages * 5) // 4
    nrm = lambda k, shape, s=1.0: jax.random.normal(k, shape, f32) * s
    gain = lambda k, shape: 1.0 + 0.02 * jax.random.normal(k, shape, f32)
    perm = jax.random.permutation(ks[0], n_pool)
    page_table = perm[: DEC_BATCH * n_pages].reshape(DEC_BATCH, n_pages).astype(jnp.int32)
    dt0 = jnp.exp(jax.random.uniform(ks[1], (DEPTH, SSM_HEADS), f32, math.log(1e-3), math.log(1e-1)))
    dt_bias = dt0 + jnp.log(-jnp.expm1(-dt0))
    a_log = jnp.log(jax.random.uniform(ks[2], (DEPTH, SSM_HEADS), f32, 1.0, 16.0))
    return {
        'x_prompt': nrm(ks[3], (BATCH, SEQ, D_MODEL)),
        'x_sample': nrm(ks[4], (DEC_BATCH, DEC_SEQ, D_MODEL)),
        'cache_k': nrm(ks[5], (DEPTH, n_pool, PAGE_SIZE, N_KV_HEADS, HEAD_DIM)),
        'cache_v': nrm(ks[6], (DEPTH, n_pool, PAGE_SIZE, N_KV_HEADS, HEAD_DIM)),
        'cache_kidx': nrm(ks[7], (DEPTH, n_pool, PAGE_SIZE, IDX_DIM)),
        'state_ssm': nrm(ks[8], (DEPTH, DEC_BATCH, SSM_HEADS, SSM_HEAD_DIM, SSM_STATE), 0.5),
        'state_conv_ssd': nrm(ks[9], (DEPTH, DEC_BATCH, SSM_CONV - 1, CONV_DIM)),
        'state_conv_ffn': nrm(ks[10], (DEPTH, DEC_BATCH, FFN_CONV - 1, 2 * D_FF)),
        'page_table': page_table,
        'c_prompt': nrm(ks[11], (BATCH, D_MODEL)),
        'c_sample': nrm(ks[12], (DEC_BATCH, D_MODEL)),
        'rel_bias': nrm(ks[13], (N_BUCKETS, N_HEADS), 0.5),
        'norm1_g': gain(ks[14], (DEPTH, D_MODEL)),
        'w_ada': nrm(ks[15], (DEPTH, D_MODEL, 6 * D_MODEL), 0.5 * D_MODEL ** -0.5),
        'b_ada': nrm(ks[16], (DEPTH, 6 * D_MODEL), 0.02),
        'w_in': nrm(ks[17], (DEPTH, D_MODEL, D_IN), D_MODEL ** -0.5),
        'q_norm_g': gain(ks[18], (DEPTH, HEAD_DIM)),
        'k_norm_g': gain(ks[19], (DEPTH, HEAD_DIM)),
        'conv_ssd_w': nrm(ks[20], (DEPTH, SSM_CONV, CONV_DIM), SSM_CONV ** -0.5),
        'conv_ssd_b': nrm(ks[21], (DEPTH, CONV_DIM), 0.02),
        'dt_bias': dt_bias,
        'a_log': a_log,
        'd_skip': gain(ks[22], (DEPTH, SSM_HEADS)),
        'ssd_norm_g': gain(ks[23], (DEPTH, D_SSM)),
        'w_out': nrm(ks[24], (DEPTH, D_MIX, D_MODEL), D_MIX ** -0.5),
        'norm2_g': gain(ks[25], (DEPTH, D_MODEL)),
        'w_up': nrm(ks[26], (DEPTH, D_MODEL, 2 * D_FF), D_MODEL ** -0.5),
        'conv_ffn_w': nrm(ks[27], (DEPTH, FFN_CONV, 2 * D_FF), FFN_CONV ** -0.5),
        'conv_ffn_b': nrm(ks[28], (DEPTH, 2 * D_FF), 0.02),
        'w_down': nrm(ks[29], (DEPTH, D_FF, D_MODEL), D_FF ** -0.5),
    }


def reference(x_prompt, x_sample, cache_k, cache_v, cache_kidx, state_ssm, state_conv_ssd, state_conv_ffn,
              page_table, c_prompt, c_sample, rel_bias, norm1_g, w_ada, b_ada, w_in, q_norm_g, k_norm_g,
              conv_ssd_w, conv_ssd_b, dt_bias, a_log, d_skip, ssd_norm_g, w_out, norm2_g, w_up,
              conv_ffn_w, conv_ffn_b, w_down):
    xp, xs = x_prompt, x_sample
    bp = xp.shape[0]
    prompt_attend = functools.partial(dsa_prompt, rel_bias)
    new_p = [[] for _ in range(6)]
    new_s = [[] for _ in range(6)]
    for layer in range(DEPTH):
        lw = (norm1_g[layer], w_ada[layer], b_ada[layer], w_in[layer], q_norm_g[layer], k_norm_g[layer],
              conv_ssd_w[layer], conv_ssd_b[layer], dt_bias[layer], a_log[layer], d_skip[layer], ssd_norm_g[layer],
              w_out[layer], norm2_g[layer], w_up[layer], conv_ffn_w[layer], conv_ffn_b[layer], w_down[layer])
        xbc_hist0 = jnp.zeros((bp, SSM_CONV - 1, CONV_DIM), xp.dtype)
        h00 = jnp.zeros((bp, SSM_HEADS, SSM_HEAD_DIM, SSM_STATE), jnp.float32)
        ffn_hist0 = jnp.zeros((bp, FFN_CONV - 1, 2 * D_FF), xp.dtype)
        xp, st_p = trunk_layer(xp, c_prompt, prompt_attend, xbc_hist0, h00, ffn_hist0, *lw)
        sample_attend = functools.partial(dsa_sample, cache_k, cache_v, cache_kidx, page_table, rel_bias, layer)
        xs, st_s = trunk_layer(xs, c_sample, sample_attend, state_conv_ssd[layer], state_ssm[layer],
                               state_conv_ffn[layer], *lw)
        for i in range(6):
            new_p[i].append(st_p[i])
            new_s[i].append(st_s[i])
    k_p, v_p, kidx_p, ssm_p, cssd_p, cffn_p = [jnp.stack(a) for a in new_p]
    k_s, v_s, kidx_s, ssm_s, cssd_s, cffn_s = [jnp.stack(a) for a in new_s]
    return (xp, xs, k_p, v_p, kidx_p, ssm_p, cssd_p, cffn_p, k_s, v_s, kidx_s, ssm_s, cssd_s, cffn_s)
```

```python
import functools
import math

import jax
import jax.numpy as jnp
import numpy as np
from jax import lax
from jax.experimental import pallas as pl
from jax.experimental.pallas import tpu as pltpu

F32 = jnp.float32
BF16 = jnp.bfloat16

D_MODEL = 1024
N_HEADS = 8
N_KV_HEADS = 4
HEAD_DIM = 128
KV_REP = N_HEADS // N_KV_HEADS
IDX_HEADS = 8
IDX_DIM = 64
TOPK_MAX = 256
TOPK_DIV = 4
N_BUCKETS = 32
MAX_DISTANCE = 128
SSM_HEADS = 16
SSM_HEAD_DIM = 64
SSM_GROUPS = 4
SSM_STATE = 128
SSM_CONV = 4
SSD_CHUNK = 128
D_ATT = N_HEADS * HEAD_DIM
D_KV = N_KV_HEADS * HEAD_DIM
D_SSM = SSM_HEADS * SSM_HEAD_DIM
D_MIX = D_ATT + D_SSM
CONV_DIM = D_SSM + 2 * SSM_GROUPS * SSM_STATE
D_FF = 2816
FFN_CONV = 3
EPS = 1e-6
IN_SPLITS = (D_ATT, D_KV, D_KV, IDX_HEADS * IDX_DIM, IDX_DIM, IDX_HEADS, D_SSM, CONV_DIM, SSM_HEADS)

LANES = 128
SUBLANES = 8
VMEM_LIMIT = 52 * 1024 * 1024

OFF_Q = 0
OFF_K = OFF_Q + D_ATT
OFF_V = OFF_K + D_KV
OFF_QI = OFF_V + D_KV
OFF_KI = OFF_QI + IDX_HEADS * IDX_DIM
OFF_WI = OFF_KI + LANES
OFF_ZG = OFF_WI + LANES
OFF_XBC = OFF_ZG + D_SSM
OFF_DT = OFF_XBC + CONV_DIM
N_PACK = OFF_DT + LANES

NEG_BIG = -1e30
ATT_CHUNK = 256
FF_CHUNK = 256
ROW_TILE = 256
Q_BLOCK = 128


def _cparams(*sem):
    return pltpu.CompilerParams(dimension_semantics=sem, vmem_limit_bytes=VMEM_LIMIT)


def _resident(shape):
    nd = len(shape)
    return pl.BlockSpec(shape, lambda *_: (0,) * nd, pipeline_mode=pl.Buffered(1))


def _silu(x):
    return x * (1.0 / (1.0 + jnp.exp(-x)))


def _softplus(x):
    return jnp.maximum(x, 0.0) + jnp.log1p(jnp.exp(-jnp.abs(x)))


def _rms(x):
    return x * lax.rsqrt(jnp.mean(x * x, axis=-1, keepdims=True) + EPS)


def _split3(x):
    a = x.astype(BF16)
    r = x - a.astype(F32)
    b = r.astype(BF16)
    c = (r - b.astype(F32)).astype(BF16)
    return a, b, c


def _dot_sel_rhs(x, sel):
    return sum(jnp.dot(p, sel, preferred_element_type=F32) for p in _split3(x))


def _dot_sel_lhs(sel, x):
    return sum(jnp.dot(sel, p, preferred_element_type=F32) for p in _split3(x))


def _dot_nt(a, b):
    return lax.dot_general(a, b, (((1,), (1,)), ((), ())), preferred_element_type=F32)


def _dot_tn(a, b):
    return lax.dot_general(a, b, (((0,), (0,)), ((), ())), preferred_element_type=F32)


def _ada_kernel(c_ref, w_ref, b_ref, o_ref):
    s = _silu(c_ref[...]).astype(BF16)
    o_ref[...] = jnp.dot(s, w_ref[...].astype(BF16), preferred_element_type=F32) + b_ref[...]


def _ada(c_all, w_ada, b_ada):
    rows = c_all.shape[0]
    n = w_ada.shape[1]
    return pl.pallas_call(
        _ada_kernel,
        grid=(n // D_MODEL,),
        in_specs=[
            pl.BlockSpec((rows, D_MODEL), lambda j: (0, 0)),
            pl.BlockSpec((D_MODEL, D_MODEL), lambda j: (0, j)),
            pl.BlockSpec((1, D_MODEL), lambda j: (0, j)),
        ],
        out_specs=pl.BlockSpec((rows, D_MODEL), lambda j: (0, j)),
        out_shape=jax.ShapeDtypeStruct((rows, n), F32),
        compiler_params=_cparams("arbitrary"),
        name="ada_mod",
    )(c_all, w_ada, b_ada.reshape(1, n))


def _inproj_kernel(x_ref, g_ref, sc_ref, sh_ref, w_ref, qg_ref, kg_ref,
                   q_o, k_o, v_o, kb_o, vb_o, qi_o, ki_o, kib_o, wi_o, zg_o, xbc_o, dt_o):
    x = x_ref[0]
    h = _rms(x) * g_ref[...]
    hb = (h * (1.0 + sc_ref[0]) + sh_ref[0]).astype(BF16)

    def sec(a, b):
        return jnp.dot(hb, w_ref[:, a:b], preferred_element_type=F32)

    zq = sec(OFF_Q, OFF_K)
    for i in range(N_HEADS):
        zh = zq[:, i * HEAD_DIM:(i + 1) * HEAD_DIM]
        q_o[0, :, i * HEAD_DIM:(i + 1) * HEAD_DIM] = (_rms(zh) * qg_ref[...]).astype(BF16)
    zk = sec(OFF_K, OFF_V)
    for i in range(N_KV_HEADS):
        zh = zk[:, i * HEAD_DIM:(i + 1) * HEAD_DIM]
        kn = _rms(zh) * kg_ref[...]
        k_o[0, :, i * HEAD_DIM:(i + 1) * HEAD_DIM] = kn
        kb_o[0, :, i * HEAD_DIM:(i + 1) * HEAD_DIM] = kn.astype(BF16)
    zv = sec(OFF_V, OFF_QI)
    v_o[0] = zv
    vb_o[0] = zv.astype(BF16)
    qi_o[0] = sec(OFF_QI, OFF_KI).astype(BF16)
    zki = sec(OFF_KI, OFF_WI)[:, :IDX_DIM]
    ki_o[0] = zki
    kib_o[0] = zki.astype(BF16)
    wi_o[0] = sec(OFF_WI, OFF_ZG)[:, :IDX_HEADS]
    zg_o[0] = sec(OFF_ZG, OFF_XBC)
    xbc_o[0] = sec(OFF_XBC, OFF_DT)
    dt_o[0] = sec(OFF_DT, N_PACK)


def _inproj(x, norm_g, sc, sh, w_pack, qg, kg, rows_per_mod):
    g_, r_, _ = x.shape
    tm = min(ROW_TILE, r_)
    nt = r_ // tm
    mod_rows = 1 if rows_per_mod == 1 else tm
    mod_map = (lambda b, i: (b, 0, 0)) if rows_per_mod == 1 else (lambda b, i: (b, i, 0))
    widths = [(D_ATT, BF16), (D_KV, F32), (D_KV, F32), (D_KV, BF16), (D_KV, BF16), (IDX_HEADS * IDX_DIM, BF16),
              (IDX_DIM, F32), (IDX_DIM, BF16), (IDX_HEADS, F32), (D_SSM, F32), (CONV_DIM, F32), (LANES, F32)]
    return pl.pallas_call(
        _inproj_kernel,
        grid=(g_, nt),
        in_specs=[
            pl.BlockSpec((1, tm, D_MODEL), lambda b, i: (b, i, 0)),
            _resident((1, D_MODEL)),
            pl.BlockSpec((1, mod_rows, D_MODEL), mod_map),
            pl.BlockSpec((1, mod_rows, D_MODEL), mod_map),
            _resident((D_MODEL, N_PACK)),
            _resident((1, HEAD_DIM)),
            _resident((1, HEAD_DIM)),
        ],
        out_specs=[pl.BlockSpec((1, tm, w), lambda b, i: (b, i, 0)) for w, _ in widths],
        out_shape=[jax.ShapeDtypeStruct((g_, r_, w), dt) for w, dt in widths],
        compiler_params=_cparams("parallel", "parallel"),
        name="in_proj",
    )(x, norm_g.reshape(1, D_MODEL), sc, sh, w_pack, qg.reshape(1, HEAD_DIM), kg.reshape(1, HEAD_DIM))


def _ssd_prompt_kernel(xbc_ref, dt_ref, zg_ref, cw_ref, cb_ref, dtb_ref, alog_ref, dsk_ref, ng_ref,
                       ex_ref, ext_ref, y_o, st_o, xe_scr, st_scr):
    c = pl.program_id(1)
    q_ = SSD_CHUNK
    halo = SUBLANES

    @pl.when(c == 0)
    def _():
        xe_scr[0:halo, :] = jnp.zeros((halo, CONV_DIM), F32)
        st_scr[...] = jnp.zeros_like(st_scr)

    xe_scr[halo:halo + q_, :] = xbc_ref[0]
    conv = cb_ref[...] + sum(
        cw_ref[j:j + 1, :] * xe_scr[halo - (SSM_CONV - 1) + j:halo - (SSM_CONV - 1) + j + q_, :]
        for j in range(SSM_CONV))
    xe_scr[0:halo, :] = xe_scr[q_:q_ + halo, :]
    xc = _silu(conv)
    xs = xc[:, :D_SSM]
    bm = xc[:, D_SSM:D_SSM + SSM_GROUPS * SSM_STATE]
    cm = xc[:, D_SSM + SSM_GROUPS * SSM_STATE:]

    dt = _softplus(dt_ref[0] + dtb_ref[...])
    da = dt * (-jnp.exp(alog_ref[...]))
    row = lax.broadcasted_iota(jnp.int32, (q_, q_), 0)
    col = lax.broadcasted_iota(jnp.int32, (q_, q_), 1)
    causal = row >= col
    tri = jnp.where(causal, 1.0, 0.0).astype(BF16)
    acs = _dot_sel_lhs(tri, da)
    acs_t = acs.T
    acs_last = acs[q_ - 1:q_, :]
    ex = ex_ref[...]
    dt_x = _dot_sel_rhs(dt, ex)
    eacs_x = _dot_sel_rhs(jnp.exp(acs), ex)
    dte_x = _dot_sel_rhs(jnp.exp(acs_last - acs), ex)
    cdec = jnp.exp(jnp.broadcast_to(acs_t[:, q_ - 1:q_], (LANES, SSM_STATE)))
    cdec_rows = _dot_sel_lhs(ext_ref[...], cdec)

    xdt = xs * dt_x
    xdtd = xdt * dte_x
    gw = SSM_HEADS // SSM_GROUPS * SSM_HEAD_DIM
    gated = []
    for g in range(SSM_GROUPS):
        bg = bm[:, g * SSM_STATE:(g + 1) * SSM_STATE].astype(BF16)
        cg = cm[:, g * SSM_STATE:(g + 1) * SSM_STATE].astype(BF16)
        cb = _dot_nt(cg, bg)
        yd = []
        for hl in range(SSM_HEADS // SSM_GROUPS):
            h = g * (SSM_HEADS // SSM_GROUPS) + hl
            seg = acs[:, h:h + 1] - acs_t[h:h + 1, :]
            lmat = jnp.exp(jnp.where(causal, seg, -jnp.inf))
            yd.append(jnp.dot((cb * lmat).astype(BF16),
                              xdt[:, h * SSM_HEAD_DIM:(h + 1) * SSM_HEAD_DIM].astype(BF16),
                              preferred_element_type=F32))
        y_diag = jnp.concatenate(yd, axis=-1)
        st_old = st_scr[g * gw:(g + 1) * gw, :]
        y_off = _dot_nt(cg, st_old.astype(BF16)) * eacs_x[:, g * gw:(g + 1) * gw]
        st_new = _dot_tn(xdtd[:, g * gw:(g + 1) * gw].astype(BF16), bg)
        st_scr[g * gw:(g + 1) * gw, :] = st_old * cdec_rows[g * gw:(g + 1) * gw, :] + st_new
        yg = y_diag + y_off + dsk_ref[:, g * gw:(g + 1) * gw] * xs[:, g * gw:(g + 1) * gw]
        gated.append(yg * _silu(zg_ref[0, :, g * gw:(g + 1) * gw]))

    y_o[0] = (_rms(jnp.concatenate(gated, axis=-1)) * ng_ref[...]).astype(y_o.dtype)

    @pl.when(c == pl.num_programs(1) - 1)
    def _():
        st_o[0] = st_scr[...]


def _ssd_prompt(xbc, dt, zg, conv_w, conv_b, dtb_pad, alog_pad, dsk_x, norm_g, ex, ext):
    b_, s_, _ = xbc.shape
    nc = s_ // SSD_CHUNK
    return pl.pallas_call(
        _ssd_prompt_kernel,
        grid=(b_, nc),
        in_specs=[
            pl.BlockSpec((1, SSD_CHUNK, CONV_DIM), lambda b, c: (b, c, 0)),
            pl.BlockSpec((1, SSD_CHUNK, LANES), lambda b, c: (b, c, 0)),
            pl.BlockSpec((1, SSD_CHUNK, D_SSM), lambda b, c: (b, c, 0)),
            _resident((SSM_CONV, CONV_DIM)),
            _resident((1, CONV_DIM)),
            _resident((1, LANES)),
            _resident((1, LANES)),
            _resident((1, D_SSM)),
            _resident((1, D_SSM)),
            _resident((LANES, D_SSM)),
            _resident((D_SSM, LANES)),
        ],
        out_specs=[
            pl.BlockSpec((1, SSD_CHUNK, D_SSM), lambda b, c: (b, c, 0)),
            pl.BlockSpec((1, D_SSM, SSM_STATE), lambda b, c: (b, 0, 0)),
        ],
        out_shape=[
            jax.ShapeDtypeStruct((b_, s_, D_SSM), F32),
            jax.ShapeDtypeStruct((b_, D_SSM, SSM_STATE), F32),
        ],
        scratch_shapes=[
            pltpu.VMEM((SSD_CHUNK + SUBLANES, CONV_DIM), F32),
            pltpu.VMEM((D_SSM, SSM_STATE), F32),
        ],
        compiler_params=_cparams("parallel", "arbitrary"),
        name="ssd_prompt",
    )(xbc, dt, zg, conv_w, conv_b, dtb_pad, alog_pad, dsk_x, norm_g, ex, ext)


def _dsa_prompt_kernel(topk, qi_ref, wi_ref, ki_ref, q_ref, k_ref, v_ref, tz_ref, cf_ref, o_ref, sc_scr):
    qb = pl.program_id(1)
    tq = Q_BLOCK
    n_kb = qb + 1
    t_idx = qb * tq + lax.broadcasted_iota(jnp.int32, (tq, LANES), 0)
    lane = lax.broadcasted_iota(jnp.int32, (tq, LANES), 1)

    wcols = [wi_ref[0, :, h:h + 1] for h in range(IDX_HEADS)]
    qi_rows = jnp.concatenate([qi_ref[0, :, h * IDX_DIM:(h + 1) * IDX_DIM] for h in range(IDX_HEADS)], axis=0)

    def score_block(kb, carry):
        off = pl.multiple_of(kb * LANES, LANES)
        d = _dot_nt(qi_rows, ki_ref[0, pl.ds(off, LANES), :])
        acc = jnp.zeros((tq, LANES), F32)
        for h in range(IDX_HEADS):
            acc = acc + wcols[h] * jnp.maximum(d[h * tq:(h + 1) * tq, :], 0.0)
        sc_scr[:, pl.ds(off, LANES)] = jnp.where(kb * LANES + lane <= t_idx, acc, -jnp.inf)
        return carry

    lax.fori_loop(0, n_kb, score_block, 0)

    def tiles_reduce(fn, init):
        def body(kb, acc):
            off = pl.multiple_of(kb * LANES, LANES)
            return fn(acc, sc_scr[:, pl.ds(off, LANES)])
        return lax.fori_loop(0, n_kb, body, init)

    def count_ge(thr):
        thr_b = jnp.broadcast_to(thr, (tq, LANES))
        acc = tiles_reduce(lambda a, t: a + jnp.where(t >= thr_b, 1.0, 0.0), jnp.zeros((tq, LANES), F32))
        return jnp.sum(acc, axis=-1, keepdims=True)

    kf = float(topk)
    all_valid = n_kb * LANES <= topk

    def search():
        mx = jnp.max(tiles_reduce(jnp.maximum, jnp.full((tq, LANES), -jnp.inf, F32)), axis=-1, keepdims=True)
        mn = jnp.min(tiles_reduce(lambda a, t: jnp.minimum(a, jnp.where(t == -jnp.inf, jnp.inf, t)),
                                  jnp.full((tq, LANES), jnp.inf, F32)), axis=-1, keepdims=True)
        top_tied = count_ge(mx) >= kf
        state = (mn, mx, jnp.where(top_tied, mx, mn), jnp.where(top_tied, 1.0, 0.0), jnp.int32(0))

        def cond(s):
            return jnp.logical_and(jnp.min(s[3]) < 0.5, s[4] < 400)

        def body(s):
            lo, hi, thr, done, it = s
            mid = 0.5 * lo + 0.5 * hi
            collapsed = jnp.logical_or(mid <= lo, mid >= hi)
            cnt = count_ge(mid)
            ge = cnt >= kf
            exact = cnt == kf
            open_ = done < 0.5
            new_thr = jnp.where(collapsed, lo, jnp.where(exact, mid, thr))
            new_done = jnp.where(jnp.logical_or(collapsed, exact), 1.0, done)
            thr = jnp.where(open_, new_thr, thr)
            lo = jnp.where(jnp.logical_and(open_, ge), mid, lo)
            hi = jnp.where(jnp.logical_and(open_, jnp.logical_not(ge)), mid, hi)
            return lo, hi, thr, jnp.where(open_, new_done, done), it + 1

        return lax.while_loop(cond, body, state)[2]

    thr = lax.cond(all_valid, lambda: jnp.full((tq, 1), -jnp.inf, F32), search)

    thr_b = jnp.broadcast_to(thr, (tq, LANES))
    c_gt = jnp.sum(tiles_reduce(lambda a, t: a + jnp.where(t > thr_b, 1.0, 0.0), jnp.zeros((tq, LANES), F32)),
                   axis=-1, keepdims=True)
    need_b = jnp.broadcast_to(kf - c_gt, (tq, LANES))
    upper = jnp.where(lax.broadcasted_iota(jnp.int32, (LANES, LANES), 0)
                      <= lax.broadcasted_iota(jnp.int32, (LANES, LANES), 1), 1.0, 0.0).astype(BF16)

    def select_block(kb, run):
        off = pl.multiple_of(kb * LANES, LANES)
        t = sc_scr[:, pl.ds(off, LANES)]
        eq = jnp.logical_and(t == thr_b, t > -jnp.inf)
        eqf = jnp.where(eq, 1.0, 0.0)
        pre = jnp.dot(eqf.astype(BF16), upper, preferred_element_type=F32) + run
        sel = jnp.logical_or(t > thr_b, jnp.logical_and(eq, pre <= need_b))
        sc_scr[:, pl.ds(off, LANES)] = jnp.where(sel, 0.0, NEG_BIG)
        return run + jnp.broadcast_to(jnp.sum(eqf, axis=-1, keepdims=True), (tq, LANES))

    lax.fori_loop(0, n_kb, select_block, jnp.zeros((tq, LANES), F32))

    @pl.when(n_kb % 2 == 1)
    def _():
        off = pl.multiple_of(n_kb * LANES, LANES)
        sc_scr[:, pl.ds(off, LANES)] = jnp.full((tq, LANES), NEG_BIG, F32)

    scale = HEAD_DIM ** -0.5
    n_chunks = (n_kb + 1) // 2
    n_far = jnp.maximum((qb - 1) // 2, 0)
    for g in range(N_KV_HEADS):
        h0 = g * KV_REP
        qg = jnp.concatenate([q_ref[0, :, (h0 + r) * HEAD_DIM:(h0 + r + 1) * HEAD_DIM] for r in range(KV_REP)], axis=0)
        rows = KV_REP * tq

        def step(j, carry, bias_fn):
            m, l, acc = carry
            off = pl.multiple_of(j * ATT_CHUNK, ATT_CHUNK)
            kc = k_ref[0, pl.ds(off, ATT_CHUNK), g * HEAD_DIM:(g + 1) * HEAD_DIM]
            vc = v_ref[0, pl.ds(off, ATT_CHUNK), g * HEAD_DIM:(g + 1) * HEAD_DIM]
            selb = sc_scr[:, pl.ds(off, ATT_CHUNK)]
            s = _dot_nt(qg, kc) * scale + bias_fn(j) + jnp.concatenate([selb] * KV_REP, axis=0)
            m_new = jnp.maximum(m, jnp.max(s, axis=-1, keepdims=True))
            alpha = jnp.exp(m - m_new)
            p = jnp.exp(s - m_new)
            l = alpha * l + jnp.sum(p, axis=-1, keepdims=True)
            acc = alpha * acc + jnp.dot(p.astype(BF16), vc, preferred_element_type=F32)
            return m_new, l, acc

        def far_bias(j):
            return jnp.concatenate([jnp.full((tq, 1), cf_ref[h0 + r], F32) for r in range(KV_REP)], axis=0)

        def near_bias(j):
            ti = (qb * tq - j * ATT_CHUNK) // tq
            return jnp.concatenate([tz_ref[ti, h0 + r] for r in range(KV_REP)], axis=0)

        init = (jnp.full((rows, 1), NEG_BIG, F32), jnp.zeros((rows, 1), F32), jnp.zeros((rows, HEAD_DIM), F32))
        carry = lax.fori_loop(0, n_far, functools.partial(step, bias_fn=far_bias), init)
        m, l, acc = lax.fori_loop(n_far, n_chunks, functools.partial(step, bias_fn=near_bias), carry)
        out = acc * (1.0 / l)
        for r in range(KV_REP):
            o_ref[0, :, (h0 + r) * HEAD_DIM:(h0 + r + 1) * HEAD_DIM] = out[r * tq:(r + 1) * tq, :].astype(o_ref.dtype)


def _dsa_prompt(qi, wi, ki_b, q_b, k_b, v_b, tz, cfar, topk):
    b_, s_, _ = q_b.shape
    nq = s_ // Q_BLOCK
    assert topk % LANES == 0 and s_ % ATT_CHUNK == 0
    return pl.pallas_call(
        functools.partial(_dsa_prompt_kernel, topk),
        grid=(b_, nq),
        in_specs=[
            pl.BlockSpec((1, Q_BLOCK, IDX_HEADS * IDX_DIM), lambda b, i: (b, i, 0)),
            pl.BlockSpec((1, Q_BLOCK, IDX_HEADS), lambda b, i: (b, i, 0)),
            pl.BlockSpec((1, s_, IDX_DIM), lambda b, i: (b, 0, 0)),
            pl.BlockSpec((1, Q_BLOCK, D_ATT), lambda b, i: (b, i, 0)),
            pl.BlockSpec((1, s_, D_KV), lambda b, i: (b, 0, 0)),
            pl.BlockSpec((1, s_, D_KV), lambda b, i: (b, 0, 0)),
            _resident(tz.shape),
            pl.BlockSpec(memory_space=pltpu.SMEM),
        ],
        out_specs=pl.BlockSpec((1, Q_BLOCK, D_ATT), lambda b, i: (b, i, 0)),
        out_shape=jax.ShapeDtypeStruct((b_, s_, D_ATT), BF16),
        scratch_shapes=[pltpu.VMEM((Q_BLOCK, s_), F32)],
        compiler_params=_cparams("parallel", "arbitrary"),
        name="dsa_prompt",
    )(qi, wi, ki_b, q_b, k_b, v_b, tz, cfar)


def _mix_and_norm(att_ref, ssd_ref, x_ref, g1_ref, sc2_ref, sh2_ref, n2_ref, wo_ref):
    o = (jnp.dot(att_ref[0].astype(BF16), wo_ref[:D_ATT, :], preferred_element_type=F32)
         + jnp.dot(ssd_ref[0].astype(BF16), wo_ref[D_ATT:, :], preferred_element_type=F32))
    x1 = x_ref[0] + g1_ref[0] * o
    h2 = ((_rms(x1) * n2_ref[...]) * (1.0 + sc2_ref[0]) + sh2_ref[0]).astype(BF16)
    return x1, h2


def _ffn_prompt_kernel(att_ref, ssd_ref, x_ref, g1_ref, sc2_ref, sh2_ref, g2_ref, n2_ref, wo_ref, wu_ref,
                       cw_ref, cb_ref, wd_ref, y_o, tail_o, acc_scr, ug_scr, uv_scr, carry_scr):
    i = pl.program_id(1)
    tm = x_ref.shape[1]
    halo = SUBLANES

    @pl.when(i == 0)
    def _():
        carry_scr[...] = jnp.zeros_like(carry_scr)

    x1, h2 = _mix_and_norm(att_ref, ssd_ref, x_ref, g1_ref, sc2_ref, sh2_ref, n2_ref, wo_ref)
    acc_scr[...] = jnp.zeros_like(acc_scr)

    def conv_cols(scr, c0):
        u = jnp.dot(h2, wu_ref[:, c0:c0 + FF_CHUNK], preferred_element_type=F32)
        scr[0:halo, :] = carry_scr[:, c0:c0 + FF_CHUNK]
        scr[halo:halo + tm, :] = u
        carry_scr[:, c0:c0 + FF_CHUNK] = u[tm - halo:, :]
        return cb_ref[:, c0:c0 + FF_CHUNK] + sum(
            cw_ref[j:j + 1, c0:c0 + FF_CHUNK] * scr[halo - (FFN_CONV - 1) + j:halo - (FFN_CONV - 1) + j + tm, :]
            for j in range(FFN_CONV))

    for jc in range(D_FF // FF_CHUNK):
        gate = conv_cols(ug_scr, jc * FF_CHUNK)
        val = conv_cols(uv_scr, D_FF + jc * FF_CHUNK)
        a = (_silu(gate) * val).astype(BF16)
        acc_scr[...] += jnp.dot(a, wd_ref[jc * FF_CHUNK:(jc + 1) * FF_CHUNK, :], preferred_element_type=F32)

    y_o[0] = x1 + g2_ref[0] * acc_scr[...]
    tail_o[0] = carry_scr[...]


def _ffn_prompt(att, ssd, x, g1, sc2, sh2, g2, norm2_g, wo_b, wu_b, cw, cb, wd_b):
    b_, s_, _ = x.shape
    tm = min(ROW_TILE, s_)
    nt = s_ // tm
    row = lambda w: pl.BlockSpec((1, tm, w), lambda b, i: (b, i, 0))
    mod = pl.BlockSpec((1, 1, D_MODEL), lambda b, i: (b, 0, 0))
    return pl.pallas_call(
        _ffn_prompt_kernel,
        grid=(b_, nt),
        in_specs=[row(D_ATT), row(D_SSM), row(D_MODEL), mod, mod, mod, mod, _resident((1, D_MODEL)),
                  _resident((D_MIX, D_MODEL)), _resident((D_MODEL, 2 * D_FF)), _resident((FFN_CONV, 2 * D_FF)),
                  _resident((1, 2 * D_FF)), _resident((D_FF, D_MODEL))],
        out_specs=[row(D_MODEL), pl.BlockSpec((1, SUBLANES, 2 * D_FF), lambda b, i: (b, 0, 0))],
        out_shape=[jax.ShapeDtypeStruct((b_, s_, D_MODEL), F32),
                   jax.ShapeDtypeStruct((b_, SUBLANES, 2 * D_FF), F32)],
        scratch_shapes=[pltpu.VMEM((tm, D_MODEL), F32), pltpu.VMEM((tm + SUBLANES, FF_CHUNK), F32),
                        pltpu.VMEM((tm + SUBLANES, FF_CHUNK), F32), pltpu.VMEM((SUBLANES, 2 * D_FF), F32)],
        compiler_params=_cparams("parallel", "arbitrary"),
        name="ffn_prompt",
    )(att, ssd, x, g1, sc2, sh2, g2, norm2_g.reshape(1, D_MODEL), wo_b, wu_b, cw, cb, wd_b)


def _ffn_sample_kernel(att_ref, ssd_ref, x_ref, g1_ref, sc2_ref, sh2_ref, g2_ref, n2_ref, wo_ref, wu_ref,
                       cw_ref, cb_ref, wd_ref, h0_ref, h1_ref, y_o, u_o, acc_scr):
    x1, h2 = _mix_and_norm(att_ref, ssd_ref, x_ref, g1_ref, sc2_ref, sh2_ref, n2_ref, wo_ref)
    acc_scr[...] = jnp.zeros_like(acc_scr)

    def conv_cols(c0):
        u = jnp.dot(h2, wu_ref[:, c0:c0 + FF_CHUNK], preferred_element_type=F32)
        u_o[:, c0:c0 + FF_CHUNK] = u
        sl = slice(c0, c0 + FF_CHUNK)
        return (cb_ref[:, sl] + cw_ref[0:1, sl] * h0_ref[:, sl] + cw_ref[1:2, sl] * h1_ref[:, sl]
                + cw_ref[2:3, sl] * u)

    for jc in range(D_FF // FF_CHUNK):
        gate = conv_cols(jc * FF_CHUNK)
        val = conv_cols(D_FF + jc * FF_CHUNK)
        a = (_silu(gate) * val).astype(BF16)
        acc_scr[...] += jnp.dot(a, wd_ref[jc * FF_CHUNK:(jc + 1) * FF_CHUNK, :], preferred_element_type=F32)

    y_o[0] = x1 + g2_ref[0] * acc_scr[...]


def _ffn_sample(att, ssd, x, g1, sc2, sh2, g2, norm2_g, wo_b, wu_b, cw, cb, wd_b, hist0, hist1):
    n = x.shape[1]
    full = lambda *shape: pl.BlockSpec(shape, lambda: (0,) * len(shape))
    r3 = lambda w: full(1, n, w)
    return pl.pallas_call(
        _ffn_sample_kernel,
        in_specs=[r3(D_ATT), r3(D_SSM), r3(D_MODEL), r3(D_MODEL), r3(D_MODEL), r3(D_MODEL), r3(D_MODEL),
                  full(1, D_MODEL), full(D_MIX, D_MODEL), full(D_MODEL, 2 * D_FF), full(FFN_CONV, 2 * D_FF),
                  full(1, 2 * D_FF), full(D_FF, D_MODEL), full(n, 2 * D_FF), full(n, 2 * D_FF)],
        out_specs=[r3(D_MODEL), full(n, 2 * D_FF)],
        out_shape=[jax.ShapeDtypeStruct((1, n, D_MODEL), F32), jax.ShapeDtypeStruct((n, 2 * D_FF), F32)],
        scratch_shapes=[pltpu.VMEM((n, D_MODEL), F32)],
        compiler_params=pltpu.CompilerParams(vmem_limit_bytes=VMEM_LIMIT),
        name="ffn_sample",
    )(att, ssd, x, g1, sc2, sh2, g2, norm2_g.reshape(1, D_MODEL), wo_b, wu_b, cw, cb, wd_b, hist0, hist1)


def _ssd_sample_kernel(xbc_ref, h0_ref, h1_ref, h2_ref, dt_ref, zg_ref, cw_ref, cb_ref, dtb_ref, alog_ref,
                       dsk_ref, ng_ref, ex_ref, st_ref, y_o, st_o):
    nb = xbc_ref.shape[0]
    conv = (cb_ref[...] + cw_ref[0:1, :] * h0_ref[...] + cw_ref[1:2, :] * h1_ref[...]
            + cw_ref[2:3, :] * h2_ref[...] + cw_ref[3:4, :] * xbc_ref[...])
    xc = _silu(conv)
    xs = xc[:, :D_SSM]
    bm = xc[:, D_SSM:D_SSM + SSM_GROUPS * SSM_STATE]
    cm = xc[:, D_SSM + SSM_GROUPS * SSM_STATE:]
    dt = _softplus(dt_ref[...] + dtb_ref[...])
    dec = jnp.exp(dt * (-jnp.exp(alog_ref[...])))
    ex = ex_ref[...]
    xdt = xs * _dot_sel_rhs(dt, ex)
    dec_x = _dot_sel_rhs(dec, ex)
    gw = SSM_HEADS // SSM_GROUPS * SSM_HEAD_DIM
    ones_n = jnp.ones((SUBLANES, SSM_STATE), BF16)
    row0 = lax.broadcasted_iota(jnp.int32, (SUBLANES, 1), 0) == 0

    def only_row0(v):
        return jnp.where(row0, jnp.broadcast_to(v, (SUBLANES, v.shape[1])), 0.0)

    ys = []
    for i in range(nb):
        yrow = []
        for g in range(SSM_GROUPS):
            hs = st_ref[i, g * gw:(g + 1) * gw, :]
            dec_rows = only_row0(dec_x[i:i + 1, g * gw:(g + 1) * gw])
            xdt_rows = only_row0(xdt[i:i + 1, g * gw:(g + 1) * gw])
            b_rows = only_row0(bm[i:i + 1, g * SSM_STATE:(g + 1) * SSM_STATE])
            c_rows = only_row0(cm[i:i + 1, g * SSM_STATE:(g + 1) * SSM_STATE])
            dec_b = sum(_dot_tn(p, ones_n) for p in _split3(dec_rows))
            upd = sum(_dot_tn(p, q) for p in _split3(xdt_rows)[:2] for q in _split3(b_rows)[:2])
            hn = hs * dec_b + upd
            st_o[i, g * gw:(g + 1) * gw, :] = hn
            yg = sum(_dot_nt(p, q) for p in _split3(c_rows)[:2] for q in _split3(hn)[:2])
            yrow.append(yg[0:1, :])
        ys.append(jnp.concatenate(yrow, axis=-1))
    y = jnp.concatenate(ys, axis=0) + dsk_ref[...] * xs
    y = y * _silu(zg_ref[...])
    y_o[...] = _rms(y) * ng_ref[...]


def _ssd_sample(xbc, hist, dt, zg, conv_w, conv_b, dtb_pad, alog_pad, dsk_x, norm_g, ex, state):
    n = xbc.shape[0]
    nb = SUBLANES
    rows = lambda w: pl.BlockSpec((nb, w), lambda i: (i, 0))
    st = pl.BlockSpec((nb, D_SSM, SSM_STATE), lambda i: (i, 0, 0))
    return pl.pallas_call(
        _ssd_sample_kernel,
        grid=(n // nb,),
        in_specs=[rows(CONV_DIM), rows(CONV_DIM), rows(CONV_DIM), rows(CONV_DIM), rows(LANES), rows(D_SSM),
                  _resident((SSM_CONV, CONV_DIM)), _resident((1, CONV_DIM)), _resident((1, LANES)),
                  _resident((1, LANES)), _resident((1, D_SSM)), _resident((1, D_SSM)), _resident((LANES, D_SSM)), st],
        out_specs=[rows(D_SSM), st],
        out_shape=[jax.ShapeDtypeStruct((n, D_SSM), F32), jax.ShapeDtypeStruct((n, D_SSM, SSM_STATE), F32)],
        compiler_params=_cparams("parallel"),
        name="ssd_sample",
    )(xbc, hist[0], hist[1], hist[2], dt, zg, conv_w, conv_b, dtb_pad, alog_pad, dsk_x, norm_g, ex, state)


def _idx_sample_kernel(pt_ref, qi_ref, wi_ref, kin_ref, pool_ref, o_ref, kbuf, sem):
    b = pl.program_id(0)
    nb = pl.num_programs(0)
    n_pages = kbuf.shape[1]
    page = kbuf.shape[2]

    def page_copy(seq, slot, j):
        return pltpu.make_async_copy(pool_ref.at[pt_ref[seq, j]], kbuf.at[slot, j], sem.at[slot])

    def fetch(seq, slot):
        def body(j, c):
            page_copy(seq, slot, j).start()
            return c
        lax.fori_loop(0, n_pages, body, 0)

    @pl.when(b == 0)
    def _():
        fetch(0, 0)

    @pl.when(b + 1 < nb)
    def _():
        fetch(b + 1, (b + 1) % 2)

    slot = b % 2

    def wait_body(j, c):
        page_copy(b, slot, j).wait()
        return c
    lax.fori_loop(0, n_pages, wait_body, 0)

    qi = qi_ref[0]
    w = wi_ref[0]
    keys = kbuf[slot].reshape(n_pages * page, IDX_DIM).astype(BF16)
    d = _dot_nt(qi, keys)
    o_ref[0, :, 0:n_pages * page] = jnp.sum(w * jnp.maximum(d, 0.0), axis=0, keepdims=True)
    knew = kin_ref[0].astype(BF16).astype(F32)
    dn = jnp.sum(qi.astype(F32) * knew, axis=-1, keepdims=True)
    s_new = jnp.sum(w * jnp.maximum(dn, 0.0), axis=0, keepdims=True)
    lane = lax.broadcasted_iota(jnp.int32, (1, LANES), 1)
    o_ref[0, :, n_pages * page:] = jnp.where(lane == 0, s_new, -jnp.inf)


def _idx_sample(page_table, qi3, wi3, ki_new3, pool):
    n, n_pages = page_table.shape
    page = pool.shape[1]
    past = n_pages * page
    grid_spec = pltpu.PrefetchScalarGridSpec(
        num_scalar_prefetch=1,
        grid=(n,),
        in_specs=[
            pl.BlockSpec((1, IDX_HEADS, IDX_DIM), lambda b, pt: (b, 0, 0)),
            pl.BlockSpec((1, IDX_HEADS, 1), lambda b, pt: (b, 0, 0)),
            pl.BlockSpec((1, 1, IDX_DIM), lambda b, pt: (b, 0, 0)),
            pl.BlockSpec(memory_space=pl.ANY),
        ],
        out_specs=pl.BlockSpec((1, 1, past + LANES), lambda b, pt: (b, 0, 0)),
        scratch_shapes=[pltpu.VMEM((2, n_pages, page, IDX_DIM), F32), pltpu.SemaphoreType.DMA((2,))],
    )
    return pl.pallas_call(
        _idx_sample_kernel,
        grid_spec=grid_spec,
        out_shape=jax.ShapeDtypeStruct((n, 1, past + LANES), F32),
        compiler_params=_cparams("arbitrary"),
        name="idx_sample",
    )(page_table, qi3, wi3, ki_new3, pool)


def _att_sample_kernel(q_ref, k_ref, v_ref, bias_ref, o_ref):
    nb = q_ref.shape[0]
    scale = HEAD_DIM ** -0.5
    head_group = lax.broadcasted_iota(jnp.int32, (N_HEADS, 1), 0) // KV_REP
    for i in range(nb):
        q = q_ref[i].astype(BF16)
        bias = bias_ref[i]
        out = jnp.zeros((N_HEADS, HEAD_DIM), F32)
        for g in range(N_KV_HEADS):
            kg = k_ref[i, :, g * HEAD_DIM:(g + 1) * HEAD_DIM].astype(BF16)
            vg = v_ref[i, :, g * HEAD_DIM:(g + 1) * HEAD_DIM].astype(BF16)
            s = _dot_nt(q, kg) * scale + bias
            m = jnp.max(s, axis=-1, keepdims=True)
            p = jnp.exp(s - m)
            p = p * (1.0 / jnp.sum(p, axis=-1, keepdims=True))
            og = jnp.dot(p.astype(BF16), vg, preferred_element_type=F32)
            out = jnp.where(head_group == g, og, out)
        o_ref[i] = out


def _att_sample(q3, k_sel, v_sel, bias):
    n, kk, _ = k_sel.shape
    nb = SUBLANES
    return pl.pallas_call(
        _att_sample_kernel,
        grid=(n // nb,),
        in_specs=[
            pl.BlockSpec((nb, N_HEADS, HEAD_DIM), lambda i: (i, 0, 0)),
            pl.BlockSpec((nb, kk, D_KV), lambda i: (i, 0, 0)),
            pl.BlockSpec((nb, kk, D_KV), lambda i: (i, 0, 0)),
            pl.BlockSpec((nb, N_HEADS, kk), lambda i: (i, 0, 0)),
        ],
        out_specs=pl.BlockSpec((nb, N_HEADS, HEAD_DIM), lambda i: (i, 0, 0)),
        out_shape=jax.ShapeDtypeStruct((n, N_HEADS, HEAD_DIM), F32),
        compiler_params=_cparams("parallel"),
        name="att_sample",
    )(q3, k_sel, v_sel, bias)


def _pack_w_in(w_in):
    offs = np.cumsum((0,) + IN_SPLITS)
    parts = [w_in[:, offs[i]:offs[i + 1]] for i in range(len(IN_SPLITS))]
    pad = lambda a: jnp.pad(a, ((0, 0), (0, LANES - a.shape[1])))
    q, k, v, qi, ki, wi, zg, xbc, dt = parts
    return jnp.concatenate([q, k, v, qi, pad(ki), pad(wi), zg, xbc, pad(dt)], axis=1).astype(BF16)


def _t5_bucket(dist):
    dist = jnp.maximum(dist, 0)
    max_exact = N_BUCKETS // 2
    d = jnp.maximum(dist, 1).astype(F32)
    large = max_exact + (jnp.log(d / max_exact) / math.log(MAX_DISTANCE / max_exact) * (N_BUCKETS - max_exact)).astype(jnp.int32)
    large = jnp.minimum(large, N_BUCKETS - 1)
    return jnp.where(dist < max_exact, dist, large)


def _head_expand():
    h = jnp.arange(LANES, dtype=jnp.int32)[:, None]
    c = jnp.arange(D_SSM, dtype=jnp.int32)[None, :]
    ex = (c // SSM_HEAD_DIM == h).astype(BF16)
    return ex, ex.T


def _pad_lanes(v):
    return jnp.pad(v.astype(F32), (0, LANES - v.shape[0])).reshape(1, LANES)


def kernel(x_prompt, x_sample, cache_k, cache_v, cache_kidx, state_ssm, state_conv_ssd, state_conv_ffn, page_table,
           c_prompt, c_sample, rel_bias, norm1_g, w_ada, b_ada, w_in, q_norm_g, k_norm_g, conv_ssd_w, conv_ssd_b,
           dt_bias, a_log, d_skip, ssd_norm_g, w_out, norm2_g, w_up, conv_ffn_w, conv_ffn_b, w_down):
    depth = w_in.shape[0]
    assert depth == 1
    bp, s_, _ = x_prompt.shape
    ns = x_sample.shape[0]
    assert x_sample.shape[1] == 1
    lyr = 0
    n_pages = page_table.shape[1]
    page = cache_k.shape[2]
    past = n_pages * page
    topk_p = min(TOPK_MAX, s_ // TOPK_DIV)
    topk_s = min(TOPK_MAX, (past + 1) // TOPK_DIV)

    w_pack = _pack_w_in(w_in[lyr])
    wo_b = w_out[lyr].astype(BF16)
    wu_b = w_up[lyr].astype(BF16)
    wd_b = w_down[lyr].astype(BF16)
    ex, ext = _head_expand()
    dtb_pad = _pad_lanes(dt_bias[lyr])
    alog_pad = _pad_lanes(a_log[lyr])
    dsk_x = jnp.repeat(d_skip[lyr].astype(F32), SSM_HEAD_DIM).reshape(1, D_SSM)
    ssd_g = ssd_norm_g[lyr].reshape(1, D_SSM)
    cw_ssd, cb_ssd = conv_ssd_w[lyr], conv_ssd_b[lyr].reshape(1, CONV_DIM)
    cw_ffn, cb_ffn = conv_ffn_w[lyr], conv_ffn_b[lyr].reshape(1, 2 * D_FF)
    dist = (jnp.arange(3, dtype=jnp.int32)[:, None, None] * Q_BLOCK
            + jnp.arange(Q_BLOCK, dtype=jnp.int32)[None, :, None]
            - jnp.arange(ATT_CHUNK, dtype=jnp.int32)[None, None, :])
    tz = jnp.transpose(rel_bias[_t5_bucket(dist)].astype(F32), (0, 3, 1, 2))
    cfar = rel_bias[_t5_bucket(jnp.int32(MAX_DISTANCE))].astype(F32)

    n_c = bp + ns
    c_all = jnp.pad(jnp.concatenate([c_prompt, c_sample], axis=0), ((0, (-n_c) % SUBLANES), (0, 0)))
    mod = _ada(c_all, w_ada[lyr], b_ada[lyr])
    mods_p = [mod[:bp, i * D_MODEL:(i + 1) * D_MODEL].reshape(bp, 1, D_MODEL) for i in range(6)]
    mods_s = [mod[bp:bp + ns, i * D_MODEL:(i + 1) * D_MODEL].reshape(1, ns, D_MODEL) for i in range(6)]

    sh1, sc1, g1, sh2, sc2, g2 = mods_p
    (q_b, k_f, v_f, k_b, v_b, qi_b, ki_f, ki_b, wi_f, zg, xbc, dt) = _inproj(
        x_prompt, norm1_g[lyr], sc1, sh1, w_pack, q_norm_g[lyr], k_norm_g[lyr], rows_per_mod=1)
    att_p = _dsa_prompt(qi_b, wi_f, ki_b, q_b, k_b, v_b, tz, cfar, topk_p)
    ssd_p, ssm_p = _ssd_prompt(xbc, dt, zg, cw_ssd, cb_ssd, dtb_pad, alog_pad, dsk_x, ssd_g, ex, ext)
    y_p, tail_p = _ffn_prompt(att_p, ssd_p, x_prompt, g1, sc2, sh2, g2, norm2_g[lyr], wo_b, wu_b, cw_ffn, cb_ffn, wd_b)

    k_p = k_f.reshape(1, bp, s_, N_KV_HEADS, HEAD_DIM)
    v_p = v_f.reshape(1, bp, s_, N_KV_HEADS, HEAD_DIM)
    kidx_p = ki_f.reshape(1, bp, s_, IDX_DIM)
    ssm_p = ssm_p.reshape(1, bp, SSM_HEADS, SSM_HEAD_DIM, SSM_STATE)
    cssd_p = xbc[:, s_ - (SSM_CONV - 1):, :].reshape(1, bp, SSM_CONV - 1, CONV_DIM)
    cffn_p = tail_p[:, SUBLANES - (FFN_CONV - 1):, :].reshape(1, bp, FFN_CONV - 1, 2 * D_FF)

    sh1, sc1, g1, sh2, sc2, g2 = mods_s
    xs3 = x_sample.reshape(1, ns, D_MODEL)
    (q_b, k_f, v_f, _, _, qi_b, ki_f, _, wi_f, zg, xbc, dt) = _inproj(
        xs3, norm1_g[lyr], sc1, sh1, w_pack, q_norm_g[lyr], k_norm_g[lyr], rows_per_mod=ns)
    k_new, v_new, ki_new = k_f[0], v_f[0], ki_f[0]

    scores = _idx_sample(page_table, qi_b[0].reshape(ns, IDX_HEADS, IDX_DIM), wi_f[0].reshape(ns, IDX_HEADS, 1),
                         ki_new.reshape(ns, 1, IDX_DIM), cache_kidx[lyr])[:, 0, :]
    _, idx = lax.top_k(scores, topk_s)
    in_past = idx < past
    pidx = jnp.minimum(idx, past - 1)
    phys = jnp.take_along_axis(page_table, pidx // page, axis=1)
    off = pidx % page
    k_sel = jnp.where(in_past[..., None], cache_k[lyr][phys, off].reshape(ns, topk_s, D_KV), k_new[:, None, :])
    v_sel = jnp.where(in_past[..., None], cache_v[lyr][phys, off].reshape(ns, topk_s, D_KV), v_new[:, None, :])
    bias = jnp.transpose(rel_bias[_t5_bucket(past - idx)].astype(F32), (0, 2, 1))
    att_s = _att_sample(q_b[0].astype(F32).reshape(ns, N_HEADS, HEAD_DIM), k_sel, v_sel, bias).reshape(1, ns, D_ATT)

    hist_ssd = state_conv_ssd[lyr]
    ssd_s, ssm_s = _ssd_sample(xbc[0], [hist_ssd[:, j, :] for j in range(SSM_CONV - 1)], dt[0], zg[0], cw_ssd, cb_ssd,
                               dtb_pad, alog_pad, dsk_x, ssd_g, ex,
                               state_ssm[lyr].reshape(ns, D_SSM, SSM_STATE))
    hist_ffn = state_conv_ffn[lyr]
    y_s, u_s = _ffn_sample(att_s, ssd_s.reshape(1, ns, D_SSM), xs3, g1, sc2, sh2, g2, norm2_g[lyr], wo_b, wu_b,
                           cw_ffn, cb_ffn, wd_b, hist_ffn[:, 0, :], hist_ffn[:, 1, :])

    k_s = k_new.reshape(1, ns, 1, N_KV_HEADS, HEAD_DIM)
    v_s = v_new.reshape(1, ns, 1, N_KV_HEADS, HEAD_DIM)
    kidx_s = ki_new.reshape(1, ns, 1, IDX_DIM)
    ssm_s = ssm_s.reshape(1, ns, SSM_HEADS, SSM_HEAD_DIM, SSM_STATE)
    cssd_s = jnp.concatenate([hist_ssd[:, 1:, :], xbc[0][:, None, :]], axis=1)[None]
    cffn_s = jnp.stack([hist_ffn[:, 1, :], u_s], axis=1)[None]

    return (y_p, y_s.reshape(ns, 1, D_MODEL), k_p, v_p, kidx_p, ssm_p, cssd_p, cffn_p,
            k_s, v_s, kidx_s, ssm_s, cssd_s, cffn_s)
```

```python
import functools
import math

import jax
import jax.numpy as jnp
import numpy as np
from jax import lax
from jax.experimental import pallas as pl
from jax.experimental.pallas import tpu as pltpu

F32 = jnp.float32
BF16 = jnp.bfloat16

D_MODEL = 1024
N_HEADS = 8
N_KV_HEADS = 4
HEAD_DIM = 128
KV_REP = N_HEADS // N_KV_HEADS
IDX_HEADS = 8
IDX_DIM = 64
TOPK_MAX = 256
TOPK_DIV = 4
N_BUCKETS = 32
MAX_DISTANCE = 128
SSM_HEADS = 16
SSM_HEAD_DIM = 64
SSM_GROUPS = 4
SSM_STATE = 128
SSM_CONV = 4
SSD_CHUNK = 128
D_ATT = N_HEADS * HEAD_DIM
D_KV = N_KV_HEADS * HEAD_DIM
D_SSM = SSM_HEADS * SSM_HEAD_DIM
D_MIX = D_ATT + D_SSM
CONV_DIM = D_SSM + 2 * SSM_GROUPS * SSM_STATE
D_FF = 2816
FFN_CONV = 3
EPS = 1e-6
IN_SPLITS = (D_ATT, D_KV, D_KV, IDX_HEADS * IDX_DIM, IDX_DIM, IDX_HEADS, D_SSM, CONV_DIM, SSM_HEADS)

LANES = 128
SUBLANES = 8
VMEM_LIMIT = 52 * 1024 * 1024

OFF_Q = 0
OFF_K = OFF_Q + D_ATT
OFF_V = OFF_K + D_KV
OFF_QI = OFF_V + D_KV
OFF_KI = OFF_QI + IDX_HEADS * IDX_DIM
OFF_WI = OFF_KI + LANES
OFF_ZG = OFF_WI + LANES
OFF_XBC = OFF_ZG + D_SSM
OFF_DT = OFF_XBC + CONV_DIM
N_PACK = OFF_DT + LANES

NEG_BIG = -1e30
ATT_CHUNK = 256
FF_CHUNK = 256
ROW_TILE = 256
Q_BLOCK = 256


def _cparams(*sem):
    return pltpu.CompilerParams(dimension_semantics=sem, vmem_limit_bytes=VMEM_LIMIT)


def _resident(shape):
    nd = len(shape)
    return pl.BlockSpec(shape, lambda *_: (0,) * nd, pipeline_mode=pl.Buffered(1))


def _silu(x):
    return x * (1.0 / (1.0 + jnp.exp(-x)))


def _softplus(x):
    return jnp.maximum(x, 0.0) + jnp.log1p(jnp.exp(-jnp.abs(x)))


def _rms(x):
    return x * lax.rsqrt(jnp.mean(x * x, axis=-1, keepdims=True) + EPS)


def _split3(x):
    a = x.astype(BF16)
    r = x - a.astype(F32)
    b = r.astype(BF16)
    c = (r - b.astype(F32)).astype(BF16)
    return a, b, c


def _dot_sel_rhs(x, sel):
    return sum(jnp.dot(p, sel, preferred_element_type=F32) for p in _split3(x))


def _dot_sel_lhs(sel, x):
    return sum(jnp.dot(sel, p, preferred_element_type=F32) for p in _split3(x))


def _dot_nt(a, b):
    return lax.dot_general(a, b, (((1,), (1,)), ((), ())), preferred_element_type=F32)


def _dot_tn(a, b):
    return lax.dot_general(a, b, (((0,), (0,)), ((), ())), preferred_element_type=F32)


def _ada_kernel(c_ref, w_ref, b_ref, o_ref):
    s = _silu(c_ref[...]).astype(BF16)
    o_ref[...] = jnp.dot(s, w_ref[...].astype(BF16), preferred_element_type=F32) + b_ref[...]


def _ada(c_all, w_ada, b_ada):
    rows = c_all.shape[0]
    n = w_ada.shape[1]
    return pl.pallas_call(
        _ada_kernel,
        grid=(n // D_MODEL,),
        in_specs=[
            pl.BlockSpec((rows, D_MODEL), lambda j: (0, 0)),
            pl.BlockSpec((D_MODEL, D_MODEL), lambda j: (0, j)),
            pl.BlockSpec((1, D_MODEL), lambda j: (0, j)),
        ],
        out_specs=pl.BlockSpec((rows, D_MODEL), lambda j: (0, j)),
        out_shape=jax.ShapeDtypeStruct((rows, n), F32),
        compiler_params=_cparams("arbitrary"),
        name="ada_mod",
    )(c_all, w_ada, b_ada.reshape(1, n))


def _inproj_kernel(x_ref, g_ref, sc_ref, sh_ref, w_ref, qg_ref, kg_ref,
                   q_o, k_o, v_o, kb_o, vb_o, qi_o, ki_o, kib_o, wi_o, zg_o, xbc_o, dt_o):
    x = x_ref[0]
    h = _rms(x) * g_ref[...]
    hb = (h * (1.0 + sc_ref[0]) + sh_ref[0]).astype(BF16)

    def sec(a, b):
        return jnp.dot(hb, w_ref[:, a:b], preferred_element_type=F32)

    zq = sec(OFF_Q, OFF_K)
    for i in range(N_HEADS):
        zh = zq[:, i * HEAD_DIM:(i + 1) * HEAD_DIM]
        q_o[0, :, i * HEAD_DIM:(i + 1) * HEAD_DIM] = (_rms(zh) * qg_ref[...]).astype(BF16)
    zk = sec(OFF_K, OFF_V)
    for i in range(N_KV_HEADS):
        zh = zk[:, i * HEAD_DIM:(i + 1) * HEAD_DIM]
        kn = _rms(zh) * kg_ref[...]
        k_o[0, :, i * HEAD_DIM:(i + 1) * HEAD_DIM] = kn
        kb_o[0, :, i * HEAD_DIM:(i + 1) * HEAD_DIM] = kn.astype(BF16)
    zv = sec(OFF_V, OFF_QI)
    v_o[0] = zv
    vb_o[0] = zv.astype(BF16)
    qi_o[0] = sec(OFF_QI, OFF_KI).astype(BF16)
    zki = sec(OFF_KI, OFF_WI)[:, :IDX_DIM]
    ki_o[0] = zki
    kib_o[0] = zki.astype(BF16)
    wi_o[0] = sec(OFF_WI, OFF_ZG)[:, :IDX_HEADS]
    zg_o[0] = sec(OFF_ZG, OFF_XBC)
    xbc_o[0] = sec(OFF_XBC, OFF_DT)
    dt_o[0] = sec(OFF_DT, N_PACK)


def _inproj(x, norm_g, sc, sh, w_pack, qg, kg, rows_per_mod):
    g_, r_, _ = x.shape
    tm = min(ROW_TILE, r_)
    nt = r_ // tm
    mod_rows = 1 if rows_per_mod == 1 else tm
    mod_map = (lambda b, i: (b, 0, 0)) if rows_per_mod == 1 else (lambda b, i: (b, i, 0))
    widths = [(D_ATT, BF16), (D_KV, F32), (D_KV, F32), (D_KV, BF16), (D_KV, BF16), (IDX_HEADS * IDX_DIM, BF16),
              (IDX_DIM, F32), (IDX_DIM, BF16), (IDX_HEADS, F32), (D_SSM, F32), (CONV_DIM, F32), (LANES, F32)]
    return pl.pallas_call(
        _inproj_kernel,
        grid=(g_, nt),
        in_specs=[
            pl.BlockSpec((1, tm, D_MODEL), lambda b, i: (b, i, 0)),
            _resident((1, D_MODEL)),
            pl.BlockSpec((1, mod_rows, D_MODEL), mod_map),
            pl.BlockSpec((1, mod_rows, D_MODEL), mod_map),
            _resident((D_MODEL, N_PACK)),
            _resident((1, HEAD_DIM)),
            _resident((1, HEAD_DIM)),
        ],
        out_specs=[pl.BlockSpec((1, tm, w), lambda b, i: (b, i, 0)) for w, _ in widths],
        out_shape=[jax.ShapeDtypeStruct((g_, r_, w), dt) for w, dt in widths],
        compiler_params=_cparams("parallel", "parallel"),
        name="in_proj",
    )(x, norm_g.reshape(1, D_MODEL), sc, sh, w_pack, qg.reshape(1, HEAD_DIM), kg.reshape(1, HEAD_DIM))


def _ssd_prompt_kernel(xbc_ref, dt_ref, zg_ref, cw_ref, cb_ref, dtb_ref, alog_ref, dsk_ref, ng_ref,
                       ex_ref, ext_ref, y_o, st_o, xe_scr, st_scr):
    c = pl.program_id(1)
    q_ = SSD_CHUNK
    halo = SUBLANES

    @pl.when(c == 0)
    def _():
        xe_scr[0:halo, :] = jnp.zeros((halo, CONV_DIM), F32)
        st_scr[...] = jnp.zeros_like(st_scr)

    xe_scr[halo:halo + q_, :] = xbc_ref[0]
    conv = cb_ref[...] + sum(
        cw_ref[j:j + 1, :] * xe_scr[halo - (SSM_CONV - 1) + j:halo - (SSM_CONV - 1) + j + q_, :]
        for j in range(SSM_CONV))
    xe_scr[0:halo, :] = xe_scr[q_:q_ + halo, :]
    xc = _silu(conv)
    xs = xc[:, :D_SSM]
    bm = xc[:, D_SSM:D_SSM + SSM_GROUPS * SSM_STATE]
    cm = xc[:, D_SSM + SSM_GROUPS * SSM_STATE:]

    dt = _softplus(dt_ref[0] + dtb_ref[...])
    da = dt * (-jnp.exp(alog_ref[...]))
    row = lax.broadcasted_iota(jnp.int32, (q_, q_), 0)
    col = lax.broadcasted_iota(jnp.int32, (q_, q_), 1)
    causal = row >= col
    tri = jnp.where(causal, 1.0, 0.0).astype(BF16)
    acs = _dot_sel_lhs(tri, da)
    acs_t = acs.T
    acs_last = acs[q_ - 1:q_, :]
    ex = ex_ref[...]
    dt_x = _dot_sel_rhs(dt, ex)
    eacs_x = _dot_sel_rhs(jnp.exp(acs), ex)
    dte_x = _dot_sel_rhs(jnp.exp(acs_last - acs), ex)
    cdec = jnp.exp(jnp.broadcast_to(acs_t[:, q_ - 1:q_], (LANES, SSM_STATE)))
    cdec_rows = _dot_sel_lhs(ext_ref[...], cdec)

    xdt = xs * dt_x
    xdtd = xdt * dte_x
    gw = SSM_HEADS // SSM_GROUPS * SSM_HEAD_DIM
    gated = []
    for g in range(SSM_GROUPS):
        bg = bm[:, g * SSM_STATE:(g + 1) * SSM_STATE].astype(BF16)
        cg = cm[:, g * SSM_STATE:(g + 1) * SSM_STATE].astype(BF16)
        cb = _dot_nt(cg, bg)
        yd = []
        for hl in range(SSM_HEADS // SSM_GROUPS):
            h = g * (SSM_HEADS // SSM_GROUPS) + hl
            seg = acs[:, h:h + 1] - acs_t[h:h + 1, :]
            lmat = jnp.exp(jnp.where(causal, seg, -jnp.inf))
            yd.append(jnp.dot((cb * lmat).astype(BF16),
                              xdt[:, h * SSM_HEAD_DIM:(h + 1) * SSM_HEAD_DIM].astype(BF16),
                              preferred_element_type=F32))
        y_diag = jnp.concatenate(yd, axis=-1)
        st_old = st_scr[g * gw:(g + 1) * gw, :]
        y_off = _dot_nt(cg, st_old.astype(BF16)) * eacs_x[:, g * gw:(g + 1) * gw]
        st_new = _dot_tn(xdtd[:, g * gw:(g + 1) * gw].astype(BF16), bg)
        st_scr[g * gw:(g + 1) * gw, :] = st_old * cdec_rows[g * gw:(g + 1) * gw, :] + st_new
        yg = y_diag + y_off + dsk_ref[:, g * gw:(g + 1) * gw] * xs[:, g * gw:(g + 1) * gw]
        gated.append(yg * _silu(zg_ref[0, :, g * gw:(g + 1) * gw]))

    y_o[0] = (_rms(jnp.concatenate(gated, axis=-1)) * ng_ref[...]).astype(y_o.dtype)

    @pl.when(c == pl.num_programs(1) - 1)
    def _():
        st_o[0] = st_scr[...]


def _ssd_prompt(xbc, dt, zg, conv_w, conv_b, dtb_pad, alog_pad, dsk_x, norm_g, ex, ext):
    b_, s_, _ = xbc.shape
    nc = s_ // SSD_CHUNK
    return pl.pallas_call(
        _ssd_prompt_kernel,
        grid=(b_, nc),
        in_specs=[
            pl.BlockSpec((1, SSD_CHUNK, CONV_DIM), lambda b, c: (b, c, 0)),
            pl.BlockSpec((1, SSD_CHUNK, LANES), lambda b, c: (b, c, 0)),
            pl.BlockSpec((1, SSD_CHUNK, D_SSM), lambda b, c: (b, c, 0)),
            _resident((SSM_CONV, CONV_DIM)),
            _resident((1, CONV_DIM)),
            _resident((1, LANES)),
            _resident((1, LANES)),
            _resident((1, D_SSM)),
            _resident((1, D_SSM)),
            _resident((LANES, D_SSM)),
            _resident((D_SSM, LANES)),
        ],
        out_specs=[
            pl.BlockSpec((1, SSD_CHUNK, D_SSM), lambda b, c: (b, c, 0)),
            pl.BlockSpec((1, D_SSM, SSM_STATE), lambda b, c: (b, 0, 0)),
        ],
        out_shape=[
            jax.ShapeDtypeStruct((b_, s_, D_SSM), F32),
            jax.ShapeDtypeStruct((b_, D_SSM, SSM_STATE), F32),
        ],
        scratch_shapes=[
            pltpu.VMEM((SSD_CHUNK + SUBLANES, CONV_DIM), F32),
            pltpu.VMEM((D_SSM, SSM_STATE), F32),
        ],
        compiler_params=_cparams("parallel", "arbitrary"),
        name="ssd_prompt",
    )(xbc, dt, zg, conv_w, conv_b, dtb_pad, alog_pad, dsk_x, norm_g, ex, ext)


def _f32_key(x):
    b = lax.bitcast_convert_type(x, jnp.int32)
    return b ^ ((b >> 31) & 0x7FFFFFFF)


def _key_f32(k):
    return lax.bitcast_convert_type(k ^ ((k >> 31) & 0x7FFFFFFF), F32)


def _dsa_prompt_kernel(topk, qi_ref, wi_ref, ki_ref, q_ref, k_ref, v_ref, tz_ref, o_ref,
                       sc_scr, wb_scr, s_scr, mx_scr, l_scr, acc_scr):
    qb = pl.program_id(1)
    tq = q_ref.shape[1]
    ch = ATT_CHUNK
    n_ch = qb + 1
    zeros_t = jnp.zeros((tq, LANES), F32)

    def chunk_at(j):
        return pl.ds(pl.multiple_of(j * ch, ch), ch)

    for h in range(IDX_HEADS):
        wb_scr[h] = jnp.broadcast_to(wi_ref[0, :, h:h + 1], (tq, ch))
    qi_rows = jnp.concatenate([qi_ref[0, :, h * IDX_DIM:(h + 1) * IDX_DIM] for h in range(IDX_HEADS)], axis=0)

    def score_chunk(j, diagonal):
        d = _dot_nt(qi_rows, ki_ref[0, chunk_at(j), :])
        acc = wb_scr[0] * jnp.maximum(d[0:tq, :], 0.0)
        for h in range(1, IDX_HEADS):
            acc = acc + wb_scr[h] * jnp.maximum(d[h * tq:(h + 1) * tq, :], 0.0)
        if diagonal:
            causal = (lax.broadcasted_iota(jnp.int32, (tq, ch), 1) <= lax.broadcasted_iota(jnp.int32, (tq, ch), 0))
            acc = jnp.where(causal, acc, -jnp.inf)
        sc_scr[:, chunk_at(j)] = acc

    def score_body(j, c):
        score_chunk(j, False)
        return c

    lax.fori_loop(0, qb, score_body, 0)
    score_chunk(qb, True)

    half = tq // 2

    def chunks_reduce(fn, init, *row_args):
        outs = []
        for r0 in (0, half):
            args = [a[r0:r0 + half] for a in row_args]

            def body(j, acc):
                t = sc_scr[r0:r0 + half, chunk_at(j)]
                return fn(fn(acc, t[:, :LANES], *args), t[:, LANES:], *args)
            outs.append(lax.fori_loop(0, n_ch, body, init[r0:r0 + half]))
        return jnp.concatenate(outs, axis=0)

    def count(cmp, thr):
        thr_b = jnp.broadcast_to(thr, (tq, LANES))
        acc = chunks_reduce(lambda a, t, x: a + jnp.where(cmp(t, x), 1.0, 0.0), zeros_t, thr_b)
        return jnp.sum(acc, axis=-1, keepdims=True)

    count_ge = functools.partial(count, lambda t, x: t >= x)

    n_valid = qb * tq + lax.broadcasted_iota(jnp.int32, (tq, 1), 0) + 1
    k_row = jnp.minimum(n_valid, topk).astype(F32)
    mx = jnp.max(chunks_reduce(jnp.maximum, jnp.full((tq, LANES), -jnp.inf, F32)), axis=-1, keepdims=True)
    mn = jnp.min(chunks_reduce(lambda a, t: jnp.minimum(a, jnp.where(t == -jnp.inf, jnp.inf, t)),
                               jnp.full((tq, LANES), jnp.inf, F32)), axis=-1, keepdims=True)
    top_tied = count_ge(mx) >= k_row
    take_all = n_valid <= topk
    state = (_f32_key(mn), _f32_key(mx), jnp.where(top_tied, mx, mn),
             jnp.where(jnp.logical_or(top_tied, take_all), 1, 0).astype(jnp.int32), jnp.int32(0))

    def cond(s):
        return jnp.logical_and(jnp.min(s[3]) == 0, s[4] < 40)

    def body(s):
        lo, hi, thr, done, it = s
        mid = (lo >> 1) + (hi >> 1) + (lo & hi & 1)
        mid_f = _key_f32(mid)
        cnt = count_ge(mid_f)
        ge = cnt >= k_row
        collapsed = mid == lo
        finished = jnp.where(collapsed, 1, jnp.where(cnt == k_row, 1, 0)) * (1 - done)
        thr = jnp.where(finished == 1, jnp.where(collapsed, _key_f32(lo), mid_f), thr)
        return jnp.where(ge, mid, lo), jnp.where(ge, hi, mid), thr, done + finished, it + 1

    thr = lax.while_loop(cond, body, state)[2]

    thr_b = jnp.broadcast_to(thr, (tq, ch))
    c_gt = count(lambda t, x: t > x, thr)
    need_b = jnp.broadcast_to(k_row - c_gt, (tq, ch))
    upper = jnp.where(lax.broadcasted_iota(jnp.int32, (ch, ch), 0) <= lax.broadcasted_iota(jnp.int32, (ch, ch), 1),
                      1.0, 0.0).astype(BF16)
    ones_c = jnp.ones((ch, LANES), BF16)

    def select_chunk(j, run):
        t = sc_scr[:, chunk_at(j)]
        eq = t == thr_b
        eqb = jnp.where(eq, 1.0, 0.0).astype(BF16)
        rank = jnp.dot(eqb, upper, preferred_element_type=F32) + jnp.concatenate([run, run], axis=1)
        tie_bias = jnp.where(rank <= need_b, 0.0, NEG_BIG)
        sc_scr[:, chunk_at(j)] = jnp.where(t > thr_b, 0.0, jnp.where(eq, tie_bias, NEG_BIG))
        return run + jnp.dot(eqb, ones_c, preferred_element_type=F32)

    lax.fori_loop(0, n_ch, select_chunk, zeros_t)

    scale = HEAD_DIM ** -0.5
    rows = KV_REP * tq
    n_far = jnp.maximum(qb - 1, 0)
    for gp in range(N_KV_HEADS // 2):
        groups = (2 * gp, 2 * gp + 1)
        q_rows = [jnp.concatenate([q_ref[0, :, (g * KV_REP + r) * HEAD_DIM:(g * KV_REP + r + 1) * HEAD_DIM]
                                   for r in range(KV_REP)], axis=0) for g in groups]
        mx_scr[...] = jnp.full(mx_scr.shape, NEG_BIG, F32)

        def logits_chunk(j, c, near):
            selb = sc_scr[:, chunk_at(j)]
            for gi, g in enumerate(groups):
                s = _dot_nt(q_rows[gi], k_ref[0, chunk_at(j), g * HEAD_DIM:(g + 1) * HEAD_DIM]) * scale
                if near:
                    bias = jnp.concatenate([tz_ref[qb - j, g * KV_REP + r] + selb for r in range(KV_REP)], axis=0)
                else:
                    bias = jnp.concatenate([selb] * KV_REP, axis=0)
                s = s + bias
                s_scr[gi, :, chunk_at(j)] = s
                rs = slice(gi * rows, (gi + 1) * rows)
                mx_scr[rs, :] = jnp.maximum(mx_scr[rs, :], jnp.maximum(s[:, :LANES], s[:, LANES:]))
            return c

        lax.fori_loop(0, n_far, functools.partial(logits_chunk, near=False), 0)
        lax.fori_loop(n_far, n_ch, functools.partial(logits_chunk, near=True), 0)
        mx_scr[...] = jnp.broadcast_to(jnp.max(mx_scr[...], axis=-1, keepdims=True), mx_scr.shape)
        l_scr[...] = jnp.zeros_like(l_scr)
        acc_scr[...] = jnp.zeros_like(acc_scr)

        def weights_chunk(j, c):
            for gi, g in enumerate(groups):
                rs = slice(gi * rows, (gi + 1) * rows)
                m = mx_scr[rs, :]
                p = jnp.exp(s_scr[gi, :, chunk_at(j)] - jnp.concatenate([m, m], axis=1))
                l_scr[rs, :] += p[:, :LANES] + p[:, LANES:]
                acc_scr[rs, :] += jnp.dot(p.astype(BF16), v_ref[0, chunk_at(j), g * HEAD_DIM:(g + 1) * HEAD_DIM],
                                          preferred_element_type=F32)
            return c

        lax.fori_loop(0, n_ch, weights_chunk, 0)
        out = acc_scr[...] * (1.0 / jnp.sum(l_scr[...], axis=-1, keepdims=True))
        for gi, g in enumerate(groups):
            for r in range(KV_REP):
                h = g * KV_REP + r
                o_ref[0, :, h * HEAD_DIM:(h + 1) * HEAD_DIM] = out[gi * rows + r * tq:gi * rows + (r + 1) * tq, :].astype(o_ref.dtype)


def _dsa_prompt(qi, wi, ki_b, q_b, k_b, v_b, tz, topk):
    b_, s_, _ = q_b.shape
    tq = min(Q_BLOCK, s_)
    assert tq == ATT_CHUNK and s_ % tq == 0
    per_batch = lambda w: pl.BlockSpec((1, s_, w), lambda b, i: (b, 0, 0), pipeline_mode=pl.Buffered(1))
    pair_rows = 2 * KV_REP * tq
    return pl.pallas_call(
        functools.partial(_dsa_prompt_kernel, topk),
        grid=(b_, s_ // tq),
        in_specs=[
            pl.BlockSpec((1, tq, IDX_HEADS * IDX_DIM), lambda b, i: (b, i, 0)),
            pl.BlockSpec((1, tq, IDX_HEADS), lambda b, i: (b, i, 0)),
            per_batch(IDX_DIM),
            pl.BlockSpec((1, tq, D_ATT), lambda b, i: (b, i, 0)),
            per_batch(D_KV),
            per_batch(D_KV),
            _resident(tz.shape),
        ],
        out_specs=pl.BlockSpec((1, tq, D_ATT), lambda b, i: (b, i, 0)),
        out_shape=jax.ShapeDtypeStruct((b_, s_, D_ATT), BF16),
        scratch_shapes=[
            pltpu.VMEM((tq, s_), F32),
            pltpu.VMEM((IDX_HEADS, tq, ATT_CHUNK), F32),
            pltpu.VMEM((2, KV_REP * tq, s_), F32),
            pltpu.VMEM((pair_rows, LANES), F32),
            pltpu.VMEM((pair_rows, LANES), F32),
            pltpu.VMEM((pair_rows, HEAD_DIM), F32),
        ],
        compiler_params=_cparams("parallel", "arbitrary"),
        name="dsa_prompt",
    )(qi, wi, ki_b, q_b, k_b, v_b, tz)


def _mix_and_norm(att_ref, ssd_ref, x_ref, g1_ref, sc2_ref, sh2_ref, n2_ref, wo_ref):
    o = (jnp.dot(att_ref[0].astype(BF16), wo_ref[:D_ATT, :], preferred_element_type=F32)
         + jnp.dot(ssd_ref[0].astype(BF16), wo_ref[D_ATT:, :], preferred_element_type=F32))
    x1 = x_ref[0] + g1_ref[0] * o
    h2 = ((_rms(x1) * n2_ref[...]) * (1.0 + sc2_ref[0]) + sh2_ref[0]).astype(BF16)
    return x1, h2


def _ffn_prompt_kernel(att_ref, ssd_ref, x_ref, g1_ref, sc2_ref, sh2_ref, g2_ref, n2_ref, wo_ref, wu_ref,
                       cw_ref, cb_ref, wd_ref, y_o, tail_o, acc_scr, ug_scr, uv_scr, carry_scr):
    i = pl.program_id(1)
    tm = x_ref.shape[1]
    halo = SUBLANES

    @pl.when(i == 0)
    def _():
        carry_scr[...] = jnp.zeros_like(carry_scr)

    x1, h2 = _mix_and_norm(att_ref, ssd_ref, x_ref, g1_ref, sc2_ref, sh2_ref, n2_ref, wo_ref)
    acc_scr[...] = jnp.zeros_like(acc_scr)

    def conv_cols(scr, c0):
        u = jnp.dot(h2, wu_ref[:, c0:c0 + FF_CHUNK], preferred_element_type=F32)
        scr[0:halo, :] = carry_scr[:, c0:c0 + FF_CHUNK]
        scr[halo:halo + tm, :] = u
        carry_scr[:, c0:c0 + FF_CHUNK] = u[tm - halo:, :]
        return cb_ref[:, c0:c0 + FF_CHUNK] + sum(
            cw_ref[j:j + 1, c0:c0 + FF_CHUNK] * scr[halo - (FFN_CONV - 1) + j:halo - (FFN_CONV - 1) + j + tm, :]
            for j in range(FFN_CONV))

    for jc in range(D_FF // FF_CHUNK):
        gate = conv_cols(ug_scr, jc * FF_CHUNK)
        val = conv_cols(uv_scr, D_FF + jc * FF_CHUNK)
        a = (_silu(gate) * val).astype(BF16)
        acc_scr[...] += jnp.dot(a, wd_ref[jc * FF_CHUNK:(jc + 1) * FF_CHUNK, :], preferred_element_type=F32)

    y_o[0] = x1 + g2_ref[0] * acc_scr[...]
    tail_o[0] = carry_scr[...]


def _ffn_prompt(att, ssd, x, g1, sc2, sh2, g2, norm2_g, wo_b, wu_b, cw, cb, wd_b):
    b_, s_, _ = x.shape
    tm = min(ROW_TILE, s_)
    nt = s_ // tm
    row = lambda w: pl.BlockSpec((1, tm, w), lambda b, i: (b, i, 0))
    mod = pl.BlockSpec((1, 1, D_MODEL), lambda b, i: (b, 0, 0))
    return pl.pallas_call(
        _ffn_prompt_kernel,
        grid=(b_, nt),
        in_specs=[row(D_ATT), row(D_SSM), row(D_MODEL), mod, mod, mod, mod, _resident((1, D_MODEL)),
                  _resident((D_MIX, D_MODEL)), _resident((D_MODEL, 2 * D_FF)), _resident((FFN_CONV, 2 * D_FF)),
                  _resident((1, 2 * D_FF)), _resident((D_FF, D_MODEL))],
        out_specs=[row(D_MODEL), pl.BlockSpec((1, SUBLANES, 2 * D_FF), lambda b, i: (b, 0, 0))],
        out_shape=[jax.ShapeDtypeStruct((b_, s_, D_MODEL), F32),
                   jax.ShapeDtypeStruct((b_, SUBLANES, 2 * D_FF), F32)],
        scratch_shapes=[pltpu.VMEM((tm, D_MODEL), F32), pltpu.VMEM((tm + SUBLANES, FF_CHUNK), F32),
                        pltpu.VMEM((tm + SUBLANES, FF_CHUNK), F32), pltpu.VMEM((SUBLANES, 2 * D_FF), F32)],
        compiler_params=_cparams("parallel", "arbitrary"),
        name="ffn_prompt",
    )(att, ssd, x, g1, sc2, sh2, g2, norm2_g.reshape(1, D_MODEL), wo_b, wu_b, cw, cb, wd_b)


def _ffn_sample_kernel(att_ref, ssd_ref, x_ref, g1_ref, sc2_ref, sh2_ref, g2_ref, n2_ref, wo_ref, wu_ref,
                       cw_ref, cb_ref, wd_ref, h0_ref, h1_ref, y_o, u_o, acc_scr):
    x1, h2 = _mix_and_norm(att_ref, ssd_ref, x_ref, g1_ref, sc2_ref, sh2_ref, n2_ref, wo_ref)
    acc_scr[...] = jnp.zeros_like(acc_scr)

    def conv_cols(c0):
        u = jnp.dot(h2, wu_ref[:, c0:c0 + FF_CHUNK], preferred_element_type=F32)
        u_o[:, c0:c0 + FF_CHUNK] = u
        sl = slice(c0, c0 + FF_CHUNK)
        return (cb_ref[:, sl] + cw_ref[0:1, sl] * h0_ref[:, sl] + cw_ref[1:2, sl] * h1_ref[:, sl]
                + cw_ref[2:3, sl] * u)

    for jc in range(D_FF // FF_CHUNK):
        gate = conv_cols(jc * FF_CHUNK)
        val = conv_cols(D_FF + jc * FF_CHUNK)
        a = (_silu(gate) * val).astype(BF16)
        acc_scr[...] += jnp.dot(a, wd_ref[jc * FF_CHUNK:(jc + 1) * FF_CHUNK, :], preferred_element_type=F32)

    y_o[0] = x1 + g2_ref[0] * acc_scr[...]


def _ffn_sample(att, ssd, x, g1, sc2, sh2, g2, norm2_g, wo_b, wu_b, cw, cb, wd_b, hist0, hist1):
    n = x.shape[1]
    full = lambda *shape: pl.BlockSpec(shape, lambda: (0,) * len(shape))
    r3 = lambda w: full(1, n, w)
    return pl.pallas_call(
        _ffn_sample_kernel,
        in_specs=[r3(D_ATT), r3(D_SSM), r3(D_MODEL), r3(D_MODEL), r3(D_MODEL), r3(D_MODEL), r3(D_MODEL),
                  full(1, D_MODEL), full(D_MIX, D_MODEL), full(D_MODEL, 2 * D_FF), full(FFN_CONV, 2 * D_FF),
                  full(1, 2 * D_FF), full(D_FF, D_MODEL), full(n, 2 * D_FF), full(n, 2 * D_FF)],
        out_specs=[r3(D_MODEL), full(n, 2 * D_FF)],
        out_shape=[jax.ShapeDtypeStruct((1, n, D_MODEL), F32), jax.ShapeDtypeStruct((n, 2 * D_FF), F32)],
        scratch_shapes=[pltpu.VMEM((n, D_MODEL), F32)],
        compiler_params=pltpu.CompilerParams(vmem_limit_bytes=VMEM_LIMIT),
        name="ffn_sample",
    )(att, ssd, x, g1, sc2, sh2, g2, norm2_g.reshape(1, D_MODEL), wo_b, wu_b, cw, cb, wd_b, hist0, hist1)


def _ssd_sample_kernel(xbc_ref, h0_ref, h1_ref, h2_ref, dt_ref, zg_ref, cw_ref, cb_ref, dtb_ref, alog_ref,
                       dsk_ref, ng_ref, ex_ref, st_ref, y_o, st_o):
    nb = xbc_ref.shape[0]
    conv = (cb_ref[...] + cw_ref[0:1, :] * h0_ref[...] + cw_ref[1:2, :] * h1_ref[...]
            + cw_ref[2:3, :] * h2_ref[...] + cw_ref[3:4, :] * xbc_ref[...])
    xc = _silu(conv)
    xs = xc[:, :D_SSM]
    bm = xc[:, D_SSM:D_SSM + SSM_GROUPS * SSM_STATE]
    cm = xc[:, D_SSM + SSM_GROUPS * SSM_STATE:]
    dt = _softplus(dt_ref[...] + dtb_ref[...])
    dec = jnp.exp(dt * (-jnp.exp(alog_ref[...])))
    ex = ex_ref[...]
    xdt = xs * _dot_sel_rhs(dt, ex)
    dec_x = _dot_sel_rhs(dec, ex)
    gw = SSM_HEADS // SSM_GROUPS * SSM_HEAD_DIM
    ones_n = jnp.ones((SUBLANES, SSM_STATE), BF16)
    row0 = lax.broadcasted_iota(jnp.int32, (SUBLANES, 1), 0) == 0

    def only_row0(v):
        return jnp.where(row0, jnp.broadcast_to(v, (SUBLANES, v.shape[1])), 0.0)

    ys = []
    for i in range(nb):
        yrow = []
        for g in range(SSM_GROUPS):
            hs = st_ref[i, g * gw:(g + 1) * gw, :]
            dec_rows = only_row0(dec_x[i:i + 1, g * gw:(g + 1) * gw])
            xdt_rows = only_row0(xdt[i:i + 1, g * gw:(g + 1) * gw])
            b_rows = only_row0(bm[i:i + 1, g * SSM_STATE:(g + 1) * SSM_STATE])
            c_rows = only_row0(cm[i:i + 1, g * SSM_STATE:(g + 1) * SSM_STATE])
            dec_b = sum(_dot_tn(p, ones_n) for p in _split3(dec_rows))
            upd = sum(_dot_tn(p, q) for p in _split3(xdt_rows)[:2] for q in _split3(b_rows)[:2])
            hn = hs * dec_b + upd
            st_o[i, g * gw:(g + 1) * gw, :] = hn
            yg = sum(_dot_nt(p, q) for p in _split3(c_rows)[:2] for q in _split3(hn)[:2])
            yrow.append(yg[0:1, :])
        ys.append(jnp.concatenate(yrow, axis=-1))
    y = jnp.concatenate(ys, axis=0) + dsk_ref[...] * xs
    y = y * _silu(zg_ref[...])
    y_o[...] = _rms(y) * ng_ref[...]


def _ssd_sample(xbc, hist, dt, zg, conv_w, conv_b, dtb_pad, alog_pad, dsk_x, norm_g, ex, state):
    n = xbc.shape[0]
    nb = SUBLANES
    rows = lambda w: pl.BlockSpec((nb, w), lambda i: (i, 0))
    st = pl.BlockSpec((nb, D_SSM, SSM_STATE), lambda i: (i, 0, 0))
    return pl.pallas_call(
        _ssd_sample_kernel,
        grid=(n // nb,),
        in_specs=[rows(CONV_DIM), rows(CONV_DIM), rows(CONV_DIM), rows(CONV_DIM), rows(LANES), rows(D_SSM),
                  _resident((SSM_CONV, CONV_DIM)), _resident((1, CONV_DIM)), _resident((1, LANES)),
                  _resident((1, LANES)), _resident((1, D_SSM)), _resident((1, D_SSM)), _resident((LANES, D_SSM)), st],
        out_specs=[rows(D_SSM), st],
        out_shape=[jax.ShapeDtypeStruct((n, D_SSM), F32), jax.ShapeDtypeStruct((n, D_SSM, SSM_STATE), F32)],
        compiler_params=_cparams("parallel"),
        name="ssd_sample",
    )(xbc, hist[0], hist[1], hist[2], dt, zg, conv_w, conv_b, dtb_pad, alog_pad, dsk_x, norm_g, ex, state)


def _idx_sample_kernel(pt_ref, qi_ref, wi_ref, kin_ref, pool_ref, o_ref, kbuf, sem):
    b = pl.program_id(0)
    nb = pl.num_programs(0)
    n_pages = kbuf.shape[1]
    page = kbuf.shape[3]

    def page_copy(seq, slot, j):
        return pltpu.make_async_copy(pool_ref.at[pt_ref[seq, j]], kbuf.at[slot, j], sem.at[slot])

    def fetch(seq, slot):
        def body(j, c):
            page_copy(seq, slot, j).start()
            return c
        lax.fori_loop(0, n_pages, body, 0)

    @pl.when(b == 0)
    def _():
        fetch(0, 0)

    @pl.when(b + 1 < nb)
    def _():
        fetch(b + 1, (b + 1) % 2)

    slot = b % 2

    def wait_body(j, c):
        page_copy(b, slot, j).wait()
        return c
    lax.fori_loop(0, n_pages, wait_body, 0)

    qi = qi_ref[0]
    w = wi_ref[0]

    def page_scores(j, c):
        d = jnp.dot(qi, kbuf[slot, j].astype(BF16), preferred_element_type=F32)
        o_ref[0, :, pl.ds(pl.multiple_of(j * page, page), page)] = jnp.sum(w * jnp.maximum(d, 0.0), axis=0, keepdims=True)
        return c

    lax.fori_loop(0, n_pages, page_scores, 0, unroll=8)
    knew = kin_ref[0].astype(BF16).astype(F32)
    dn = jnp.sum(qi.astype(F32) * knew, axis=-1, keepdims=True)
    s_new = jnp.sum(w * jnp.maximum(dn, 0.0), axis=0, keepdims=True)
    lane = lax.broadcasted_iota(jnp.int32, (1, LANES), 1)
    o_ref[0, :, n_pages * page:] = jnp.where(lane == 0, s_new, -jnp.inf)


def _idx_sample(page_table, qi3, wi3, ki_new3, pool_t):
    n, n_pages = page_table.shape
    page = pool_t.shape[2]
    past = n_pages * page
    grid_spec = pltpu.PrefetchScalarGridSpec(
        num_scalar_prefetch=1,
        grid=(n,),
        in_specs=[
            pl.BlockSpec((1, IDX_HEADS, IDX_DIM), lambda b, pt: (b, 0, 0)),
            pl.BlockSpec((1, IDX_HEADS, 1), lambda b, pt: (b, 0, 0)),
            pl.BlockSpec((1, 1, IDX_DIM), lambda b, pt: (b, 0, 0)),
            pl.BlockSpec(memory_space=pl.ANY),
        ],
        out_specs=pl.BlockSpec((1, 1, past + LANES), lambda b, pt: (b, 0, 0)),
        scratch_shapes=[pltpu.VMEM((2, n_pages, IDX_DIM, page), F32), pltpu.SemaphoreType.DMA((2,))],
    )
    return pl.pallas_call(
        _idx_sample_kernel,
        grid_spec=grid_spec,
        out_shape=jax.ShapeDtypeStruct((n, 1, past + LANES), F32),
        compiler_params=_cparams("arbitrary"),
        name="idx_sample",
    )(page_table, qi3, wi3, ki_new3, pool_t)


def _att_sample_kernel(q_ref, k_ref, v_ref, bias_ref, o_ref):
    nb = q_ref.shape[0]
    scale = HEAD_DIM ** -0.5
    head_group = lax.broadcasted_iota(jnp.int32, (N_HEADS, 1), 0) // KV_REP
    for i in range(nb):
        q = q_ref[i].astype(BF16)
        bias = bias_ref[i]
        out = jnp.zeros((N_HEADS, HEAD_DIM), F32)
        for g in range(N_KV_HEADS):
            kg = k_ref[i, :, g * HEAD_DIM:(g + 1) * HEAD_DIM].astype(BF16)
            vg = v_ref[i, :, g * HEAD_DIM:(g + 1) * HEAD_DIM].astype(BF16)
            s = _dot_nt(q, kg) * scale + bias
            m = jnp.max(s, axis=-1, keepdims=True)
            p = jnp.exp(s - m)
            p = p * (1.0 / jnp.sum(p, axis=-1, keepdims=True))
            og = jnp.dot(p.astype(BF16), vg, preferred_element_type=F32)
            out = jnp.where(head_group == g, og, out)
        o_ref[i] = out


def _att_sample(q3, k_sel, v_sel, bias):
    n, kk, _ = k_sel.shape
    nb = SUBLANES
    return pl.pallas_call(
        _att_sample_kernel,
        grid=(n // nb,),
        in_specs=[
            pl.BlockSpec((nb, N_HEADS, HEAD_DIM), lambda i: (i, 0, 0)),
            pl.BlockSpec((nb, kk, D_KV), lambda i: (i, 0, 0)),
            pl.BlockSpec((nb, kk, D_KV), lambda i: (i, 0, 0)),
            pl.BlockSpec((nb, N_HEADS, kk), lambda i: (i, 0, 0)),
        ],
        out_specs=pl.BlockSpec((nb, N_HEADS, HEAD_DIM), lambda i: (i, 0, 0)),
        out_shape=jax.ShapeDtypeStruct((n, N_HEADS, HEAD_DIM), F32),
        compiler_params=_cparams("parallel"),
        name="att_sample",
    )(q3, k_sel, v_sel, bias)


def _pack_w_in(w_in):
    offs = np.cumsum((0,) + IN_SPLITS)
    parts = [w_in[:, offs[i]:offs[i + 1]] for i in range(len(IN_SPLITS))]
    pad = lambda a: jnp.pad(a, ((0, 0), (0, LANES - a.shape[1])))
    q, k, v, qi, ki, wi, zg, xbc, dt = parts
    return jnp.concatenate([q, k, v, qi, pad(ki), pad(wi), zg, xbc, pad(dt)], axis=1).astype(BF16)


def _t5_bucket(dist):
    dist = jnp.maximum(dist, 0)
    max_exact = N_BUCKETS // 2
    d = jnp.maximum(dist, 1).astype(F32)
    large = max_exact + (jnp.log(d / max_exact) / math.log(MAX_DISTANCE / max_exact) * (N_BUCKETS - max_exact)).astype(jnp.int32)
    large = jnp.minimum(large, N_BUCKETS - 1)
    return jnp.where(dist < max_exact, dist, large)


def _rel_bias_tables(rel_bias, tq):
    assert ATT_CHUNK <= tq and tq >= MAX_DISTANCE
    n = 2 * tq
    idx = jnp.arange(n, dtype=jnp.int32)
    col_minus_row = jnp.where(idx < tq, idx, idx - n)
    far = rel_bias[_t5_bucket(jnp.int32(MAX_DISTANCE))]
    tabs = []
    for d0 in (0, tq):
        f = rel_bias[_t5_bucket(d0 - col_minus_row)] - far[None, :]
        flat = jnp.tile(f.T, (1, tq))
        tabs.append(flat[:, :tq * (n - 1)].reshape(-1, tq, n - 1)[:, :, :ATT_CHUNK])
    return jnp.stack(tabs)


def _head_expand():
    h = jnp.arange(LANES, dtype=jnp.int32)[:, None]
    c = jnp.arange(D_SSM, dtype=jnp.int32)[None, :]
    ex = (c // SSM_HEAD_DIM == h).astype(BF16)
    return ex, ex.T


def _pad_lanes(v):
    return jnp.pad(v.astype(F32), (0, LANES - v.shape[0])).reshape(1, LANES)


def kernel(x_prompt, x_sample, cache_k, cache_v, cache_kidx, state_ssm, state_conv_ssd, state_conv_ffn, page_table,
           c_prompt, c_sample, rel_bias, norm1_g, w_ada, b_ada, w_in, q_norm_g, k_norm_g, conv_ssd_w, conv_ssd_b,
           dt_bias, a_log, d_skip, ssd_norm_g, w_out, norm2_g, w_up, conv_ffn_w, conv_ffn_b, w_down):
    depth = w_in.shape[0]
    assert depth == 1
    bp, s_, _ = x_prompt.shape
    ns = x_sample.shape[0]
    assert x_sample.shape[1] == 1
    lyr = 0
    n_pages = page_table.shape[1]
    page = cache_k.shape[2]
    past = n_pages * page
    topk_p = min(TOPK_MAX, s_ // TOPK_DIV)
    topk_s = min(TOPK_MAX, (past + 1) // TOPK_DIV)

    w_pack = _pack_w_in(w_in[lyr])
    wo_b = w_out[lyr].astype(BF16)
    wu_b = w_up[lyr].astype(BF16)
    wd_b = w_down[lyr].astype(BF16)
    ex, ext = _head_expand()
    dtb_pad = _pad_lanes(dt_bias[lyr])
    alog_pad = _pad_lanes(a_log[lyr])
    dsk_x = jnp.repeat(d_skip[lyr].astype(F32), SSM_HEAD_DIM).reshape(1, D_SSM)
    ssd_g = ssd_norm_g[lyr].reshape(1, D_SSM)
    cw_ssd, cb_ssd = conv_ssd_w[lyr], conv_ssd_b[lyr].reshape(1, CONV_DIM)
    cw_ffn, cb_ffn = conv_ffn_w[lyr], conv_ffn_b[lyr].reshape(1, 2 * D_FF)
    tz = _rel_bias_tables(rel_bias.astype(F32), min(Q_BLOCK, s_))

    n_c = bp + ns
    c_all = jnp.pad(jnp.concatenate([c_prompt, c_sample], axis=0), ((0, (-n_c) % SUBLANES), (0, 0)))
    mod = _ada(c_all, w_ada[lyr], b_ada[lyr])
    mods_p = [mod[:bp, i * D_MODEL:(i + 1) * D_MODEL].reshape(bp, 1, D_MODEL) for i in range(6)]
    mods_s = [mod[bp:bp + ns, i * D_MODEL:(i + 1) * D_MODEL].reshape(1, ns, D_MODEL) for i in range(6)]

    sh1, sc1, g1, sh2, sc2, g2 = mods_p
    (q_b, k_f, v_f, k_b, v_b, qi_b, ki_f, ki_b, wi_f, zg, xbc, dt) = _inproj(
        x_prompt, norm1_g[lyr], sc1, sh1, w_pack, q_norm_g[lyr], k_norm_g[lyr], rows_per_mod=1)
    att_p = _dsa_prompt(qi_b, wi_f, ki_b, q_b, k_b, v_b, tz, topk_p)
    ssd_p, ssm_p = _ssd_prompt(xbc, dt, zg, cw_ssd, cb_ssd, dtb_pad, alog_pad, dsk_x, ssd_g, ex, ext)
    y_p, tail_p = _ffn_prompt(att_p, ssd_p, x_prompt, g1, sc2, sh2, g2, norm2_g[lyr], wo_b, wu_b, cw_ffn, cb_ffn, wd_b)

    k_p = k_f.reshape(1, bp, s_, N_KV_HEADS, HEAD_DIM)
    v_p = v_f.reshape(1, bp, s_, N_KV_HEADS, HEAD_DIM)
    kidx_p = ki_f.reshape(1, bp, s_, IDX_DIM)
    ssm_p = ssm_p.reshape(1, bp, SSM_HEADS, SSM_HEAD_DIM, SSM_STATE)
    cssd_p = xbc[:, s_ - (SSM_CONV - 1):, :].reshape(1, bp, SSM_CONV - 1, CONV_DIM)
    cffn_p = tail_p[:, SUBLANES - (FFN_CONV - 1):, :].reshape(1, bp, FFN_CONV - 1, 2 * D_FF)

    sh1, sc1, g1, sh2, sc2, g2 = mods_s
    xs3 = x_sample.reshape(1, ns, D_MODEL)
    (q_b, k_f, v_f, _, _, qi_b, ki_f, _, wi_f, zg, xbc, dt) = _inproj(
        xs3, norm1_g[lyr], sc1, sh1, w_pack, q_norm_g[lyr], k_norm_g[lyr], rows_per_mod=ns)
    k_new, v_new, ki_new = k_f[0], v_f[0], ki_f[0]

    scores = _idx_sample(page_table, qi_b[0].reshape(ns, IDX_HEADS, IDX_DIM), wi_f[0].reshape(ns, IDX_HEADS, 1),
                         ki_new.reshape(ns, 1, IDX_DIM), jnp.swapaxes(cache_kidx[lyr], 1, 2))[:, 0, :]
    _, idx = lax.top_k(scores, topk_s)
    in_past = idx < past
    pidx = jnp.minimum(idx, past - 1)
    page_hit = (pidx // page)[:, :, None] == jnp.arange(n_pages, dtype=jnp.int32)
    phys = jnp.sum(jnp.where(page_hit, page_table[:, None, :], 0), axis=-1)
    off = pidx % page
    k_sel = jnp.where(in_past[..., None], cache_k[lyr][phys, off].reshape(ns, topk_s, D_KV), k_new[:, None, :])
    v_sel = jnp.where(in_past[..., None], cache_v[lyr][phys, off].reshape(ns, topk_s, D_KV), v_new[:, None, :])
    bucket_hit = _t5_bucket(past - idx)[:, None, :, None] == jnp.arange(N_BUCKETS, dtype=jnp.int32)
    bias = jnp.sum(jnp.where(bucket_hit, rel_bias.astype(F32).T[None, :, None, :], 0.0), axis=-1)
    att_s = _att_sample(q_b[0].astype(F32).reshape(ns, N_HEADS, HEAD_DIM), k_sel, v_sel, bias).reshape(1, ns, D_ATT)

    hist_ssd = state_conv_ssd[lyr]
    ssd_s, ssm_s = _ssd_sample(xbc[0], [hist_ssd[:, j, :] for j in range(SSM_CONV - 1)], dt[0], zg[0], cw_ssd, cb_ssd,
                               dtb_pad, alog_pad, dsk_x, ssd_g, ex,
                               state_ssm[lyr].reshape(ns, D_SSM, SSM_STATE))
    hist_ffn = state_conv_ffn[lyr]
    y_s, u_s = _ffn_sample(att_s, ssd_s.reshape(1, ns, D_SSM), xs3, g1, sc2, sh2, g2, norm2_g[lyr], wo_b, wu_b,
                           cw_ffn, cb_ffn, wd_b, hist_ffn[:, 0, :], hist_ffn[:, 1, :])

    k_s = k_new.reshape(1, ns, 1, N_KV_HEADS, HEAD_DIM)
    v_s = v_new.reshape(1, ns, 1, N_KV_HEADS, HEAD_DIM)
    kidx_s = ki_new.reshape(1, ns, 1, IDX_DIM)
    ssm_s = ssm_s.reshape(1, ns, SSM_HEADS, SSM_HEAD_DIM, SSM_STATE)
    cssd_s = jnp.concatenate([hist_ssd[:, 1:, :], xbc[0][:, None, :]], axis=1)[None]
    cffn_s = jnp.stack([hist_ffn[:, 1, :], u_s], axis=1)[None]

    return (y_p, y_s.reshape(ns, 1, D_MODEL), k_p, v_p, kidx_p, ssm_p, cssd_p, cffn_p,
            k_s, v_s, kidx_s, ssm_s, cssd_s, cffn_s)
```

```python
import functools
import math

import jax
import jax.numpy as jnp
import numpy as np
from jax import lax
from jax.experimental import pallas as pl
from jax.experimental.pallas import tpu as pltpu

F32 = jnp.float32
BF16 = jnp.bfloat16

D_MODEL = 1024
N_HEADS = 8
N_KV_HEADS = 4
HEAD_DIM = 128
KV_REP = N_HEADS // N_KV_HEADS
IDX_HEADS = 8
IDX_DIM = 64
TOPK_MAX = 256
TOPK_DIV = 4
N_BUCKETS = 32
MAX_DISTANCE = 128
SSM_HEADS = 16
SSM_HEAD_DIM = 64
SSM_GROUPS = 4
SSM_STATE = 128
SSM_CONV = 4
SSD_CHUNK = 128
D_ATT = N_HEADS * HEAD_DIM
D_KV = N_KV_HEADS * HEAD_DIM
D_SSM = SSM_HEADS * SSM_HEAD_DIM
D_MIX = D_ATT + D_SSM
CONV_DIM = D_SSM + 2 * SSM_GROUPS * SSM_STATE
D_FF = 2816
FFN_CONV = 3
EPS = 1e-6
IN_SPLITS = (D_ATT, D_KV, D_KV, IDX_HEADS * IDX_DIM, IDX_DIM, IDX_HEADS, D_SSM, CONV_DIM, SSM_HEADS)

LANES = 128
SUBLANES = 8
VMEM_LIMIT = 52 * 1024 * 1024

OFF_Q = 0
OFF_K = OFF_Q + D_ATT
OFF_V = OFF_K + D_KV
OFF_QI = OFF_V + D_KV
OFF_KI = OFF_QI + IDX_HEADS * IDX_DIM
OFF_WI = OFF_KI + LANES
OFF_ZG = OFF_WI + LANES
OFF_XBC = OFF_ZG + D_SSM
OFF_DT = OFF_XBC + CONV_DIM
N_PACK = OFF_DT + LANES

NEG_BIG = -1e30
LOG2_E = math.log2(math.e)
ATT_CHUNK = 256
FF_CHUNK = 256
ROW_TILE = 256
Q_BLOCK = 256
PAGES_PER_DOT = 4
VALUE_STEPS = 14
SNAP_ONLY_TRIPS = 4
SCAN_FOLD = 64
ATT_FOLD = 16


def _cparams(*sem):
    return pltpu.CompilerParams(dimension_semantics=sem, vmem_limit_bytes=VMEM_LIMIT)


def _resident(shape):
    nd = len(shape)
    return pl.BlockSpec(shape, lambda *_: (0,) * nd, pipeline_mode=pl.Buffered(1))


def _silu(x):
    return x * (1.0 / (1.0 + jnp.exp(-x)))


def _softplus(x):
    return jnp.maximum(x, 0.0) + jnp.log1p(jnp.exp(-jnp.abs(x)))


def _rms(x):
    return x * lax.rsqrt(jnp.mean(x * x, axis=-1, keepdims=True) + EPS)


def _split3(x):
    a = x.astype(BF16)
    r = x - a.astype(F32)
    b = r.astype(BF16)
    c = (r - b.astype(F32)).astype(BF16)
    return a, b, c


def _dot_sel_rhs(x, sel):
    return sum(jnp.dot(p, sel, preferred_element_type=F32) for p in _split3(x))


def _dot_sel_lhs(sel, x):
    return sum(jnp.dot(sel, p, preferred_element_type=F32) for p in _split3(x))


def _dot_nt(a, b):
    return lax.dot_general(a, b, (((1,), (1,)), ((), ())), preferred_element_type=F32)


def _dot_tn(a, b):
    return lax.dot_general(a, b, (((0,), (0,)), ((), ())), preferred_element_type=F32)


def _ada_kernel(c_ref, w_ref, b_ref, o_ref):
    s = _silu(c_ref[...]).astype(BF16)
    o_ref[...] = jnp.dot(s, w_ref[...].astype(BF16), preferred_element_type=F32) + b_ref[...]


def _ada(c_all, w_ada, b_ada):
    rows = c_all.shape[0]
    n = w_ada.shape[1]
    return pl.pallas_call(
        _ada_kernel,
        grid=(n // D_MODEL,),
        in_specs=[
            pl.BlockSpec((rows, D_MODEL), lambda j: (0, 0)),
            pl.BlockSpec((D_MODEL, D_MODEL), lambda j: (0, j)),
            pl.BlockSpec((1, D_MODEL), lambda j: (0, j)),
        ],
        out_specs=pl.BlockSpec((rows, D_MODEL), lambda j: (0, j)),
        out_shape=jax.ShapeDtypeStruct((rows, n), F32),
        compiler_params=_cparams("arbitrary"),
        name="ada_mod",
    )(c_all, w_ada, b_ada.reshape(1, n))


def _inproj_kernel(queries_on_lanes, x_ref, g_ref, sc_ref, sh_ref, w_ref, qg_ref, kg_ref,
                   q_o, k_o, v_o, kb_o, vb_o, qi_o, ki_o, kib_o, wi_o, zg_o, xbc_o, dt_o):
    x = x_ref[0]
    h = _rms(x) * g_ref[...]
    hb = (h * (1.0 + sc_ref[0]) + sh_ref[0]).astype(BF16)

    def sec(a, b):
        return jnp.dot(hb, w_ref[:, a:b], preferred_element_type=F32)

    zq = sec(OFF_Q, OFF_K)
    for i in range(N_HEADS):
        qn = _rms(zq[:, i * HEAD_DIM:(i + 1) * HEAD_DIM]) * qg_ref[...]
        if queries_on_lanes:
            q_o[0, i * HEAD_DIM:(i + 1) * HEAD_DIM, :] = qn.T.astype(BF16)
        else:
            q_o[0, :, i * HEAD_DIM:(i + 1) * HEAD_DIM] = qn.astype(BF16)
    zk = sec(OFF_K, OFF_V)
    for i in range(N_KV_HEADS):
        zh = zk[:, i * HEAD_DIM:(i + 1) * HEAD_DIM]
        kn = _rms(zh) * kg_ref[...]
        k_o[0, :, i * HEAD_DIM:(i + 1) * HEAD_DIM] = kn
        kb_o[0, :, i * HEAD_DIM:(i + 1) * HEAD_DIM] = kn.astype(BF16)
    zv = sec(OFF_V, OFF_QI)
    v_o[0] = zv
    zqi = sec(OFF_QI, OFF_KI)
    zwi = sec(OFF_WI, OFF_ZG)
    if queries_on_lanes:
        vb_o[0] = zv.T.astype(BF16)
        qi_o[0] = zqi.T.astype(BF16)
        wi_o[0] = zwi.T[:IDX_HEADS, :]
    else:
        vb_o[0] = zv.astype(BF16)
        qi_o[0] = zqi.astype(BF16)
        wi_o[0] = zwi[:, :IDX_HEADS]
    zki = sec(OFF_KI, OFF_WI)[:, :IDX_DIM]
    ki_o[0] = zki
    kib_o[0] = zki.astype(BF16)
    zg_o[0] = sec(OFF_ZG, OFF_XBC)
    xbc_o[0] = sec(OFF_XBC, OFF_DT)
    dt_o[0] = sec(OFF_DT, N_PACK)


def _inproj(x, norm_g, sc, sh, w_pack, qg, kg, rows_per_mod, queries_on_lanes):
    g_, r_, _ = x.shape
    tm = min(ROW_TILE, r_)
    nt = r_ // tm
    mod_rows = 1 if rows_per_mod == 1 else tm
    mod_map = (lambda b, i: (b, 0, 0)) if rows_per_mod == 1 else (lambda b, i: (b, i, 0))
    outs = [(D_ATT, BF16, True), (D_KV, F32, False), (D_KV, F32, False), (D_KV, BF16, False), (D_KV, BF16, True),
            (IDX_HEADS * IDX_DIM, BF16, True), (IDX_DIM, F32, False), (IDX_DIM, BF16, False), (IDX_HEADS, F32, True),
            (D_SSM, F32, False), (CONV_DIM, F32, False), (LANES, F32, False)]
    out_specs, out_shape = [], []
    for w, dt, tr in outs:
        if tr and queries_on_lanes:
            out_specs.append(pl.BlockSpec((1, w, tm), lambda b, i: (b, 0, i)))
            out_shape.append(jax.ShapeDtypeStruct((g_, w, r_), dt))
        else:
            out_specs.append(pl.BlockSpec((1, tm, w), lambda b, i: (b, i, 0)))
            out_shape.append(jax.ShapeDtypeStruct((g_, r_, w), dt))
    return pl.pallas_call(
        functools.partial(_inproj_kernel, queries_on_lanes),
        grid=(g_, nt),
        in_specs=[
            pl.BlockSpec((1, tm, D_MODEL), lambda b, i: (b, i, 0)),
            _resident((1, D_MODEL)),
            pl.BlockSpec((1, mod_rows, D_MODEL), mod_map),
            pl.BlockSpec((1, mod_rows, D_MODEL), mod_map),
            _resident((D_MODEL, N_PACK)),
            _resident((1, HEAD_DIM)),
            _resident((1, HEAD_DIM)),
        ],
        out_specs=out_specs,
        out_shape=out_shape,
        compiler_params=_cparams("parallel", "parallel"),
        name="in_proj",
    )(x, norm_g.reshape(1, D_MODEL), sc, sh, w_pack, qg.reshape(1, HEAD_DIM), kg.reshape(1, HEAD_DIM))


def _ssd_prompt_kernel(xbc_ref, dt_ref, zg_ref, cw_ref, cb_ref, dtb_ref, alog_ref, dsk_ref, ng_ref,
                       ex_ref, ext_ref, y_o, st_o, xe_scr, st_scr):
    c = pl.program_id(1)
    q_ = SSD_CHUNK
    halo = SUBLANES

    @pl.when(c == 0)
    def _():
        xe_scr[0:halo, :] = jnp.zeros((halo, CONV_DIM), F32)
        st_scr[...] = jnp.zeros_like(st_scr)

    xe_scr[halo:halo + q_, :] = xbc_ref[0]
    conv = cb_ref[...] + sum(
        cw_ref[j:j + 1, :] * xe_scr[halo - (SSM_CONV - 1) + j:halo - (SSM_CONV - 1) + j + q_, :]
        for j in range(SSM_CONV))
    xe_scr[0:halo, :] = xe_scr[q_:q_ + halo, :]
    xc = _silu(conv)
    xs = xc[:, :D_SSM]
    bm = xc[:, D_SSM:D_SSM + SSM_GROUPS * SSM_STATE]
    cm = xc[:, D_SSM + SSM_GROUPS * SSM_STATE:]

    dt = _softplus(dt_ref[0] + dtb_ref[...])
    da = dt * (-jnp.exp(alog_ref[...]))
    row = lax.broadcasted_iota(jnp.int32, (q_, q_), 0)
    col = lax.broadcasted_iota(jnp.int32, (q_, q_), 1)
    causal = row >= col
    tri = jnp.where(causal, 1.0, 0.0).astype(BF16)
    acs = _dot_sel_lhs(tri, da)
    acs_t = acs.T
    acs_last = acs[q_ - 1:q_, :]
    ex = ex_ref[...]
    dt_x = _dot_sel_rhs(dt, ex)
    eacs_x = _dot_sel_rhs(jnp.exp(acs), ex)
    dte_x = _dot_sel_rhs(jnp.exp(acs_last - acs), ex)
    cdec = jnp.exp(jnp.broadcast_to(acs_t[:, q_ - 1:q_], (LANES, SSM_STATE)))
    cdec_rows = _dot_sel_lhs(ext_ref[...], cdec)

    xdt = xs * dt_x
    xdtd = xdt * dte_x
    gw = SSM_HEADS // SSM_GROUPS * SSM_HEAD_DIM
    gated = []
    for g in range(SSM_GROUPS):
        bg = bm[:, g * SSM_STATE:(g + 1) * SSM_STATE].astype(BF16)
        cg = cm[:, g * SSM_STATE:(g + 1) * SSM_STATE].astype(BF16)
        cb = _dot_nt(cg, bg)
        yd = []
        for hl in range(SSM_HEADS // SSM_GROUPS):
            h = g * (SSM_HEADS // SSM_GROUPS) + hl
            seg = acs[:, h:h + 1] - acs_t[h:h + 1, :]
            lmat = jnp.exp(jnp.where(causal, seg, -jnp.inf))
            yd.append(jnp.dot((cb * lmat).astype(BF16),
                              xdt[:, h * SSM_HEAD_DIM:(h + 1) * SSM_HEAD_DIM].astype(BF16),
                              preferred_element_type=F32))
        y_diag = jnp.concatenate(yd, axis=-1)
        st_old = st_scr[g * gw:(g + 1) * gw, :]
        y_off = _dot_nt(cg, st_old.astype(BF16)) * eacs_x[:, g * gw:(g + 1) * gw]
        st_new = _dot_tn(xdtd[:, g * gw:(g + 1) * gw].astype(BF16), bg)
        st_scr[g * gw:(g + 1) * gw, :] = st_old * cdec_rows[g * gw:(g + 1) * gw, :] + st_new
        yg = y_diag + y_off + dsk_ref[:, g * gw:(g + 1) * gw] * xs[:, g * gw:(g + 1) * gw]
        gated.append(yg * _silu(zg_ref[0, :, g * gw:(g + 1) * gw]))

    y_o[0] = (_rms(jnp.concatenate(gated, axis=-1)) * ng_ref[...]).astype(y_o.dtype)

    @pl.when(c == pl.num_programs(1) - 1)
    def _():
        st_o[0] = st_scr[...]


def _ssd_prompt(xbc, dt, zg, conv_w, conv_b, dtb_pad, alog_pad, dsk_x, norm_g, ex, ext):
    b_, s_, _ = xbc.shape
    nc = s_ // SSD_CHUNK
    return pl.pallas_call(
        _ssd_prompt_kernel,
        grid=(b_, nc),
        in_specs=[
            pl.BlockSpec((1, SSD_CHUNK, CONV_DIM), lambda b, c: (b, c, 0)),
            pl.BlockSpec((1, SSD_CHUNK, LANES), lambda b, c: (b, c, 0)),
            pl.BlockSpec((1, SSD_CHUNK, D_SSM), lambda b, c: (b, c, 0)),
            _resident((SSM_CONV, CONV_DIM)),
            _resident((1, CONV_DIM)),
            _resident((1, LANES)),
            _resident((1, LANES)),
            _resident((1, D_SSM)),
            _resident((1, D_SSM)),
            _resident((LANES, D_SSM)),
            _resident((D_SSM, LANES)),
        ],
        out_specs=[
            pl.BlockSpec((1, SSD_CHUNK, D_SSM), lambda b, c: (b, c, 0)),
            pl.BlockSpec((1, D_SSM, SSM_STATE), lambda b, c: (b, 0, 0)),
        ],
        out_shape=[
            jax.ShapeDtypeStruct((b_, s_, D_SSM), F32),
            jax.ShapeDtypeStruct((b_, D_SSM, SSM_STATE), F32),
        ],
        scratch_shapes=[
            pltpu.VMEM((SSD_CHUNK + SUBLANES, CONV_DIM), F32),
            pltpu.VMEM((D_SSM, SSM_STATE), F32),
        ],
        compiler_params=_cparams("parallel", "arbitrary"),
        name="ssd_prompt",
    )(xbc, dt, zg, conv_w, conv_b, dtb_pad, alog_pad, dsk_x, norm_g, ex, ext)


def _f32_key(x):
    b = lax.bitcast_convert_type(x, jnp.int32)
    return b ^ ((b >> 31) & 0x7FFFFFFF)


def _key_f32(k):
    return lax.bitcast_convert_type(k ^ ((k >> 31) & 0x7FFFFFFF), F32)


def _fold(x, rows, op):
    return op(x.reshape(x.shape[0] // rows, rows, x.shape[1]), axis=0)


def _dsa_prompt_kernel(topk, qi_ref, wi_ref, ki_ref, q_ref, k_ref, v_ref, tz_ref, o_ref,
                       sc_scr, s_scr, acc_scr):
    qb = pl.program_id(1)
    tq = q_ref.shape[2]
    ch = ATT_CHUNK
    n_ch = qb + 1

    def chunk_at(j):
        return pl.ds(pl.multiple_of(j * ch, ch), ch)

    qi_all = jnp.concatenate([qi_ref[0, h * IDX_DIM:(h + 1) * IDX_DIM, :] for h in range(IDX_HEADS)], axis=1)
    w_rows = [wi_ref[0, h:h + 1, :] for h in range(IDX_HEADS)]

    def score_chunk(j, diagonal):
        d = jnp.dot(ki_ref[0, chunk_at(j), :], qi_all, preferred_element_type=F32)
        acc = w_rows[0] * jnp.maximum(d[:, 0:tq], 0.0)
        for h in range(1, IDX_HEADS):
            acc = acc + w_rows[h] * jnp.maximum(d[:, h * tq:(h + 1) * tq], 0.0)
        if diagonal:
            causal = (lax.broadcasted_iota(jnp.int32, (ch, tq), 0) <= lax.broadcasted_iota(jnp.int32, (ch, tq), 1))
            acc = jnp.where(causal, acc, -jnp.inf)
        sc_scr[chunk_at(j), :] = acc

    def score_body(j, c):
        score_chunk(j, False)
        return c

    lax.fori_loop(0, qb, score_body, 0)
    score_chunk(qb, True)

    def chunks_reduce(fn, op, init):
        def body(j, acc):
            rows = pl.ds(pl.multiple_of(j * (2 * ch), 2 * ch), 2 * ch)
            return op(jnp.stack([acc, _fold(fn(sc_scr[rows, :]), SCAN_FOLD, op)]), axis=0)
        return op(lax.fori_loop(0, (n_ch + 1) // 2, body, jnp.full((SCAN_FOLD, tq), init, F32)), axis=0, keepdims=True)

    @pl.when(n_ch % 2 == 1)
    def _():
        sc_scr[chunk_at(n_ch), :] = jnp.full((ch, tq), -jnp.inf, F32)

    def count(cmp, thr):
        return chunks_reduce(lambda t: jnp.where(cmp(t, thr), 1.0, 0.0), jnp.sum, 0.0)

    count_ge = functools.partial(count, lambda t, x: t >= x)

    n_valid = qb * tq + lax.broadcasted_iota(jnp.int32, (1, tq), 1) + 1
    k_row = jnp.minimum(n_valid, topk).astype(F32)
    mx = chunks_reduce(lambda t: t, jnp.max, -jnp.inf)
    mn = chunks_reduce(lambda t: jnp.where(t == -jnp.inf, jnp.inf, t), jnp.min, jnp.inf)
    top_tied = count_ge(mx) >= k_row
    take_all = n_valid <= topk
    state = (mn, mx, jnp.where(top_tied, mx, mn), jnp.where(jnp.logical_or(top_tied, take_all), 1, 0).astype(jnp.int32))

    def value_step(_, s):
        lo, hi, thr, done = s
        mid = 0.5 * lo + 0.5 * hi
        inside = jnp.where(mid > lo, jnp.where(mid < hi, 1, 0), 0)
        cnt = count_ge(mid)
        ge = cnt >= k_row
        finished = jnp.where(cnt == k_row, inside, 0) * (1 - done)
        lo = jnp.where(inside == 1, jnp.where(ge, mid, lo), lo)
        hi = jnp.where(inside == 1, jnp.where(ge, hi, mid), hi)
        return lo, hi, jnp.where(finished == 1, mid, thr), done + finished

    def snap_step(s):
        lo, hi, thr, done = s
        cand = chunks_reduce(lambda t: jnp.where(t < hi, t, -jnp.inf), jnp.max, -jnp.inf)
        ge = count_ge(cand) >= k_row
        finished = jnp.where(ge, 1, 0) * (1 - done)
        return lo, jnp.where(ge, hi, cand), jnp.where(finished == 1, cand, thr), done + finished

    def key_step(s):
        lo, hi, thr, done = s
        klo, khi = _f32_key(lo), _f32_key(hi)
        mid_k = (klo >> 1) + (khi >> 1) + (klo & khi & 1)
        mid = _key_f32(mid_k)
        cnt = count_ge(mid)
        ge = cnt >= k_row
        collapsed = mid_k == klo
        finished = jnp.where(collapsed, 1, jnp.where(cnt == k_row, 1, 0)) * (1 - done)
        thr = jnp.where(finished == 1, jnp.where(collapsed, lo, mid), thr)
        return jnp.where(ge, mid, lo), jnp.where(ge, hi, mid), thr, done + finished

    state = lax.fori_loop(0, VALUE_STEPS, value_step, state)

    def refine(s):
        it = s[4]
        st = snap_step(s[:4])
        st = lax.cond(it >= SNAP_ONLY_TRIPS, key_step, lambda x: x, st)
        return (*st, it + 1)

    thr = lax.while_loop(lambda s: jnp.logical_and(jnp.min(s[3]) == 0, s[4] < SNAP_ONLY_TRIPS + 40), refine,
                         (*state, jnp.int32(0)))[2]

    surplus = jnp.max(count_ge(thr) - k_row)

    @pl.when(surplus <= 0.0)
    def _():
        def select_chunk(j, c):
            sc_scr[chunk_at(j), :] = jnp.where(sc_scr[chunk_at(j), :] >= thr, 0.0, NEG_BIG)
            return c
        lax.fori_loop(0, n_ch, select_chunk, 0)

    @pl.when(surplus > 0.0)
    def _():
        need = k_row - count(lambda t, x: t > x, thr)
        lower = jnp.where(lax.broadcasted_iota(jnp.int32, (ch, ch), 1) <= lax.broadcasted_iota(jnp.int32, (ch, ch), 0),
                          1.0, 0.0).astype(BF16)
        ones_r = jnp.ones((SUBLANES, ch), BF16)

        def select_chunk(j, run):
            t = sc_scr[chunk_at(j), :]
            eq = t == thr
            eqb = jnp.where(eq, 1.0, 0.0).astype(BF16)
            rank = jnp.dot(lower, eqb, preferred_element_type=F32) + run
            tie_bias = jnp.where(rank <= need, 0.0, NEG_BIG)
            sc_scr[chunk_at(j), :] = jnp.where(t > thr, 0.0, jnp.where(eq, tie_bias, NEG_BIG))
            return run + jnp.dot(ones_r, eqb, preferred_element_type=F32)[0:1, :]

        lax.fori_loop(0, n_ch, select_chunk, jnp.zeros((1, tq), F32))

    scale = HEAD_DIM ** -0.5 * LOG2_E
    cols = KV_REP * tq
    n_far = jnp.maximum(qb - 1, 0)
    for gp in range(N_KV_HEADS // 2):
        groups = (2 * gp, 2 * gp + 1)
        q_cols = [jnp.concatenate([q_ref[0, (g * KV_REP + r) * HEAD_DIM:(g * KV_REP + r + 1) * HEAD_DIM, :]
                                   for r in range(KV_REP)], axis=1) for g in groups]

        def logits_chunks(jj, mx, near, width):
            mx = list(mx)
            for j in [jj * width + i for i in range(width)]:
                selb = sc_scr[chunk_at(j), :]
                for gi, g in enumerate(groups):
                    s = jnp.dot(k_ref[0, chunk_at(j), g * HEAD_DIM:(g + 1) * HEAD_DIM], q_cols[gi],
                                preferred_element_type=F32) * scale
                    if near:
                        bias = jnp.concatenate([tz_ref[qb - j, g * KV_REP + r] + selb for r in range(KV_REP)], axis=1)
                    else:
                        bias = jnp.concatenate([selb] * KV_REP, axis=1)
                    s = s + bias
                    s_scr[gi, chunk_at(j), :] = s
                    mx[gi] = jnp.maximum(mx[gi], _fold(s, ATT_FOLD, jnp.max))
            return tuple(mx)

        mx = (jnp.full((ATT_FOLD, cols), NEG_BIG, F32),) * 2
        mx = lax.fori_loop(0, n_far // 2, functools.partial(logits_chunks, near=False, width=2), mx)
        mx = lax.fori_loop(n_far // 2 * 2, n_far, functools.partial(logits_chunks, near=False, width=1), mx)
        mx = lax.fori_loop(n_far, n_ch, functools.partial(logits_chunks, near=True, width=1), mx)
        m_row = [jnp.max(m, axis=0, keepdims=True) for m in mx]
        acc_scr[...] = jnp.zeros_like(acc_scr)

        def weights_chunks(jj, l8, width):
            l8 = list(l8)
            for j in [jj * width + i for i in range(width)]:
                for gi, g in enumerate(groups):
                    p = jnp.exp2(s_scr[gi, chunk_at(j), :] - m_row[gi])
                    l8[gi] = l8[gi] + _fold(p, ATT_FOLD, jnp.sum)
                    acc_scr[gi] += jnp.dot(v_ref[0, g * HEAD_DIM:(g + 1) * HEAD_DIM, chunk_at(j)], p.astype(BF16),
                                           preferred_element_type=F32)
            return tuple(l8)

        l8 = (jnp.zeros((ATT_FOLD, cols), F32),) * 2
        l8 = lax.fori_loop(0, n_ch // 2, functools.partial(weights_chunks, width=2), l8)
        l8 = lax.fori_loop(n_ch // 2 * 2, n_ch, functools.partial(weights_chunks, width=1), l8)
        for gi, g in enumerate(groups):
            out = acc_scr[gi] * (1.0 / jnp.sum(l8[gi], axis=0, keepdims=True))
            for r in range(KV_REP):
                h = g * KV_REP + r
                o_ref[0, :, h * HEAD_DIM:(h + 1) * HEAD_DIM] = out[:, r * tq:(r + 1) * tq].T.astype(o_ref.dtype)


def _dsa_prompt(qi_t, wi_t, ki_b, q_t, k_b, v_t, tz, topk):
    b_, s_, _ = k_b.shape
    tq = min(Q_BLOCK, s_)
    assert tq == ATT_CHUNK and s_ % (2 * tq) == 0
    once = pl.Buffered(1)
    q_tile = lambda rows: pl.BlockSpec((1, rows, tq), lambda b, i: (b, 0, i))
    return pl.pallas_call(
        functools.partial(_dsa_prompt_kernel, topk),
        grid=(b_, s_ // tq),
        in_specs=[
            q_tile(IDX_HEADS * IDX_DIM),
            q_tile(IDX_HEADS),
            pl.BlockSpec((1, s_, IDX_DIM), lambda b, i: (b, 0, 0), pipeline_mode=once),
            q_tile(D_ATT),
            pl.BlockSpec((1, s_, D_KV), lambda b, i: (b, 0, 0), pipeline_mode=once),
            pl.BlockSpec((1, D_KV, s_), lambda b, i: (b, 0, 0), pipeline_mode=once),
            _resident(tz.shape),
        ],
        out_specs=pl.BlockSpec((1, tq, D_ATT), lambda b, i: (b, i, 0)),
        out_shape=jax.ShapeDtypeStruct((b_, s_, D_ATT), BF16),
        scratch_shapes=[
            pltpu.VMEM((s_, tq), F32),
            pltpu.VMEM((2, s_, KV_REP * tq), F32),
            pltpu.VMEM((2, HEAD_DIM, KV_REP * tq), F32),
        ],
        compiler_params=_cparams("parallel", "arbitrary"),
        name="dsa_prompt",
    )(qi_t, wi_t, ki_b, q_t, k_b, v_t, tz)


def _mix_and_norm(att_ref, ssd_ref, x_ref, g1_ref, sc2_ref, sh2_ref, n2_ref, wo_ref):
    o = (jnp.dot(att_ref[0].astype(BF16), wo_ref[:D_ATT, :], preferred_element_type=F32)
         + jnp.dot(ssd_ref[0].astype(BF16), wo_ref[D_ATT:, :], preferred_element_type=F32))
    x1 = x_ref[0] + g1_ref[0] * o
    h2 = ((_rms(x1) * n2_ref[...]) * (1.0 + sc2_ref[0]) + sh2_ref[0]).astype(BF16)
    return x1, h2


def _ffn_prompt_kernel(att_ref, ssd_ref, x_ref, g1_ref, sc2_ref, sh2_ref, g2_ref, n2_ref, wo_ref, wu_ref,
                       cw_ref, cb_ref, wd_ref, y_o, tail_o, acc_scr, ug_scr, uv_scr, carry_scr):
    i = pl.program_id(1)
    tm = x_ref.shape[1]
    halo = SUBLANES

    @pl.when(i == 0)
    def _():
        carry_scr[...] = jnp.zeros_like(carry_scr)

    x1, h2 = _mix_and_norm(att_ref, ssd_ref, x_ref, g1_ref, sc2_ref, sh2_ref, n2_ref, wo_ref)
    acc_scr[...] = jnp.zeros_like(acc_scr)

    def conv_cols(scr, c0):
        u = jnp.dot(h2, wu_ref[:, c0:c0 + FF_CHUNK], preferred_element_type=F32)
        scr[0:halo, :] = carry_scr[:, c0:c0 + FF_CHUNK]
        scr[halo:halo + tm, :] = u
        carry_scr[:, c0:c0 + FF_CHUNK] = u[tm - halo:, :]
        return cb_ref[:, c0:c0 + FF_CHUNK] + sum(
            cw_ref[j:j + 1, c0:c0 + FF_CHUNK] * scr[halo - (FFN_CONV - 1) + j:halo - (FFN_CONV - 1) + j + tm, :]
            for j in range(FFN_CONV))

    for jc in range(D_FF // FF_CHUNK):
        gate = conv_cols(ug_scr, jc * FF_CHUNK)
        val = conv_cols(uv_scr, D_FF + jc * FF_CHUNK)
        a = (_silu(gate) * val).astype(BF16)
        acc_scr[...] += jnp.dot(a, wd_ref[jc * FF_CHUNK:(jc + 1) * FF_CHUNK, :], preferred_element_type=F32)

    y_o[0] = x1 + g2_ref[0] * acc_scr[...]
    tail_o[0] = carry_scr[...]


def _ffn_prompt(att, ssd, x, g1, sc2, sh2, g2, norm2_g, wo_b, wu_b, cw, cb, wd_b):
    b_, s_, _ = x.shape
    tm = min(ROW_TILE, s_)
    nt = s_ // tm
    row = lambda w: pl.BlockSpec((1, tm, w), lambda b, i: (b, i, 0))
    mod = pl.BlockSpec((1, 1, D_MODEL), lambda b, i: (b, 0, 0))
    return pl.pallas_call(
        _ffn_prompt_kernel,
        grid=(b_, nt),
        in_specs=[row(D_ATT), row(D_SSM), row(D_MODEL), mod, mod, mod, mod, _resident((1, D_MODEL)),
                  _resident((D_MIX, D_MODEL)), _resident((D_MODEL, 2 * D_FF)), _resident((FFN_CONV, 2 * D_FF)),
                  _resident((1, 2 * D_FF)), _resident((D_FF, D_MODEL))],
        out_specs=[row(D_MODEL), pl.BlockSpec((1, SUBLANES, 2 * D_FF), lambda b, i: (b, 0, 0))],
        out_shape=[jax.ShapeDtypeStruct((b_, s_, D_MODEL), F32),
                   jax.ShapeDtypeStruct((b_, SUBLANES, 2 * D_FF), F32)],
        scratch_shapes=[pltpu.VMEM((tm, D_MODEL), F32), pltpu.VMEM((tm + SUBLANES, FF_CHUNK), F32),
                        pltpu.VMEM((tm + SUBLANES, FF_CHUNK), F32), pltpu.VMEM((SUBLANES, 2 * D_FF), F32)],
        compiler_params=_cparams("parallel", "arbitrary"),
        name="ffn_prompt",
    )(att, ssd, x, g1, sc2, sh2, g2, norm2_g.reshape(1, D_MODEL), wo_b, wu_b, cw, cb, wd_b)


def _ffn_sample_kernel(att_ref, ssd_ref, x_ref, g1_ref, sc2_ref, sh2_ref, g2_ref, n2_ref, wo_ref, wu_ref,
                       cw_ref, cb_ref, wd_ref, h0_ref, h1_ref, y_o, u_o, acc_scr):
    x1, h2 = _mix_and_norm(att_ref, ssd_ref, x_ref, g1_ref, sc2_ref, sh2_ref, n2_ref, wo_ref)
    acc_scr[...] = jnp.zeros_like(acc_scr)

    def conv_cols(c0):
        u = jnp.dot(h2, wu_ref[:, c0:c0 + FF_CHUNK], preferred_element_type=F32)
        u_o[:, c0:c0 + FF_CHUNK] = u
        sl = slice(c0, c0 + FF_CHUNK)
        return (cb_ref[:, sl] + cw_ref[0:1, sl] * h0_ref[:, sl] + cw_ref[1:2, sl] * h1_ref[:, sl]
                + cw_ref[2:3, sl] * u)

    for jc in range(D_FF // FF_CHUNK):
        gate = conv_cols(jc * FF_CHUNK)
        val = conv_cols(D_FF + jc * FF_CHUNK)
        a = (_silu(gate) * val).astype(BF16)
        acc_scr[...] += jnp.dot(a, wd_ref[jc * FF_CHUNK:(jc + 1) * FF_CHUNK, :], preferred_element_type=F32)

    y_o[0] = x1 + g2_ref[0] * acc_scr[...]


def _ffn_sample(att, ssd, x, g1, sc2, sh2, g2, norm2_g, wo_b, wu_b, cw, cb, wd_b, hist0, hist1):
    n = x.shape[1]
    full = lambda *shape: pl.BlockSpec(shape, lambda: (0,) * len(shape))
    r3 = lambda w: full(1, n, w)
    return pl.pallas_call(
        _ffn_sample_kernel,
        in_specs=[r3(D_ATT), r3(D_SSM), r3(D_MODEL), r3(D_MODEL), r3(D_MODEL), r3(D_MODEL), r3(D_MODEL),
                  full(1, D_MODEL), full(D_MIX, D_MODEL), full(D_MODEL, 2 * D_FF), full(FFN_CONV, 2 * D_FF),
                  full(1, 2 * D_FF), full(D_FF, D_MODEL), full(n, 2 * D_FF), full(n, 2 * D_FF)],
        out_specs=[r3(D_MODEL), full(n, 2 * D_FF)],
        out_shape=[jax.ShapeDtypeStruct((1, n, D_MODEL), F32), jax.ShapeDtypeStruct((n, 2 * D_FF), F32)],
        scratch_shapes=[pltpu.VMEM((n, D_MODEL), F32)],
        compiler_params=pltpu.CompilerParams(vmem_limit_bytes=VMEM_LIMIT),
        name="ffn_sample",
    )(att, ssd, x, g1, sc2, sh2, g2, norm2_g.reshape(1, D_MODEL), wo_b, wu_b, cw, cb, wd_b, hist0, hist1)


def _ssd_sample_kernel(xbc_ref, h0_ref, h1_ref, h2_ref, dt_ref, zg_ref, cw_ref, cb_ref, dtb_ref, alog_ref,
                       dsk_ref, ng_ref, ex_ref, st_ref, y_o, st_o):
    nb = xbc_ref.shape[0]
    conv = (cb_ref[...] + cw_ref[0:1, :] * h0_ref[...] + cw_ref[1:2, :] * h1_ref[...]
            + cw_ref[2:3, :] * h2_ref[...] + cw_ref[3:4, :] * xbc_ref[...])
    xc = _silu(conv)
    xs = xc[:, :D_SSM]
    bm = xc[:, D_SSM:D_SSM + SSM_GROUPS * SSM_STATE]
    cm = xc[:, D_SSM + SSM_GROUPS * SSM_STATE:]
    dt = _softplus(dt_ref[...] + dtb_ref[...])
    dec = jnp.exp(dt * (-jnp.exp(alog_ref[...])))
    ex = ex_ref[...]
    xdt = xs * _dot_sel_rhs(dt, ex)
    dec_x = _dot_sel_rhs(dec, ex)
    gw = SSM_HEADS // SSM_GROUPS * SSM_HEAD_DIM
    ones_n = jnp.ones((SUBLANES, SSM_STATE), BF16)
    row0 = lax.broadcasted_iota(jnp.int32, (SUBLANES, 1), 0) == 0

    def only_row0(v):
        return jnp.where(row0, jnp.broadcast_to(v, (SUBLANES, v.shape[1])), 0.0)

    ys = []
    for i in range(nb):
        yrow = []
        for g in range(SSM_GROUPS):
            hs = st_ref[i, g * gw:(g + 1) * gw, :]
            dec_rows = only_row0(dec_x[i:i + 1, g * gw:(g + 1) * gw])
            xdt_rows = only_row0(xdt[i:i + 1, g * gw:(g + 1) * gw])
            b_rows = only_row0(bm[i:i + 1, g * SSM_STATE:(g + 1) * SSM_STATE])
            c_rows = only_row0(cm[i:i + 1, g * SSM_STATE:(g + 1) * SSM_STATE])
            dec_b = sum(_dot_tn(p, ones_n) for p in _split3(dec_rows))
            upd = sum(_dot_tn(p, q) for p in _split3(xdt_rows)[:2] for q in _split3(b_rows)[:2])
            hn = hs * dec_b + upd
            st_o[i, g * gw:(g + 1) * gw, :] = hn
            yg = sum(_dot_nt(p, q) for p in _split3(c_rows)[:2] for q in _split3(hn)[:2])
            yrow.append(yg[0:1, :])
        ys.append(jnp.concatenate(yrow, axis=-1))
    y = jnp.concatenate(ys, axis=0) + dsk_ref[...] * xs
    y = y * _silu(zg_ref[...])
    y_o[...] = _rms(y) * ng_ref[...]


def _ssd_sample(xbc, hist, dt, zg, conv_w, conv_b, dtb_pad, alog_pad, dsk_x, norm_g, ex, state):
    n = xbc.shape[0]
    nb = SUBLANES
    rows = lambda w: pl.BlockSpec((nb, w), lambda i: (i, 0))
    st = pl.BlockSpec((nb, D_SSM, SSM_STATE), lambda i: (i, 0, 0))
    return pl.pallas_call(
        _ssd_sample_kernel,
        grid=(n // nb,),
        in_specs=[rows(CONV_DIM), rows(CONV_DIM), rows(CONV_DIM), rows(CONV_DIM), rows(LANES), rows(D_SSM),
                  _resident((SSM_CONV, CONV_DIM)), _resident((1, CONV_DIM)), _resident((1, LANES)),
                  _resident((1, LANES)), _resident((1, D_SSM)), _resident((1, D_SSM)), _resident((LANES, D_SSM)), st],
        out_specs=[rows(D_SSM), st],
        out_shape=[jax.ShapeDtypeStruct((n, D_SSM), F32), jax.ShapeDtypeStruct((n, D_SSM, SSM_STATE), F32)],
        compiler_params=_cparams("parallel"),
        name="ssd_sample",
    )(xbc, hist[0], hist[1], hist[2], dt, zg, conv_w, conv_b, dtb_pad, alog_pad, dsk_x, norm_g, ex, state)


def _idx_sample_kernel(pt_ref, qi_ref, wi_ref, kin_ref, pool_ref, o_ref, kbuf, sem):
    b = pl.program_id(0)
    nb = pl.num_programs(0)
    n_pages = kbuf.shape[1]
    page = kbuf.shape[3]

    def page_copy(seq, slot, j):
        return pltpu.make_async_copy(pool_ref.at[pt_ref[seq, j]], kbuf.at[slot, j], sem.at[slot])

    def fetch(seq, slot):
        def body(j, c):
            page_copy(seq, slot, j).start()
            return c
        lax.fori_loop(0, n_pages, body, 0, unroll=8)

    @pl.when(b == 0)
    def _():
        fetch(0, 0)

    @pl.when(b + 1 < nb)
    def _():
        fetch(b + 1, (b + 1) % 2)

    slot = b % 2

    def wait_body(j, c):
        page_copy(b, slot, j).wait()
        return c
    lax.fori_loop(0, n_pages, wait_body, 0)

    qi = qi_ref[0]
    w = wi_ref[0]

    span = PAGES_PER_DOT * page

    def page_scores(j, c):
        keys = jnp.concatenate([kbuf[slot, PAGES_PER_DOT * j + i] for i in range(PAGES_PER_DOT)], axis=1)
        d = jnp.dot(qi, keys.astype(BF16), preferred_element_type=F32)
        o_ref[0, :, pl.ds(pl.multiple_of(j * span, span), span)] = jnp.sum(w * jnp.maximum(d, 0.0), axis=0, keepdims=True)
        return c

    lax.fori_loop(0, n_pages // PAGES_PER_DOT, page_scores, 0, unroll=2)
    knew = kin_ref[0].astype(BF16).astype(F32)
    dn = jnp.sum(qi.astype(F32) * knew, axis=-1, keepdims=True)
    s_new = jnp.sum(w * jnp.maximum(dn, 0.0), axis=0, keepdims=True)
    lane = lax.broadcasted_iota(jnp.int32, (1, LANES), 1)
    o_ref[0, :, n_pages * page:] = jnp.where(lane == 0, s_new, -jnp.inf)


def _idx_sample(page_table, qi3, wi3, ki_new3, pool_t):
    n, n_pages = page_table.shape
    page = pool_t.shape[2]
    assert n_pages % (8 * PAGES_PER_DOT) == 0
    past = n_pages * page
    grid_spec = pltpu.PrefetchScalarGridSpec(
        num_scalar_prefetch=1,
        grid=(n,),
        in_specs=[
            pl.BlockSpec((1, IDX_HEADS, IDX_DIM), lambda b, pt: (b, 0, 0)),
            pl.BlockSpec((1, IDX_HEADS, 1), lambda b, pt: (b, 0, 0)),
            pl.BlockSpec((1, 1, IDX_DIM), lambda b, pt: (b, 0, 0)),
            pl.BlockSpec(memory_space=pl.ANY),
        ],
        out_specs=pl.BlockSpec((1, 1, past + LANES), lambda b, pt: (b, 0, 0)),
        scratch_shapes=[pltpu.VMEM((2, n_pages, IDX_DIM, page), F32), pltpu.SemaphoreType.DMA((2,))],
    )
    return pl.pallas_call(
        _idx_sample_kernel,
        grid_spec=grid_spec,
        out_shape=jax.ShapeDtypeStruct((n, 1, past + LANES), F32),
        compiler_params=_cparams("arbitrary"),
        name="idx_sample",
    )(page_table, qi3, wi3, ki_new3, pool_t)


def _att_sample_kernel(q_ref, k_ref, v_ref, bias_ref, o_ref):
    nb = q_ref.shape[0]
    scale = HEAD_DIM ** -0.5
    head_group = lax.broadcasted_iota(jnp.int32, (N_HEADS, 1), 0) // KV_REP
    for i in range(nb):
        q = q_ref[i].astype(BF16)
        bias = bias_ref[i]
        out = jnp.zeros((N_HEADS, HEAD_DIM), F32)
        for g in range(N_KV_HEADS):
            kg = k_ref[i, :, g * HEAD_DIM:(g + 1) * HEAD_DIM].astype(BF16)
            vg = v_ref[i, :, g * HEAD_DIM:(g + 1) * HEAD_DIM].astype(BF16)
            s = _dot_nt(q, kg) * scale + bias
            m = jnp.max(s, axis=-1, keepdims=True)
            p = jnp.exp(s - m)
            p = p * (1.0 / jnp.sum(p, axis=-1, keepdims=True))
            og = jnp.dot(p.astype(BF16), vg, preferred_element_type=F32)
            out = jnp.where(head_group == g, og, out)
        o_ref[i] = out


def _att_sample(q3, k_sel, v_sel, bias):
    n, kk, _ = k_sel.shape
    nb = SUBLANES
    return pl.pallas_call(
        _att_sample_kernel,
        grid=(n // nb,),
        in_specs=[
            pl.BlockSpec((nb, N_HEADS, HEAD_DIM), lambda i: (i, 0, 0)),
            pl.BlockSpec((nb, kk, D_KV), lambda i: (i, 0, 0)),
            pl.BlockSpec((nb, kk, D_KV), lambda i: (i, 0, 0)),
            pl.BlockSpec((nb, N_HEADS, kk), lambda i: (i, 0, 0)),
        ],
        out_specs=pl.BlockSpec((nb, N_HEADS, HEAD_DIM), lambda i: (i, 0, 0)),
        out_shape=jax.ShapeDtypeStruct((n, N_HEADS, HEAD_DIM), F32),
        compiler_params=_cparams("parallel"),
        name="att_sample",
    )(q3, k_sel, v_sel, bias)


def _pack_w_in(w_in):
    offs = np.cumsum((0,) + IN_SPLITS)
    parts = [w_in[:, offs[i]:offs[i + 1]] for i in range(len(IN_SPLITS))]
    pad = lambda a: jnp.pad(a, ((0, 0), (0, LANES - a.shape[1])))
    q, k, v, qi, ki, wi, zg, xbc, dt = parts
    return jnp.concatenate([q, k, v, qi, pad(ki), pad(wi), zg, xbc, pad(dt)], axis=1).astype(BF16)


def _t5_bucket(dist):
    dist = jnp.maximum(dist, 0)
    max_exact = N_BUCKETS // 2
    d = jnp.maximum(dist, 1).astype(F32)
    large = max_exact + (jnp.log(d / max_exact) / math.log(MAX_DISTANCE / max_exact) * (N_BUCKETS - max_exact)).astype(jnp.int32)
    large = jnp.minimum(large, N_BUCKETS - 1)
    return jnp.where(dist < max_exact, dist, large)


def _rel_bias_tables(rel_bias, tq):
    assert ATT_CHUNK == tq and tq >= MAX_DISTANCE
    n = 2 * tq
    idx = jnp.arange(n, dtype=jnp.int32)
    query_minus_key = jnp.where(idx < tq, idx, idx - n)
    far = rel_bias[_t5_bucket(jnp.int32(MAX_DISTANCE))]
    tabs = []
    for d0 in (0, tq):
        f = (rel_bias[_t5_bucket(d0 + query_minus_key)] - far[None, :]) * LOG2_E
        flat = jnp.tile(f.T, (1, ATT_CHUNK))
        tabs.append(flat[:, :ATT_CHUNK * (n - 1)].reshape(-1, ATT_CHUNK, n - 1)[:, :, :tq])
    return jnp.stack(tabs)


def _head_expand():
    h = jnp.arange(LANES, dtype=jnp.int32)[:, None]
    c = jnp.arange(D_SSM, dtype=jnp.int32)[None, :]
    ex = (c // SSM_HEAD_DIM == h).astype(BF16)
    return ex, ex.T


def _pad_lanes(v):
    return jnp.pad(v.astype(F32), (0, LANES - v.shape[0])).reshape(1, LANES)


def kernel(x_prompt, x_sample, cache_k, cache_v, cache_kidx, state_ssm, state_conv_ssd, state_conv_ffn, page_table,
           c_prompt, c_sample, rel_bias, norm1_g, w_ada, b_ada, w_in, q_norm_g, k_norm_g, conv_ssd_w, conv_ssd_b,
           dt_bias, a_log, d_skip, ssd_norm_g, w_out, norm2_g, w_up, conv_ffn_w, conv_ffn_b, w_down):
    depth = w_in.shape[0]
    assert depth == 1
    bp, s_, _ = x_prompt.shape
    ns = x_sample.shape[0]
    assert x_sample.shape[1] == 1
    lyr = 0
    n_pages = page_table.shape[1]
    page = cache_k.shape[2]
    past = n_pages * page
    topk_p = min(TOPK_MAX, s_ // TOPK_DIV)
    topk_s = min(TOPK_MAX, (past + 1) // TOPK_DIV)

    w_pack = _pack_w_in(w_in[lyr])
    wo_b = w_out[lyr].astype(BF16)
    wu_b = w_up[lyr].astype(BF16)
    wd_b = w_down[lyr].astype(BF16)
    ex, ext = _head_expand()
    dtb_pad = _pad_lanes(dt_bias[lyr])
    alog_pad = _pad_lanes(a_log[lyr])
    dsk_x = jnp.repeat(d_skip[lyr].astype(F32), SSM_HEAD_DIM).reshape(1, D_SSM)
    ssd_g = ssd_norm_g[lyr].reshape(1, D_SSM)
    cw_ssd, cb_ssd = conv_ssd_w[lyr], conv_ssd_b[lyr].reshape(1, CONV_DIM)
    cw_ffn, cb_ffn = conv_ffn_w[lyr], conv_ffn_b[lyr].reshape(1, 2 * D_FF)
    tz = _rel_bias_tables(rel_bias.astype(F32), min(Q_BLOCK, s_))

    n_c = bp + ns
    c_all = jnp.pad(jnp.concatenate([c_prompt, c_sample], axis=0), ((0, (-n_c) % SUBLANES), (0, 0)))
    mod = _ada(c_all, w_ada[lyr], b_ada[lyr])
    mods_p = [mod[:bp, i * D_MODEL:(i + 1) * D_MODEL].reshape(bp, 1, D_MODEL) for i in range(6)]
    mods_s = [mod[bp:bp + ns, i * D_MODEL:(i + 1) * D_MODEL].reshape(1, ns, D_MODEL) for i in range(6)]

    sh1, sc1, g1, sh2, sc2, g2 = mods_p
    (q_t, k_f, v_f, k_b, v_t, qi_t, ki_f, ki_b, wi_t, zg, xbc, dt) = _inproj(
        x_prompt, norm1_g[lyr], sc1, sh1, w_pack, q_norm_g[lyr], k_norm_g[lyr], rows_per_mod=1, queries_on_lanes=True)
    att_p = _dsa_prompt(qi_t, wi_t, ki_b, q_t, k_b, v_t, tz, topk_p)
    ssd_p, ssm_p = _ssd_prompt(xbc, dt, zg, cw_ssd, cb_ssd, dtb_pad, alog_pad, dsk_x, ssd_g, ex, ext)
    y_p, tail_p = _ffn_prompt(att_p, ssd_p, x_prompt, g1, sc2, sh2, g2, norm2_g[lyr], wo_b, wu_b, cw_ffn, cb_ffn, wd_b)

    k_p = k_f.reshape(1, bp, s_, N_KV_HEADS, HEAD_DIM)
    v_p = v_f.reshape(1, bp, s_, N_KV_HEADS, HEAD_DIM)
    kidx_p = ki_f.reshape(1, bp, s_, IDX_DIM)
    ssm_p = ssm_p.reshape(1, bp, SSM_HEADS, SSM_HEAD_DIM, SSM_STATE)
    cssd_p = xbc[:, s_ - (SSM_CONV - 1):, :].reshape(1, bp, SSM_CONV - 1, CONV_DIM)
    cffn_p = tail_p[:, SUBLANES - (FFN_CONV - 1):, :].reshape(1, bp, FFN_CONV - 1, 2 * D_FF)

    sh1, sc1, g1, sh2, sc2, g2 = mods_s
    xs3 = x_sample.reshape(1, ns, D_MODEL)
    (q_b, k_f, v_f, _, _, qi_b, ki_f, _, wi_f, zg, xbc, dt) = _inproj(
        xs3, norm1_g[lyr], sc1, sh1, w_pack, q_norm_g[lyr], k_norm_g[lyr], rows_per_mod=ns, queries_on_lanes=False)
    k_new, v_new, ki_new = k_f[0], v_f[0], ki_f[0]

    scores = _idx_sample(page_table, qi_b[0].reshape(ns, IDX_HEADS, IDX_DIM), wi_f[0].reshape(ns, IDX_HEADS, 1),
                         ki_new.reshape(ns, 1, IDX_DIM), jnp.swapaxes(cache_kidx[lyr], 1, 2))[:, 0, :]
    _, idx = lax.top_k(scores, topk_s)
    in_past = idx < past
    pidx = jnp.minimum(idx, past - 1)
    page_hit = (pidx // page)[:, :, None] == jnp.arange(n_pages, dtype=jnp.int32)
    phys = jnp.sum(jnp.where(page_hit, page_table[:, None, :], 0), axis=-1)
    off = pidx % page
    k_sel = jnp.where(in_past[..., None], cache_k[lyr][phys, off].reshape(ns, topk_s, D_KV), k_new[:, None, :])
    v_sel = jnp.where(in_past[..., None], cache_v[lyr][phys, off].reshape(ns, topk_s, D_KV), v_new[:, None, :])
    bucket_hit = _t5_bucket(past - idx)[:, None, :, None] == jnp.arange(N_BUCKETS, dtype=jnp.int32)
    bias = jnp.sum(jnp.where(bucket_hit, rel_bias.astype(F32).T[None, :, None, :], 0.0), axis=-1)
    att_s = _att_sample(q_b[0].astype(F32).reshape(ns, N_HEADS, HEAD_DIM), k_sel, v_sel, bias).reshape(1, ns, D_ATT)

    hist_ssd = state_conv_ssd[lyr]
    ssd_s, ssm_s = _ssd_sample(xbc[0], [hist_ssd[:, j, :] for j in range(SSM_CONV - 1)], dt[0], zg[0], cw_ssd, cb_ssd,
                               dtb_pad, alog_pad, dsk_x, ssd_g, ex,
                               state_ssm[lyr].reshape(ns, D_SSM, SSM_STATE))
    hist_ffn = state_conv_ffn[lyr]
    y_s, u_s = _ffn_sample(att_s, ssd_s.reshape(1, ns, D_SSM), xs3, g1, sc2, sh2, g2, norm2_g[lyr], wo_b, wu_b,
                           cw_ffn, cb_ffn, wd_b, hist_ffn[:, 0, :], hist_ffn[:, 1, :])

    k_s = k_new.reshape(1, ns, 1, N_KV_HEADS, HEAD_DIM)
    v_s = v_new.reshape(1, ns, 1, N_KV_HEADS, HEAD_DIM)
    kidx_s = ki_new.reshape(1, ns, 1, IDX_DIM)
    ssm_s = ssm_s.reshape(1, ns, SSM_HEADS, SSM_HEAD_DIM, SSM_STATE)
    cssd_s = jnp.concatenate([hist_ssd[:, 1:, :], xbc[0][:, None, :]], axis=1)[None]
    cffn_s = jnp.stack([hist_ffn[:, 1, :], u_s], axis=1)[None]

    return (y_p, y_s.reshape(ns, 1, D_MODEL), k_p, v_p, kidx_p, ssm_p, cssd_p, cffn_p,
            k_s, v_s, kidx_s, ssm_s, cssd_s, cffn_s)
```

```python
import functools
import math

import jax
import jax.numpy as jnp
import numpy as np
from jax import lax
from jax.experimental import pallas as pl
from jax.experimental.pallas import tpu as pltpu

F32 = jnp.float32
BF16 = jnp.bfloat16

D_MODEL = 1024
N_HEADS = 8
N_KV_HEADS = 4
HEAD_DIM = 128
KV_REP = N_HEADS // N_KV_HEADS
IDX_HEADS = 8
IDX_DIM = 64
TOPK_MAX = 256
TOPK_DIV = 4
N_BUCKETS = 32
MAX_DISTANCE = 128
SSM_HEADS = 16
SSM_HEAD_DIM = 64
SSM_GROUPS = 4
SSM_STATE = 128
SSM_CONV = 4
SSD_CHUNK = 128
D_ATT = N_HEADS * HEAD_DIM
D_KV = N_KV_HEADS * HEAD_DIM
D_SSM = SSM_HEADS * SSM_HEAD_DIM
D_MIX = D_ATT + D_SSM
CONV_DIM = D_SSM + 2 * SSM_GROUPS * SSM_STATE
D_FF = 2816
FFN_CONV = 3
EPS = 1e-6
IN_SPLITS = (D_ATT, D_KV, D_KV, IDX_HEADS * IDX_DIM, IDX_DIM, IDX_HEADS, D_SSM, CONV_DIM, SSM_HEADS)

LANES = 128
SUBLANES = 8
VMEM_LIMIT = 52 * 1024 * 1024

OFF_Q = 0
OFF_K = OFF_Q + D_ATT
OFF_V = OFF_K + D_KV
OFF_QI = OFF_V + D_KV
OFF_KI = OFF_QI + IDX_HEADS * IDX_DIM
OFF_WI = OFF_KI + LANES
OFF_ZG = OFF_WI + LANES
OFF_XBC = OFF_ZG + D_SSM
OFF_DT = OFF_XBC + CONV_DIM
N_PACK = OFF_DT + LANES

NEG_BIG = -1e30
LOG2_E = math.log2(math.e)
ATT_CHUNK = 256
FF_CHUNK = 256
ROW_TILE = 256
Q_BLOCK = 256
PAGES_PER_DOT = 4
SCORE_PAD = 2 * ATT_CHUNK
VALUE_STEPS = 14
SNAP_ONLY_TRIPS = 4
SCAN_FOLD = 64
ATT_FOLD = 16


def _cparams(*sem):
    return pltpu.CompilerParams(dimension_semantics=sem, vmem_limit_bytes=VMEM_LIMIT)


def _resident(shape):
    nd = len(shape)
    return pl.BlockSpec(shape, lambda *_: (0,) * nd, pipeline_mode=pl.Buffered(1))


def _silu(x):
    return x * (1.0 / (1.0 + jnp.exp(-x)))


def _softplus(x):
    return jnp.maximum(x, 0.0) + jnp.log1p(jnp.exp(-jnp.abs(x)))


def _rms(x):
    return x * lax.rsqrt(jnp.mean(x * x, axis=-1, keepdims=True) + EPS)


def _split3(x):
    a = x.astype(BF16)
    r = x - a.astype(F32)
    b = r.astype(BF16)
    c = (r - b.astype(F32)).astype(BF16)
    return a, b, c


def _dot_sel_rhs(x, sel):
    return sum(jnp.dot(p, sel, preferred_element_type=F32) for p in _split3(x))


def _dot_sel_lhs(sel, x):
    return sum(jnp.dot(sel, p, preferred_element_type=F32) for p in _split3(x))


def _dot_nt(a, b):
    return lax.dot_general(a, b, (((1,), (1,)), ((), ())), preferred_element_type=F32)


def _dot_tn(a, b):
    return lax.dot_general(a, b, (((0,), (0,)), ((), ())), preferred_element_type=F32)


def _ada_kernel(c_ref, w_ref, b_ref, o_ref):
    s = _silu(c_ref[...]).astype(BF16)
    o_ref[...] = jnp.dot(s, w_ref[...].astype(BF16), preferred_element_type=F32) + b_ref[...]


def _ada(c_all, w_ada, b_ada):
    rows = c_all.shape[0]
    n = w_ada.shape[1]
    return pl.pallas_call(
        _ada_kernel,
        grid=(n // D_MODEL,),
        in_specs=[
            pl.BlockSpec((rows, D_MODEL), lambda j: (0, 0)),
            pl.BlockSpec((D_MODEL, D_MODEL), lambda j: (0, j)),
            pl.BlockSpec((1, D_MODEL), lambda j: (0, j)),
        ],
        out_specs=pl.BlockSpec((rows, D_MODEL), lambda j: (0, j)),
        out_shape=jax.ShapeDtypeStruct((rows, n), F32),
        compiler_params=_cparams("arbitrary"),
        name="ada_mod",
    )(c_all, w_ada, b_ada.reshape(1, n))


def _inproj_kernel(queries_on_lanes, x_ref, g_ref, sc_ref, sh_ref, w_ref, qg_ref, kg_ref,
                   q_o, k_o, v_o, kb_o, vb_o, qi_o, ki_o, kib_o, wi_o, zg_o, xbc_o, dt_o):
    x = x_ref[0]
    h = _rms(x) * g_ref[...]
    hb = (h * (1.0 + sc_ref[0]) + sh_ref[0]).astype(BF16)

    def sec(a, b):
        return jnp.dot(hb, w_ref[:, a:b], preferred_element_type=F32)

    zq = sec(OFF_Q, OFF_K)
    for i in range(N_HEADS):
        qn = _rms(zq[:, i * HEAD_DIM:(i + 1) * HEAD_DIM]) * qg_ref[...]
        if queries_on_lanes:
            q_o[0, i * HEAD_DIM:(i + 1) * HEAD_DIM, :] = qn.T.astype(BF16)
        else:
            q_o[0, :, i * HEAD_DIM:(i + 1) * HEAD_DIM] = qn.astype(BF16)
    zk = sec(OFF_K, OFF_V)
    for i in range(N_KV_HEADS):
        zh = zk[:, i * HEAD_DIM:(i + 1) * HEAD_DIM]
        kn = _rms(zh) * kg_ref[...]
        k_o[0, :, i * HEAD_DIM:(i + 1) * HEAD_DIM] = kn
        kb_o[0, :, i * HEAD_DIM:(i + 1) * HEAD_DIM] = kn.astype(BF16)
    zv = sec(OFF_V, OFF_QI)
    v_o[0] = zv
    zqi = sec(OFF_QI, OFF_KI)
    zwi = sec(OFF_WI, OFF_ZG)
    if queries_on_lanes:
        vb_o[0] = zv.T.astype(BF16)
        qi_o[0] = zqi.T.astype(BF16)
        wi_o[0] = zwi.T[:IDX_HEADS, :]
    else:
        vb_o[0] = zv.astype(BF16)
        qi_o[0] = zqi.astype(BF16)
        wi_o[0] = zwi[:, :IDX_HEADS]
    zki = sec(OFF_KI, OFF_WI)[:, :IDX_DIM]
    ki_o[0] = zki
    kib_o[0] = zki.astype(BF16)
    zg_o[0] = sec(OFF_ZG, OFF_XBC)
    xbc_o[0] = sec(OFF_XBC, OFF_DT)
    dt_o[0] = sec(OFF_DT, N_PACK)


def _inproj(x, norm_g, sc, sh, w_pack, qg, kg, rows_per_mod, queries_on_lanes):
    g_, r_, _ = x.shape
    tm = min(ROW_TILE, r_)
    nt = r_ // tm
    mod_rows = 1 if rows_per_mod == 1 else tm
    mod_map = (lambda b, i: (b, 0, 0)) if rows_per_mod == 1 else (lambda b, i: (b, i, 0))
    outs = [(D_ATT, BF16, True), (D_KV, F32, False), (D_KV, F32, False), (D_KV, BF16, False), (D_KV, BF16, True),
            (IDX_HEADS * IDX_DIM, BF16, True), (IDX_DIM, F32, False), (IDX_DIM, BF16, False), (IDX_HEADS, F32, True),
            (D_SSM, F32, False), (CONV_DIM, F32, False), (LANES, F32, False)]
    out_specs, out_shape = [], []
    for w, dt, tr in outs:
        if tr and queries_on_lanes:
            out_specs.append(pl.BlockSpec((1, w, tm), lambda b, i: (b, 0, i)))
            out_shape.append(jax.ShapeDtypeStruct((g_, w, r_), dt))
        else:
            out_specs.append(pl.BlockSpec((1, tm, w), lambda b, i: (b, i, 0)))
            out_shape.append(jax.ShapeDtypeStruct((g_, r_, w), dt))
    return pl.pallas_call(
        functools.partial(_inproj_kernel, queries_on_lanes),
        grid=(g_, nt),
        in_specs=[
            pl.BlockSpec((1, tm, D_MODEL), lambda b, i: (b, i, 0)),
            _resident((1, D_MODEL)),
            pl.BlockSpec((1, mod_rows, D_MODEL), mod_map),
            pl.BlockSpec((1, mod_rows, D_MODEL), mod_map),
            _resident((D_MODEL, N_PACK)),
            _resident((1, HEAD_DIM)),
            _resident((1, HEAD_DIM)),
        ],
        out_specs=out_specs,
        out_shape=out_shape,
        compiler_params=_cparams("parallel", "parallel"),
        name="in_proj",
    )(x, norm_g.reshape(1, D_MODEL), sc, sh, w_pack, qg.reshape(1, HEAD_DIM), kg.reshape(1, HEAD_DIM))


def _ssd_prompt_kernel(xbc_ref, dt_ref, zg_ref, cw_ref, cb_ref, dtb_ref, alog_ref, dsk_ref, ng_ref,
                       ex_ref, ext_ref, y_o, st_o, xe_scr, st_scr):
    c = pl.program_id(1)
    q_ = SSD_CHUNK
    halo = SUBLANES

    @pl.when(c == 0)
    def _():
        xe_scr[0:halo, :] = jnp.zeros((halo, CONV_DIM), F32)
        st_scr[...] = jnp.zeros_like(st_scr)

    xe_scr[halo:halo + q_, :] = xbc_ref[0]
    conv = cb_ref[...] + sum(
        cw_ref[j:j + 1, :] * xe_scr[halo - (SSM_CONV - 1) + j:halo - (SSM_CONV - 1) + j + q_, :]
        for j in range(SSM_CONV))
    xe_scr[0:halo, :] = xe_scr[q_:q_ + halo, :]
    xc = _silu(conv)
    xs = xc[:, :D_SSM]
    bm = xc[:, D_SSM:D_SSM + SSM_GROUPS * SSM_STATE]
    cm = xc[:, D_SSM + SSM_GROUPS * SSM_STATE:]

    dt = _softplus(dt_ref[0] + dtb_ref[...])
    da = dt * (-jnp.exp(alog_ref[...]))
    row = lax.broadcasted_iota(jnp.int32, (q_, q_), 0)
    col = lax.broadcasted_iota(jnp.int32, (q_, q_), 1)
    causal = row >= col
    tri = jnp.where(causal, 1.0, 0.0).astype(BF16)
    acs = _dot_sel_lhs(tri, da)
    acs_t = acs.T
    acs_last = acs[q_ - 1:q_, :]
    ex = ex_ref[...]
    dt_x = _dot_sel_rhs(dt, ex)
    eacs_x = _dot_sel_rhs(jnp.exp(acs), ex)
    dte_x = _dot_sel_rhs(jnp.exp(acs_last - acs), ex)
    cdec = jnp.exp(jnp.broadcast_to(acs_t[:, q_ - 1:q_], (LANES, SSM_STATE)))
    cdec_rows = _dot_sel_lhs(ext_ref[...], cdec)

    xdt = xs * dt_x
    xdtd = xdt * dte_x
    gw = SSM_HEADS // SSM_GROUPS * SSM_HEAD_DIM
    gated = []
    for g in range(SSM_GROUPS):
        bg = bm[:, g * SSM_STATE:(g + 1) * SSM_STATE].astype(BF16)
        cg = cm[:, g * SSM_STATE:(g + 1) * SSM_STATE].astype(BF16)
        cb = _dot_nt(cg, bg)
        yd = []
        for hl in range(SSM_HEADS // SSM_GROUPS):
            h = g * (SSM_HEADS // SSM_GROUPS) + hl
            seg = acs[:, h:h + 1] - acs_t[h:h + 1, :]
            lmat = jnp.exp(jnp.where(causal, seg, -jnp.inf))
            yd.append(jnp.dot((cb * lmat).astype(BF16),
                              xdt[:, h * SSM_HEAD_DIM:(h + 1) * SSM_HEAD_DIM].astype(BF16),
                              preferred_element_type=F32))
        y_diag = jnp.concatenate(yd, axis=-1)
        st_old = st_scr[g * gw:(g + 1) * gw, :]
        y_off = _dot_nt(cg, st_old.astype(BF16)) * eacs_x[:, g * gw:(g + 1) * gw]
        st_new = _dot_tn(xdtd[:, g * gw:(g + 1) * gw].astype(BF16), bg)
        st_scr[g * gw:(g + 1) * gw, :] = st_old * cdec_rows[g * gw:(g + 1) * gw, :] + st_new
        yg = y_diag + y_off + dsk_ref[:, g * gw:(g + 1) * gw] * xs[:, g * gw:(g + 1) * gw]
        gated.append(yg * _silu(zg_ref[0, :, g * gw:(g + 1) * gw]))

    y_o[0] = (_rms(jnp.concatenate(gated, axis=-1)) * ng_ref[...]).astype(y_o.dtype)

    @pl.when(c == pl.num_programs(1) - 1)
    def _():
        st_o[0] = st_scr[...]


def _ssd_prompt(xbc, dt, zg, conv_w, conv_b, dtb_pad, alog_pad, dsk_x, norm_g, ex, ext):
    b_, s_, _ = xbc.shape
    nc = s_ // SSD_CHUNK
    return pl.pallas_call(
        _ssd_prompt_kernel,
        grid=(b_, nc),
        in_specs=[
            pl.BlockSpec((1, SSD_CHUNK, CONV_DIM), lambda b, c: (b, c, 0)),
            pl.BlockSpec((1, SSD_CHUNK, LANES), lambda b, c: (b, c, 0)),
            pl.BlockSpec((1, SSD_CHUNK, D_SSM), lambda b, c: (b, c, 0)),
            _resident((SSM_CONV, CONV_DIM)),
            _resident((1, CONV_DIM)),
            _resident((1, LANES)),
            _resident((1, LANES)),
            _resident((1, D_SSM)),
            _resident((1, D_SSM)),
            _resident((LANES, D_SSM)),
            _resident((D_SSM, LANES)),
        ],
        out_specs=[
            pl.BlockSpec((1, SSD_CHUNK, D_SSM), lambda b, c: (b, c, 0)),
            pl.BlockSpec((1, D_SSM, SSM_STATE), lambda b, c: (b, 0, 0)),
        ],
        out_shape=[
            jax.ShapeDtypeStruct((b_, s_, D_SSM), F32),
            jax.ShapeDtypeStruct((b_, D_SSM, SSM_STATE), F32),
        ],
        scratch_shapes=[
            pltpu.VMEM((SSD_CHUNK + SUBLANES, CONV_DIM), F32),
            pltpu.VMEM((D_SSM, SSM_STATE), F32),
        ],
        compiler_params=_cparams("parallel", "arbitrary"),
        name="ssd_prompt",
    )(xbc, dt, zg, conv_w, conv_b, dtb_pad, alog_pad, dsk_x, norm_g, ex, ext)


def _f32_key(x):
    b = lax.bitcast_convert_type(x, jnp.int32)
    return b ^ ((b >> 31) & 0x7FFFFFFF)


def _key_f32(k):
    return lax.bitcast_convert_type(k ^ ((k >> 31) & 0x7FFFFFFF), F32)


def _fold(x, rows, op):
    return op(x.reshape(x.shape[0] // rows, rows, x.shape[1]), axis=0)


def _kth_largest(scan, k_row, take_all):
    def count_ge(x):
        return scan(lambda t: jnp.where(t >= x, 1.0, 0.0), jnp.sum, 0.0)

    mx = scan(lambda t: t, jnp.max, -jnp.inf)
    mn = scan(lambda t: jnp.where(t == -jnp.inf, jnp.inf, t), jnp.min, jnp.inf)
    top_tied = count_ge(mx) >= k_row
    state = (mn, mx, jnp.where(top_tied, mx, mn), jnp.where(jnp.logical_or(top_tied, take_all), 1, 0).astype(jnp.int32))

    def value_step(_, s):
        lo, hi, thr, done = s
        mid = 0.5 * lo + 0.5 * hi
        inside = jnp.where(mid > lo, jnp.where(mid < hi, 1, 0), 0)
        cnt = count_ge(mid)
        ge = cnt >= k_row
        finished = jnp.where(cnt == k_row, inside, 0) * (1 - done)
        lo = jnp.where(inside == 1, jnp.where(ge, mid, lo), lo)
        hi = jnp.where(inside == 1, jnp.where(ge, hi, mid), hi)
        return lo, hi, jnp.where(finished == 1, mid, thr), done + finished

    def snap_step(s):
        lo, hi, thr, done = s
        cand = scan(lambda t: jnp.where(t < hi, t, -jnp.inf), jnp.max, -jnp.inf)
        ge = count_ge(cand) >= k_row
        finished = jnp.where(ge, 1, 0) * (1 - done)
        return lo, jnp.where(ge, hi, cand), jnp.where(finished == 1, cand, thr), done + finished

    def key_step(s):
        lo, hi, thr, done = s
        klo, khi = _f32_key(lo), _f32_key(hi)
        mid_k = (klo >> 1) + (khi >> 1) + (klo & khi & 1)
        mid = _key_f32(mid_k)
        cnt = count_ge(mid)
        ge = cnt >= k_row
        collapsed = mid_k == klo
        finished = jnp.where(collapsed, 1, jnp.where(cnt == k_row, 1, 0)) * (1 - done)
        thr = jnp.where(finished == 1, jnp.where(collapsed, lo, mid), thr)
        return jnp.where(ge, mid, lo), jnp.where(ge, hi, mid), thr, done + finished

    state = lax.fori_loop(0, VALUE_STEPS, value_step, state)

    def refine(s):
        it = s[4]
        st = snap_step(s[:4])
        st = lax.cond(it >= SNAP_ONLY_TRIPS, key_step, lambda x: x, st)
        return (*st, it + 1)

    return lax.while_loop(lambda s: jnp.logical_and(jnp.min(s[3]) == 0, s[4] < SNAP_ONLY_TRIPS + 40), refine,
                          (*state, jnp.int32(0)))[2]


def _dsa_prompt_kernel(topk, qi_ref, wi_ref, ki_ref, q_ref, k_ref, v_ref, tz_ref, o_ref,
                       sc_scr, s_scr, acc_scr):
    qb = pl.program_id(1)
    tq = q_ref.shape[2]
    ch = ATT_CHUNK
    n_ch = qb + 1

    def chunk_at(j):
        return pl.ds(pl.multiple_of(j * ch, ch), ch)

    qi_all = jnp.concatenate([qi_ref[0, h * IDX_DIM:(h + 1) * IDX_DIM, :] for h in range(IDX_HEADS)], axis=1)
    w_rows = [wi_ref[0, h:h + 1, :] for h in range(IDX_HEADS)]

    def score_chunk(j, diagonal):
        d = jnp.dot(ki_ref[0, chunk_at(j), :], qi_all, preferred_element_type=F32)
        acc = w_rows[0] * jnp.maximum(d[:, 0:tq], 0.0)
        for h in range(1, IDX_HEADS):
            acc = acc + w_rows[h] * jnp.maximum(d[:, h * tq:(h + 1) * tq], 0.0)
        if diagonal:
            causal = (lax.broadcasted_iota(jnp.int32, (ch, tq), 0) <= lax.broadcasted_iota(jnp.int32, (ch, tq), 1))
            acc = jnp.where(causal, acc, -jnp.inf)
        sc_scr[chunk_at(j), :] = acc

    def score_body(j, c):
        score_chunk(j, False)
        return c

    lax.fori_loop(0, qb, score_body, 0)
    score_chunk(qb, True)

    def chunks_reduce(fn, op, init):
        def body(j, acc):
            rows = pl.ds(pl.multiple_of(j * (2 * ch), 2 * ch), 2 * ch)
            return op(jnp.stack([acc, _fold(fn(sc_scr[rows, :]), SCAN_FOLD, op)]), axis=0)
        return op(lax.fori_loop(0, (n_ch + 1) // 2, body, jnp.full((SCAN_FOLD, tq), init, F32)), axis=0, keepdims=True)

    @pl.when(n_ch % 2 == 1)
    def _():
        sc_scr[chunk_at(n_ch), :] = jnp.full((ch, tq), -jnp.inf, F32)

    def count(cmp, thr):
        return chunks_reduce(lambda t: jnp.where(cmp(t, thr), 1.0, 0.0), jnp.sum, 0.0)

    count_ge = functools.partial(count, lambda t, x: t >= x)

    n_valid = qb * tq + lax.broadcasted_iota(jnp.int32, (1, tq), 1) + 1
    k_row = jnp.minimum(n_valid, topk).astype(F32)
    thr = _kth_largest(chunks_reduce, k_row, n_valid <= topk)

    surplus = jnp.max(count_ge(thr) - k_row)

    @pl.when(surplus <= 0.0)
    def _():
        def select_chunk(j, c):
            sc_scr[chunk_at(j), :] = jnp.where(sc_scr[chunk_at(j), :] >= thr, 0.0, NEG_BIG)
            return c
        lax.fori_loop(0, n_ch, select_chunk, 0)

    @pl.when(surplus > 0.0)
    def _():
        need = k_row - count(lambda t, x: t > x, thr)
        lower = jnp.where(lax.broadcasted_iota(jnp.int32, (ch, ch), 1) <= lax.broadcasted_iota(jnp.int32, (ch, ch), 0),
                          1.0, 0.0).astype(BF16)
        ones_r = jnp.ones((SUBLANES, ch), BF16)

        def select_chunk(j, run):
            t = sc_scr[chunk_at(j), :]
            eq = t == thr
            eqb = jnp.where(eq, 1.0, 0.0).astype(BF16)
            rank = jnp.dot(lower, eqb, preferred_element_type=F32) + run
            tie_bias = jnp.where(rank <= need, 0.0, NEG_BIG)
            sc_scr[chunk_at(j), :] = jnp.where(t > thr, 0.0, jnp.where(eq, tie_bias, NEG_BIG))
            return run + jnp.dot(ones_r, eqb, preferred_element_type=F32)[0:1, :]

        lax.fori_loop(0, n_ch, select_chunk, jnp.zeros((1, tq), F32))

    scale = HEAD_DIM ** -0.5 * LOG2_E
    cols = KV_REP * tq
    n_far = jnp.maximum(qb - 1, 0)
    for gp in range(N_KV_HEADS // 2):
        groups = (2 * gp, 2 * gp + 1)
        q_cols = [jnp.concatenate([q_ref[0, (g * KV_REP + r) * HEAD_DIM:(g * KV_REP + r + 1) * HEAD_DIM, :]
                                   for r in range(KV_REP)], axis=1) for g in groups]

        def logits_chunks(jj, mx, near, width):
            mx = list(mx)
            for j in [jj * width + i for i in range(width)]:
                selb = sc_scr[chunk_at(j), :]
                for gi, g in enumerate(groups):
                    s = jnp.dot(k_ref[0, chunk_at(j), g * HEAD_DIM:(g + 1) * HEAD_DIM], q_cols[gi],
                                preferred_element_type=F32) * scale
                    if near:
                        bias = jnp.concatenate([tz_ref[qb - j, g * KV_REP + r] + selb for r in range(KV_REP)], axis=1)
                    else:
                        bias = jnp.concatenate([selb] * KV_REP, axis=1)
                    s = s + bias
                    s_scr[gi, chunk_at(j), :] = s
                    mx[gi] = jnp.maximum(mx[gi], _fold(s, ATT_FOLD, jnp.max))
            return tuple(mx)

        mx = (jnp.full((ATT_FOLD, cols), NEG_BIG, F32),) * 2
        mx = lax.fori_loop(0, n_far // 2, functools.partial(logits_chunks, near=False, width=2), mx)
        mx = lax.fori_loop(n_far // 2 * 2, n_far, functools.partial(logits_chunks, near=False, width=1), mx)
        mx = lax.fori_loop(n_far, n_ch, functools.partial(logits_chunks, near=True, width=1), mx)
        m_row = [jnp.max(m, axis=0, keepdims=True) for m in mx]
        acc_scr[...] = jnp.zeros_like(acc_scr)

        def weights_chunks(jj, l8, width):
            l8 = list(l8)
            for j in [jj * width + i for i in range(width)]:
                for gi, g in enumerate(groups):
                    p = jnp.exp2(s_scr[gi, chunk_at(j), :] - m_row[gi])
                    l8[gi] = l8[gi] + _fold(p, ATT_FOLD, jnp.sum)
                    acc_scr[gi] += jnp.dot(v_ref[0, g * HEAD_DIM:(g + 1) * HEAD_DIM, chunk_at(j)], p.astype(BF16),
                                           preferred_element_type=F32)
            return tuple(l8)

        l8 = (jnp.zeros((ATT_FOLD, cols), F32),) * 2
        l8 = lax.fori_loop(0, n_ch // 2, functools.partial(weights_chunks, width=2), l8)
        l8 = lax.fori_loop(n_ch // 2 * 2, n_ch, functools.partial(weights_chunks, width=1), l8)
        for gi, g in enumerate(groups):
            out = acc_scr[gi] * (1.0 / jnp.sum(l8[gi], axis=0, keepdims=True))
            for r in range(KV_REP):
                h = g * KV_REP + r
                o_ref[0, :, h * HEAD_DIM:(h + 1) * HEAD_DIM] = out[:, r * tq:(r + 1) * tq].T.astype(o_ref.dtype)


def _dsa_prompt(qi_t, wi_t, ki_b, q_t, k_b, v_t, tz, topk):
    b_, s_, _ = k_b.shape
    tq = min(Q_BLOCK, s_)
    assert tq == ATT_CHUNK and s_ % (2 * tq) == 0
    once = pl.Buffered(1)
    q_tile = lambda rows: pl.BlockSpec((1, rows, tq), lambda b, i: (b, 0, i))
    return pl.pallas_call(
        functools.partial(_dsa_prompt_kernel, topk),
        grid=(b_, s_ // tq),
        in_specs=[
            q_tile(IDX_HEADS * IDX_DIM),
            q_tile(IDX_HEADS),
            pl.BlockSpec((1, s_, IDX_DIM), lambda b, i: (b, 0, 0), pipeline_mode=once),
            q_tile(D_ATT),
            pl.BlockSpec((1, s_, D_KV), lambda b, i: (b, 0, 0), pipeline_mode=once),
            pl.BlockSpec((1, D_KV, s_), lambda b, i: (b, 0, 0), pipeline_mode=once),
            _resident(tz.shape),
        ],
        out_specs=pl.BlockSpec((1, tq, D_ATT), lambda b, i: (b, i, 0)),
        out_shape=jax.ShapeDtypeStruct((b_, s_, D_ATT), BF16),
        scratch_shapes=[
            pltpu.VMEM((s_, tq), F32),
            pltpu.VMEM((2, s_, KV_REP * tq), F32),
            pltpu.VMEM((2, HEAD_DIM, KV_REP * tq), F32),
        ],
        compiler_params=_cparams("parallel", "arbitrary"),
        name="dsa_prompt",
    )(qi_t, wi_t, ki_b, q_t, k_b, v_t, tz)


def _mix_and_norm(att_ref, ssd_ref, x_ref, g1_ref, sc2_ref, sh2_ref, n2_ref, wo_ref):
    o = (jnp.dot(att_ref[0].astype(BF16), wo_ref[:D_ATT, :], preferred_element_type=F32)
         + jnp.dot(ssd_ref[0].astype(BF16), wo_ref[D_ATT:, :], preferred_element_type=F32))
    x1 = x_ref[0] + g1_ref[0] * o
    h2 = ((_rms(x1) * n2_ref[...]) * (1.0 + sc2_ref[0]) + sh2_ref[0]).astype(BF16)
    return x1, h2


def _ffn_prompt_kernel(att_ref, ssd_ref, x_ref, g1_ref, sc2_ref, sh2_ref, g2_ref, n2_ref, wo_ref, wu_ref,
                       cw_ref, cb_ref, wd_ref, y_o, tail_o, a_scr, ug_scr, uv_scr, carry_scr):
    i = pl.program_id(1)
    tm = x_ref.shape[1]
    halo = SUBLANES

    @pl.when(i == 0)
    def _():
        carry_scr[...] = jnp.zeros_like(carry_scr)

    x1, h2 = _mix_and_norm(att_ref, ssd_ref, x_ref, g1_ref, sc2_ref, sh2_ref, n2_ref, wo_ref)

    def conv_cols(scr, c0):
        u = jnp.dot(h2, wu_ref[:, c0:c0 + FF_CHUNK], preferred_element_type=F32)
        scr[0:halo, :] = carry_scr[:, c0:c0 + FF_CHUNK]
        scr[halo:halo + tm, :] = u
        carry_scr[:, c0:c0 + FF_CHUNK] = u[tm - halo:, :]
        return cb_ref[:, c0:c0 + FF_CHUNK] + sum(
            cw_ref[j:j + 1, c0:c0 + FF_CHUNK] * scr[halo - (FFN_CONV - 1) + j:halo - (FFN_CONV - 1) + j + tm, :]
            for j in range(FFN_CONV))

    for jc in range(D_FF // FF_CHUNK):
        gate = conv_cols(ug_scr.at[jc % 2], jc * FF_CHUNK)
        val = conv_cols(uv_scr.at[jc % 2], D_FF + jc * FF_CHUNK)
        a_scr[:, jc * FF_CHUNK:(jc + 1) * FF_CHUNK] = (_silu(gate) * val).astype(BF16)

    y_o[0] = x1 + g2_ref[0] * jnp.dot(a_scr[...], wd_ref[...], preferred_element_type=F32)
    tail_o[0] = carry_scr[...]


def _ffn_prompt(att, ssd, x, g1, sc2, sh2, g2, norm2_g, wo_b, wu_b, cw, cb, wd_b):
    b_, s_, _ = x.shape
    tm = min(ROW_TILE, s_)
    nt = s_ // tm
    row = lambda w: pl.BlockSpec((1, tm, w), lambda b, i: (b, i, 0))
    mod = pl.BlockSpec((1, 1, D_MODEL), lambda b, i: (b, 0, 0))
    return pl.pallas_call(
        _ffn_prompt_kernel,
        grid=(b_, nt),
        in_specs=[row(D_ATT), row(D_SSM), row(D_MODEL), mod, mod, mod, mod, _resident((1, D_MODEL)),
                  _resident((D_MIX, D_MODEL)), _resident((D_MODEL, 2 * D_FF)), _resident((FFN_CONV, 2 * D_FF)),
                  _resident((1, 2 * D_FF)), _resident((D_FF, D_MODEL))],
        out_specs=[row(D_MODEL), pl.BlockSpec((1, SUBLANES, 2 * D_FF), lambda b, i: (b, 0, 0))],
        out_shape=[jax.ShapeDtypeStruct((b_, s_, D_MODEL), F32),
                   jax.ShapeDtypeStruct((b_, SUBLANES, 2 * D_FF), F32)],
        scratch_shapes=[pltpu.VMEM((tm, D_FF), BF16), pltpu.VMEM((2, tm + SUBLANES, FF_CHUNK), F32),
                        pltpu.VMEM((2, tm + SUBLANES, FF_CHUNK), F32), pltpu.VMEM((SUBLANES, 2 * D_FF), F32)],
        compiler_params=_cparams("parallel", "arbitrary"),
        name="ffn_prompt",
    )(att, ssd, x, g1, sc2, sh2, g2, norm2_g.reshape(1, D_MODEL), wo_b, wu_b, cw, cb, wd_b)


def _ffn_sample_kernel(att_ref, ssd_ref, x_ref, g1_ref, sc2_ref, sh2_ref, g2_ref, n2_ref, wo_ref, wu_ref,
                       cw_ref, cb_ref, wd_ref, h0_ref, h1_ref, y_o, u_o, acc_scr):
    x1, h2 = _mix_and_norm(att_ref, ssd_ref, x_ref, g1_ref, sc2_ref, sh2_ref, n2_ref, wo_ref)
    acc_scr[...] = jnp.zeros_like(acc_scr)

    def conv_cols(c0):
        u = jnp.dot(h2, wu_ref[:, c0:c0 + FF_CHUNK], preferred_element_type=F32)
        u_o[:, c0:c0 + FF_CHUNK] = u
        sl = slice(c0, c0 + FF_CHUNK)
        return (cb_ref[:, sl] + cw_ref[0:1, sl] * h0_ref[:, sl] + cw_ref[1:2, sl] * h1_ref[:, sl]
                + cw_ref[2:3, sl] * u)

    for jc in range(D_FF // FF_CHUNK):
        gate = conv_cols(jc * FF_CHUNK)
        val = conv_cols(D_FF + jc * FF_CHUNK)
        a = (_silu(gate) * val).astype(BF16)
        acc_scr[...] += jnp.dot(a, wd_ref[jc * FF_CHUNK:(jc + 1) * FF_CHUNK, :], preferred_element_type=F32)

    y_o[0] = x1 + g2_ref[0] * acc_scr[...]


def _ffn_sample(att, ssd, x, g1, sc2, sh2, g2, norm2_g, wo_b, wu_b, cw, cb, wd_b, hist0, hist1):
    n = x.shape[1]
    full = lambda *shape: pl.BlockSpec(shape, lambda: (0,) * len(shape))
    r3 = lambda w: full(1, n, w)
    return pl.pallas_call(
        _ffn_sample_kernel,
        in_specs=[r3(D_ATT), r3(D_SSM), r3(D_MODEL), r3(D_MODEL), r3(D_MODEL), r3(D_MODEL), r3(D_MODEL),
                  full(1, D_MODEL), full(D_MIX, D_MODEL), full(D_MODEL, 2 * D_FF), full(FFN_CONV, 2 * D_FF),
                  full(1, 2 * D_FF), full(D_FF, D_MODEL), full(n, 2 * D_FF), full(n, 2 * D_FF)],
        out_specs=[r3(D_MODEL), full(n, 2 * D_FF)],
        out_shape=[jax.ShapeDtypeStruct((1, n, D_MODEL), F32), jax.ShapeDtypeStruct((n, 2 * D_FF), F32)],
        scratch_shapes=[pltpu.VMEM((n, D_MODEL), F32)],
        compiler_params=pltpu.CompilerParams(vmem_limit_bytes=VMEM_LIMIT),
        name="ffn_sample",
    )(att, ssd, x, g1, sc2, sh2, g2, norm2_g.reshape(1, D_MODEL), wo_b, wu_b, cw, cb, wd_b, hist0, hist1)


def _ssd_sample_kernel(xbc_ref, h0_ref, h1_ref, h2_ref, dt_ref, zg_ref, cw_ref, cb_ref, dtb_ref, alog_ref,
                       dsk_ref, ng_ref, ex_ref, st_ref, y_o, st_o):
    nb = xbc_ref.shape[0]
    conv = (cb_ref[...] + cw_ref[0:1, :] * h0_ref[...] + cw_ref[1:2, :] * h1_ref[...]
            + cw_ref[2:3, :] * h2_ref[...] + cw_ref[3:4, :] * xbc_ref[...])
    xc = _silu(conv)
    xs = xc[:, :D_SSM]
    bm = xc[:, D_SSM:D_SSM + SSM_GROUPS * SSM_STATE]
    cm = xc[:, D_SSM + SSM_GROUPS * SSM_STATE:]
    dt = _softplus(dt_ref[...] + dtb_ref[...])
    dec = jnp.exp(dt * (-jnp.exp(alog_ref[...])))
    ex = ex_ref[...]
    xdt = xs * _dot_sel_rhs(dt, ex)
    dec_x = _dot_sel_rhs(dec, ex)
    gw = SSM_HEADS // SSM_GROUPS * SSM_HEAD_DIM
    ones_n = jnp.ones((SUBLANES, SSM_STATE), BF16)
    row0 = lax.broadcasted_iota(jnp.int32, (SUBLANES, 1), 0) == 0

    def only_row0(v):
        return jnp.where(row0, jnp.broadcast_to(v, (SUBLANES, v.shape[1])), 0.0)

    ys = []
    for i in range(nb):
        yrow = []
        for g in range(SSM_GROUPS):
            hs = st_ref[i, g * gw:(g + 1) * gw, :]
            dec_rows = only_row0(dec_x[i:i + 1, g * gw:(g + 1) * gw])
            xdt_rows = only_row0(xdt[i:i + 1, g * gw:(g + 1) * gw])
            b_rows = only_row0(bm[i:i + 1, g * SSM_STATE:(g + 1) * SSM_STATE])
            c_rows = only_row0(cm[i:i + 1, g * SSM_STATE:(g + 1) * SSM_STATE])
            dec_b = sum(_dot_tn(p, ones_n) for p in _split3(dec_rows))
            upd = sum(_dot_tn(p, q) for p in _split3(xdt_rows)[:2] for q in _split3(b_rows)[:2])
            hn = hs * dec_b + upd
            st_o[i, g * gw:(g + 1) * gw, :] = hn
            yg = sum(_dot_nt(p, q) for p in _split3(c_rows)[:2] for q in _split3(hn)[:2])
            yrow.append(yg[0:1, :])
        ys.append(jnp.concatenate(yrow, axis=-1))
    y = jnp.concatenate(ys, axis=0) + dsk_ref[...] * xs
    y = y * _silu(zg_ref[...])
    y_o[...] = _rms(y) * ng_ref[...]


def _ssd_sample(xbc, hist, dt, zg, conv_w, conv_b, dtb_pad, alog_pad, dsk_x, norm_g, ex, state):
    n = xbc.shape[0]
    nb = SUBLANES
    rows = lambda w: pl.BlockSpec((nb, w), lambda i: (i, 0))
    st = pl.BlockSpec((nb, D_SSM, SSM_STATE), lambda i: (i, 0, 0))
    return pl.pallas_call(
        _ssd_sample_kernel,
        grid=(n // nb,),
        in_specs=[rows(CONV_DIM), rows(CONV_DIM), rows(CONV_DIM), rows(CONV_DIM), rows(LANES), rows(D_SSM),
                  _resident((SSM_CONV, CONV_DIM)), _resident((1, CONV_DIM)), _resident((1, LANES)),
                  _resident((1, LANES)), _resident((1, D_SSM)), _resident((1, D_SSM)), _resident((LANES, D_SSM)), st],
        out_specs=[rows(D_SSM), st],
        out_shape=[jax.ShapeDtypeStruct((n, D_SSM), F32), jax.ShapeDtypeStruct((n, D_SSM, SSM_STATE), F32)],
        compiler_params=_cparams("parallel"),
        name="ssd_sample",
    )(xbc, hist[0], hist[1], hist[2], dt, zg, conv_w, conv_b, dtb_pad, alog_pad, dsk_x, norm_g, ex, state)


def _idx_sample_kernel(pt_ref, qi_ref, wi_ref, kin_ref, pool_ref, o_ref, kbuf, sem):
    b = pl.program_id(0)
    nb = pl.num_programs(0)
    n_pages = kbuf.shape[1]
    page = kbuf.shape[3]

    def page_copy(seq, slot, j):
        return pltpu.make_async_copy(pool_ref.at[pt_ref[seq, j]], kbuf.at[slot, j], sem.at[slot])

    def fetch(seq, slot):
        def body(j, c):
            page_copy(seq, slot, j).start()
            return c
        lax.fori_loop(0, n_pages, body, 0, unroll=8)

    @pl.when(b == 0)
    def _():
        fetch(0, 0)

    @pl.when(b + 1 < nb)
    def _():
        fetch(b + 1, (b + 1) % 2)

    slot = b % 2

    def wait_body(j, c):
        page_copy(b, slot, j).wait()
        return c
    lax.fori_loop(0, n_pages, wait_body, 0)

    qi = qi_ref[0]
    w = wi_ref[0]

    span = PAGES_PER_DOT * page

    def page_scores(j, c):
        keys = jnp.concatenate([kbuf[slot, PAGES_PER_DOT * j + i] for i in range(PAGES_PER_DOT)], axis=1)
        d = jnp.dot(qi, keys.astype(BF16), preferred_element_type=F32)
        o_ref[0, :, pl.ds(pl.multiple_of(j * span, span), span)] = jnp.sum(w * jnp.maximum(d, 0.0), axis=0, keepdims=True)
        return c

    lax.fori_loop(0, n_pages // PAGES_PER_DOT, page_scores, 0, unroll=2)
    knew = kin_ref[0].astype(BF16).astype(F32)
    dn = jnp.sum(qi.astype(F32) * knew, axis=-1, keepdims=True)
    s_new = jnp.sum(w * jnp.maximum(dn, 0.0), axis=0, keepdims=True)
    lane = lax.broadcasted_iota(jnp.int32, (1, SCORE_PAD), 1)
    o_ref[0, :, n_pages * page:] = jnp.where(lane == 0, s_new, -jnp.inf)


def _idx_sample(page_table, qi3, wi3, ki_new3, pool_t):
    n, n_pages = page_table.shape
    page = pool_t.shape[2]
    assert n_pages % (8 * PAGES_PER_DOT) == 0
    past = n_pages * page
    grid_spec = pltpu.PrefetchScalarGridSpec(
        num_scalar_prefetch=1,
        grid=(n,),
        in_specs=[
            pl.BlockSpec((1, IDX_HEADS, IDX_DIM), lambda b, pt: (b, 0, 0)),
            pl.BlockSpec((1, IDX_HEADS, 1), lambda b, pt: (b, 0, 0)),
            pl.BlockSpec((1, 1, IDX_DIM), lambda b, pt: (b, 0, 0)),
            pl.BlockSpec(memory_space=pl.ANY),
        ],
        out_specs=pl.BlockSpec((1, 1, past + SCORE_PAD), lambda b, pt: (b, 0, 0)),
        scratch_shapes=[pltpu.VMEM((2, n_pages, IDX_DIM, page), F32), pltpu.SemaphoreType.DMA((2,))],
    )
    return pl.pallas_call(
        _idx_sample_kernel,
        grid_spec=grid_spec,
        out_shape=jax.ShapeDtypeStruct((n, 1, past + SCORE_PAD), F32),
        compiler_params=_cparams("arbitrary"),
        name="idx_sample",
    )(page_table, qi3, wi3, ki_new3, pool_t)


def _att_sample_kernel(q_ref, k_ref, v_ref, knew_ref, vnew_ref, isnew_ref, bias_ref, o_ref):
    nb = q_ref.shape[0]
    scale = HEAD_DIM ** -0.5
    head_group = lax.broadcasted_iota(jnp.int32, (N_HEADS, 1), 0) // KV_REP
    for i in range(nb):
        q = q_ref[i].astype(BF16)
        bias = bias_ref[i]
        is_new = isnew_ref[i] > 0.5
        out = jnp.zeros((N_HEADS, HEAD_DIM), F32)
        for g in range(N_KV_HEADS):
            kg = jnp.where(is_new, knew_ref[i, g:g + 1, :], k_ref[i, :, g, :]).astype(BF16)
            vg = jnp.where(is_new, vnew_ref[i, g:g + 1, :], v_ref[i, :, g, :]).astype(BF16)
            s = _dot_nt(q, kg) * scale + bias
            m = jnp.max(s, axis=-1, keepdims=True)
            p = jnp.exp(s - m)
            p = p * (1.0 / jnp.sum(p, axis=-1, keepdims=True))
            og = jnp.dot(p.astype(BF16), vg, preferred_element_type=F32)
            out = jnp.where(head_group == g, og, out)
        o_ref[i] = out


def _att_sample(q3, k_sel, v_sel, k_new, v_new, is_new, bias):
    n, kk = k_sel.shape[:2]
    nb = SUBLANES // 2
    rows4 = pl.BlockSpec((nb, kk, N_KV_HEADS, HEAD_DIM), lambda i: (i, 0, 0, 0))
    new3 = pl.BlockSpec((nb, N_KV_HEADS, HEAD_DIM), lambda i: (i, 0, 0))
    return pl.pallas_call(
        _att_sample_kernel,
        grid=(n // nb,),
        in_specs=[
            pl.BlockSpec((nb, N_HEADS, HEAD_DIM), lambda i: (i, 0, 0)),
            rows4, rows4, new3, new3,
            pl.BlockSpec((nb, kk, 1), lambda i: (i, 0, 0)),
            pl.BlockSpec((nb, N_HEADS, kk), lambda i: (i, 0, 0)),
        ],
        out_specs=pl.BlockSpec((nb, N_HEADS, HEAD_DIM), lambda i: (i, 0, 0)),
        out_shape=jax.ShapeDtypeStruct((n, N_HEADS, HEAD_DIM), F32),
        compiler_params=_cparams("parallel"),
        name="att_sample",
    )(q3, k_sel, v_sel, k_new, v_new, is_new, bias)


def _sel_sample_kernel(k_keep, n_valid, sc_ref, o_ref):
    rows, n = sc_ref.shape
    ch = ATT_CHUNK
    span = 2 * ch

    def scan(fn, op, init):
        def body(j, acc):
            r = pl.ds(pl.multiple_of(j * span, span), span)
            return op(jnp.stack([acc, _fold(fn(sc_ref[r, :]), SCAN_FOLD, op)]), axis=0)
        return op(lax.fori_loop(0, rows // span, body, jnp.full((SCAN_FOLD, n), init, F32)), axis=0, keepdims=True)

    k_row = jnp.full((1, n), float(k_keep), F32)
    thr = _kth_largest(scan, k_row, jnp.full((1, n), n_valid <= k_keep))
    need = k_row - scan(lambda t: jnp.where(t > thr, 1.0, 0.0), jnp.sum, 0.0)
    lower = jnp.where(lax.broadcasted_iota(jnp.int32, (ch, ch), 1) <= lax.broadcasted_iota(jnp.int32, (ch, ch), 0),
                      1.0, 0.0).astype(BF16)
    ones_r = jnp.ones((SUBLANES, ch), BF16)
    key_local = lax.broadcasted_iota(jnp.int32, (ch, n), 0).astype(F32)
    slot_iota = lax.broadcasted_iota(jnp.int32, (k_keep, n), 0).astype(F32)
    none = float(rows)

    def chunk_body(c, carry):
        run, slot, out = carry
        t = sc_ref[pl.ds(pl.multiple_of(c * ch, ch), ch), :]
        eq = t == thr
        eqb = jnp.where(eq, 1.0, 0.0).astype(BF16)
        rank = jnp.dot(lower, eqb, preferred_element_type=F32) + run
        cand = jnp.where(t > thr, key_local, jnp.where(eq, jnp.where(rank <= need, key_local, none), none))
        left = _fold(jnp.where(cand < none, 1.0, 0.0), SUBLANES, jnp.sum).sum(axis=0, keepdims=True)
        base = jnp.asarray(c * ch, F32)

        def extract(s):
            cand, out, slot, left = s
            cur = jnp.min(cand, axis=0, keepdims=True)
            has = cur < none
            out = jnp.where(slot_iota == slot, jnp.where(has, cur + base, out), out)
            inc = jnp.where(has, 1.0, 0.0)
            return jnp.where(cand == cur, none, cand), out, slot + inc, left - inc

        _, out, slot, _ = lax.while_loop(lambda s: jnp.max(s[3]) > 0.0, extract, (cand, out, slot, left))
        return run + jnp.dot(ones_r, eqb, preferred_element_type=F32)[0:1, :], slot, out

    zeros_row = jnp.zeros((1, n), F32)
    _, _, out = lax.fori_loop(0, rows // ch, chunk_body, (zeros_row, zeros_row, jnp.zeros((k_keep, n), F32)))
    o_ref[...] = out.astype(jnp.int32)


def _sel_sample(scores_t, k_keep, n_valid):
    rows, n = scores_t.shape
    assert rows % (2 * ATT_CHUNK) == 0
    return pl.pallas_call(
        functools.partial(_sel_sample_kernel, k_keep, n_valid),
        in_specs=[pl.BlockSpec((rows, n), lambda: (0, 0))],
        out_specs=pl.BlockSpec((k_keep, n), lambda: (0, 0)),
        out_shape=jax.ShapeDtypeStruct((k_keep, n), jnp.int32),
        compiler_params=pltpu.CompilerParams(vmem_limit_bytes=VMEM_LIMIT),
        name="sel_sample",
    )(scores_t)


def _pack_w_in(w_in):
    offs = np.cumsum((0,) + IN_SPLITS)
    parts = [w_in[:, offs[i]:offs[i + 1]] for i in range(len(IN_SPLITS))]
    pad = lambda a: jnp.pad(a, ((0, 0), (0, LANES - a.shape[1])))
    q, k, v, qi, ki, wi, zg, xbc, dt = parts
    return jnp.concatenate([q, k, v, qi, pad(ki), pad(wi), zg, xbc, pad(dt)], axis=1).astype(BF16)


def _t5_bucket(dist):
    dist = jnp.maximum(dist, 0)
    max_exact = N_BUCKETS // 2
    d = jnp.maximum(dist, 1).astype(F32)
    large = max_exact + (jnp.log(d / max_exact) / math.log(MAX_DISTANCE / max_exact) * (N_BUCKETS - max_exact)).astype(jnp.int32)
    large = jnp.minimum(large, N_BUCKETS - 1)
    return jnp.where(dist < max_exact, dist, large)


def _rel_bias_tables(rel_bias, tq):
    assert ATT_CHUNK == tq and tq >= MAX_DISTANCE
    n = 2 * tq
    idx = jnp.arange(n, dtype=jnp.int32)
    query_minus_key = jnp.where(idx < tq, idx, idx - n)
    far = rel_bias[_t5_bucket(jnp.int32(MAX_DISTANCE))]
    tabs = []
    for d0 in (0, tq):
        f = (rel_bias[_t5_bucket(d0 + query_minus_key)] - far[None, :]) * LOG2_E
        flat = jnp.tile(f.T, (1, ATT_CHUNK))
        tabs.append(flat[:, :ATT_CHUNK * (n - 1)].reshape(-1, ATT_CHUNK, n - 1)[:, :, :tq])
    return jnp.stack(tabs)


def _head_expand():
    h = jnp.arange(LANES, dtype=jnp.int32)[:, None]
    c = jnp.arange(D_SSM, dtype=jnp.int32)[None, :]
    ex = (c // SSM_HEAD_DIM == h).astype(BF16)
    return ex, ex.T


def _pad_lanes(v):
    return jnp.pad(v.astype(F32), (0, LANES - v.shape[0])).reshape(1, LANES)


def kernel(x_prompt, x_sample, cache_k, cache_v, cache_kidx, state_ssm, state_conv_ssd, state_conv_ffn, page_table,
           c_prompt, c_sample, rel_bias, norm1_g, w_ada, b_ada, w_in, q_norm_g, k_norm_g, conv_ssd_w, conv_ssd_b,
           dt_bias, a_log, d_skip, ssd_norm_g, w_out, norm2_g, w_up, conv_ffn_w, conv_ffn_b, w_down):
    depth = w_in.shape[0]
    assert depth == 1
    bp, s_, _ = x_prompt.shape
    ns = x_sample.shape[0]
    assert x_sample.shape[1] == 1
    lyr = 0
    n_pages = page_table.shape[1]
    page = cache_k.shape[2]
    past = n_pages * page
    topk_p = min(TOPK_MAX, s_ // TOPK_DIV)
    topk_s = min(TOPK_MAX, (past + 1) // TOPK_DIV)

    w_pack = _pack_w_in(w_in[lyr])
    wo_b = w_out[lyr].astype(BF16)
    wu_b = w_up[lyr].astype(BF16)
    wd_b = w_down[lyr].astype(BF16)
    ex, ext = _head_expand()
    dtb_pad = _pad_lanes(dt_bias[lyr])
    alog_pad = _pad_lanes(a_log[lyr])
    dsk_x = jnp.repeat(d_skip[lyr].astype(F32), SSM_HEAD_DIM).reshape(1, D_SSM)
    ssd_g = ssd_norm_g[lyr].reshape(1, D_SSM)
    cw_ssd, cb_ssd = conv_ssd_w[lyr], conv_ssd_b[lyr].reshape(1, CONV_DIM)
    cw_ffn, cb_ffn = conv_ffn_w[lyr], conv_ffn_b[lyr].reshape(1, 2 * D_FF)
    tz = _rel_bias_tables(rel_bias.astype(F32), min(Q_BLOCK, s_))

    n_c = bp + ns
    c_all = jnp.pad(jnp.concatenate([c_prompt, c_sample], axis=0), ((0, (-n_c) % SUBLANES), (0, 0)))
    mod = _ada(c_all, w_ada[lyr], b_ada[lyr])
    mods_p = [mod[:bp, i * D_MODEL:(i + 1) * D_MODEL].reshape(bp, 1, D_MODEL) for i in range(6)]
    mods_s = [mod[bp:bp + ns, i * D_MODEL:(i + 1) * D_MODEL].reshape(1, ns, D_MODEL) for i in range(6)]

    sh1, sc1, g1, sh2, sc2, g2 = mods_p
    (q_t, k_f, v_f, k_b, v_t, qi_t, ki_f, ki_b, wi_t, zg, xbc, dt) = _inproj(
        x_prompt, norm1_g[lyr], sc1, sh1, w_pack, q_norm_g[lyr], k_norm_g[lyr], rows_per_mod=1, queries_on_lanes=True)
    att_p = _dsa_prompt(qi_t, wi_t, ki_b, q_t, k_b, v_t, tz, topk_p)
    ssd_p, ssm_p = _ssd_prompt(xbc, dt, zg, cw_ssd, cb_ssd, dtb_pad, alog_pad, dsk_x, ssd_g, ex, ext)
    y_p, tail_p = _ffn_prompt(att_p, ssd_p, x_prompt, g1, sc2, sh2, g2, norm2_g[lyr], wo_b, wu_b, cw_ffn, cb_ffn, wd_b)

    k_p = k_f.reshape(1, bp, s_, N_KV_HEADS, HEAD_DIM)
    v_p = v_f.reshape(1, bp, s_, N_KV_HEADS, HEAD_DIM)
    kidx_p = ki_f.reshape(1, bp, s_, IDX_DIM)
    ssm_p = ssm_p.reshape(1, bp, SSM_HEADS, SSM_HEAD_DIM, SSM_STATE)
    cssd_p = xbc[:, s_ - (SSM_CONV - 1):, :].reshape(1, bp, SSM_CONV - 1, CONV_DIM)
    cffn_p = tail_p[:, SUBLANES - (FFN_CONV - 1):, :].reshape(1, bp, FFN_CONV - 1, 2 * D_FF)

    sh1, sc1, g1, sh2, sc2, g2 = mods_s
    xs3 = x_sample.reshape(1, ns, D_MODEL)
    (q_b, k_f, v_f, _, _, qi_b, ki_f, _, wi_f, zg, xbc, dt) = _inproj(
        xs3, norm1_g[lyr], sc1, sh1, w_pack, q_norm_g[lyr], k_norm_g[lyr], rows_per_mod=ns, queries_on_lanes=False)
    k_new, v_new, ki_new = k_f[0], v_f[0], ki_f[0]

    scores = _idx_sample(page_table, qi_b[0].reshape(ns, IDX_HEADS, IDX_DIM), wi_f[0].reshape(ns, IDX_HEADS, 1),
                         ki_new.reshape(ns, 1, IDX_DIM), jnp.swapaxes(cache_kidx[lyr], 1, 2))[:, 0, :]
    idx = _sel_sample(scores.T, topk_s, past + 1).T
    is_new = (idx >= past).astype(F32)[:, :, None]
    pidx = jnp.minimum(idx, past - 1)
    page_hit = (pidx // page)[:, :, None] == jnp.arange(n_pages, dtype=jnp.int32)
    phys = jnp.sum(jnp.where(page_hit, page_table[:, None, :], 0), axis=-1)
    off = pidx % page
    bucket_hit = _t5_bucket(past - idx)[:, None, :, None] == jnp.arange(N_BUCKETS, dtype=jnp.int32)
    bias = jnp.sum(jnp.where(bucket_hit, rel_bias.astype(F32).T[None, :, None, :], 0.0), axis=-1)
    att_s = _att_sample(q_b[0].astype(F32).reshape(ns, N_HEADS, HEAD_DIM), cache_k[lyr][phys, off], cache_v[lyr][phys, off],
                        k_new.reshape(ns, N_KV_HEADS, HEAD_DIM), v_new.reshape(ns, N_KV_HEADS, HEAD_DIM), is_new,
                        bias).reshape(1, ns, D_ATT)

    hist_ssd = state_conv_ssd[lyr]
    ssd_s, ssm_s = _ssd_sample(xbc[0], [hist_ssd[:, j, :] for j in range(SSM_CONV - 1)], dt[0], zg[0], cw_ssd, cb_ssd,
                               dtb_pad, alog_pad, dsk_x, ssd_g, ex,
                               state_ssm[lyr].reshape(ns, D_SSM, SSM_STATE))
    hist_ffn = state_conv_ffn[lyr]
    y_s, u_s = _ffn_sample(att_s, ssd_s.reshape(1, ns, D_SSM), xs3, g1, sc2, sh2, g2, norm2_g[lyr], wo_b, wu_b,
                           cw_ffn, cb_ffn, wd_b, hist_ffn[:, 0, :], hist_ffn[:, 1, :])

    k_s = k_new.reshape(1, ns, 1, N_KV_HEADS, HEAD_DIM)
    v_s = v_new.reshape(1, ns, 1, N_KV_HEADS, HEAD_DIM)
    kidx_s = ki_new.reshape(1, ns, 1, IDX_DIM)
    ssm_s = ssm_s.reshape(1, ns, SSM_HEADS, SSM_HEAD_DIM, SSM_STATE)
    cssd_s = jnp.concatenate([hist_ssd[:, 1:, :], xbc[0][:, None, :]], axis=1)[None]
    cffn_s = jnp.stack([hist_ffn[:, 1, :], u_s], axis=1)[None]

    return (y_p, y_s.reshape(ns, 1, D_MODEL), k_p, v_p, kidx_p, ssm_p, cssd_p, cffn_p,
            k_s, v_s, kidx_s, ssm_s, cssd_s, cffn_s)
```

```python
import functools
import math

import jax
import jax.numpy as jnp
import numpy as np
from jax import lax
from jax.experimental import pallas as pl
from jax.experimental.pallas import tpu as pltpu

F32 = jnp.float32
BF16 = jnp.bfloat16

D_MODEL = 1024
N_HEADS = 8
N_KV_HEADS = 4
HEAD_DIM = 128
KV_REP = N_HEADS // N_KV_HEADS
IDX_HEADS = 8
IDX_DIM = 64
TOPK_MAX = 256
TOPK_DIV = 4
N_BUCKETS = 32
MAX_DISTANCE = 128
SSM_HEADS = 16
SSM_HEAD_DIM = 64
SSM_GROUPS = 4
SSM_STATE = 128
SSM_CONV = 4
SSD_CHUNK = 128
D_ATT = N_HEADS * HEAD_DIM
D_KV = N_KV_HEADS * HEAD_DIM
D_SSM = SSM_HEADS * SSM_HEAD_DIM
D_MIX = D_ATT + D_SSM
CONV_DIM = D_SSM + 2 * SSM_GROUPS * SSM_STATE
D_FF = 2816
FFN_CONV = 3
EPS = 1e-6
IN_SPLITS = (D_ATT, D_KV, D_KV, IDX_HEADS * IDX_DIM, IDX_DIM, IDX_HEADS, D_SSM, CONV_DIM, SSM_HEADS)

LANES = 128
SUBLANES = 8
VMEM_LIMIT = 52 * 1024 * 1024

OFF_Q = 0
OFF_K = OFF_Q + D_ATT
OFF_V = OFF_K + D_KV
OFF_QI = OFF_V + D_KV
OFF_KI = OFF_QI + IDX_HEADS * IDX_DIM
OFF_WI = OFF_KI + LANES
OFF_ZG = OFF_WI + LANES
OFF_XBC = OFF_ZG + D_SSM
OFF_DT = OFF_XBC + CONV_DIM
N_PACK = OFF_DT + LANES

NEG_BIG = -1e30
LOG2_E = math.log2(math.e)
ATT_CHUNK = 256
FF_CHUNK = 256
ROW_TILE = 256
Q_BLOCK = 256
PAGES_PER_DOT = 4
SCORE_PAD = 2 * ATT_CHUNK
VALUE_STEPS = 14
SNAP_ONLY_TRIPS = 4
SCAN_FOLD = 64
ATT_FOLD = 16


def _cparams(*sem):
    return pltpu.CompilerParams(dimension_semantics=sem, vmem_limit_bytes=VMEM_LIMIT)


def _resident(shape):
    nd = len(shape)
    return pl.BlockSpec(shape, lambda *_: (0,) * nd, pipeline_mode=pl.Buffered(1))


def _silu(x):
    return x * (1.0 / (1.0 + jnp.exp(-x)))


def _softplus(x):
    return jnp.maximum(x, 0.0) + jnp.log1p(jnp.exp(-jnp.abs(x)))


def _rms(x):
    return x * lax.rsqrt(jnp.mean(x * x, axis=-1, keepdims=True) + EPS)


def _split3(x):
    a = x.astype(BF16)
    r = x - a.astype(F32)
    b = r.astype(BF16)
    c = (r - b.astype(F32)).astype(BF16)
    return a, b, c


def _dot_sel_rhs(x, sel):
    return sum(jnp.dot(p, sel, preferred_element_type=F32) for p in _split3(x))


def _dot_sel_lhs(sel, x):
    return sum(jnp.dot(sel, p, preferred_element_type=F32) for p in _split3(x))


def _dot_nt(a, b):
    return lax.dot_general(a, b, (((1,), (1,)), ((), ())), preferred_element_type=F32)


def _dot_tn(a, b):
    return lax.dot_general(a, b, (((0,), (0,)), ((), ())), preferred_element_type=F32)


def _ada_kernel(c_ref, w_ref, b_ref, o_ref):
    s = _silu(c_ref[...]).astype(BF16)
    o_ref[...] = jnp.dot(s, w_ref[...].astype(BF16), preferred_element_type=F32) + b_ref[...]


def _ada(c_all, w_ada, b_ada):
    rows = c_all.shape[0]
    n = w_ada.shape[1]
    return pl.pallas_call(
        _ada_kernel,
        grid=(n // D_MODEL,),
        in_specs=[
            pl.BlockSpec((rows, D_MODEL), lambda j: (0, 0)),
            pl.BlockSpec((D_MODEL, D_MODEL), lambda j: (0, j)),
            pl.BlockSpec((1, D_MODEL), lambda j: (0, j)),
        ],
        out_specs=pl.BlockSpec((rows, D_MODEL), lambda j: (0, j)),
        out_shape=jax.ShapeDtypeStruct((rows, n), F32),
        compiler_params=_cparams("arbitrary"),
        name="ada_mod",
    )(c_all, w_ada, b_ada.reshape(1, n))


def _inproj_kernel(queries_on_lanes, x_ref, g_ref, sc_ref, sh_ref, w_ref, qg_ref, kg_ref,
                   q_o, k_o, v_o, kb_o, vb_o, qi_o, ki_o, kib_o, wi_o, zg_o, xbc_o, dt_o):
    x = x_ref[0]
    h = _rms(x) * g_ref[...]
    hb = (h * (1.0 + sc_ref[0]) + sh_ref[0]).astype(BF16)

    def sec(a, b):
        return jnp.dot(hb, w_ref[:, a:b], preferred_element_type=F32)

    zq = sec(OFF_Q, OFF_K)
    for i in range(N_HEADS):
        qn = _rms(zq[:, i * HEAD_DIM:(i + 1) * HEAD_DIM]) * qg_ref[...]
        if queries_on_lanes:
            q_o[0, i * HEAD_DIM:(i + 1) * HEAD_DIM, :] = qn.T.astype(BF16)
        else:
            q_o[0, :, i * HEAD_DIM:(i + 1) * HEAD_DIM] = qn.astype(BF16)
    zk = sec(OFF_K, OFF_V)
    for i in range(N_KV_HEADS):
        zh = zk[:, i * HEAD_DIM:(i + 1) * HEAD_DIM]
        kn = _rms(zh) * kg_ref[...]
        k_o[0, :, i, :] = kn
        kb_o[0, :, i * HEAD_DIM:(i + 1) * HEAD_DIM] = kn.astype(BF16)
    zv = sec(OFF_V, OFF_QI)
    for i in range(N_KV_HEADS):
        v_o[0, :, i, :] = zv[:, i * HEAD_DIM:(i + 1) * HEAD_DIM]
    zqi = sec(OFF_QI, OFF_KI)
    zwi = sec(OFF_WI, OFF_ZG)
    if queries_on_lanes:
        vb_o[0] = zv.T.astype(BF16)
        qi_o[0] = zqi.T.astype(BF16)
        wi_o[0] = zwi.T[:IDX_HEADS, :]
    else:
        vb_o[0] = zv.astype(BF16)
        qi_o[0] = zqi.astype(BF16)
        wi_o[0] = zwi[:, :IDX_HEADS]
    zki = sec(OFF_KI, OFF_WI)[:, :IDX_DIM]
    ki_o[0] = zki
    kib_o[0] = zki.astype(BF16)
    zg_o[0] = sec(OFF_ZG, OFF_XBC)
    xbc_o[0] = sec(OFF_XBC, OFF_DT)
    dt_o[0] = sec(OFF_DT, N_PACK)


def _inproj(x, norm_g, sc, sh, w_pack, qg, kg, rows_per_mod, queries_on_lanes):
    g_, r_, _ = x.shape
    tm = min(ROW_TILE, r_)
    nt = r_ // tm
    mod_rows = 1 if rows_per_mod == 1 else tm
    mod_map = (lambda b, i: (b, 0, 0)) if rows_per_mod == 1 else (lambda b, i: (b, i, 0))
    outs = [(D_ATT, BF16, True), (D_KV, F32, False), (D_KV, F32, False), (D_KV, BF16, False), (D_KV, BF16, True),
            (IDX_HEADS * IDX_DIM, BF16, True), (IDX_DIM, F32, False), (IDX_DIM, BF16, False), (IDX_HEADS, F32, True),
            (D_SSM, F32, False), (CONV_DIM, F32, False), (LANES, F32, False)]
    out_specs, out_shape = [], []
    for n_out, (w, dt, tr) in enumerate(outs):
        if n_out in (1, 2):
            out_specs.append(pl.BlockSpec((1, tm, N_KV_HEADS, HEAD_DIM), lambda b, i: (b, i, 0, 0)))
            out_shape.append(jax.ShapeDtypeStruct((g_, r_, N_KV_HEADS, HEAD_DIM), dt))
        elif tr and queries_on_lanes:
            out_specs.append(pl.BlockSpec((1, w, tm), lambda b, i: (b, 0, i)))
            out_shape.append(jax.ShapeDtypeStruct((g_, w, r_), dt))
        else:
            out_specs.append(pl.BlockSpec((1, tm, w), lambda b, i: (b, i, 0)))
            out_shape.append(jax.ShapeDtypeStruct((g_, r_, w), dt))
    return pl.pallas_call(
        functools.partial(_inproj_kernel, queries_on_lanes),
        grid=(g_, nt),
        in_specs=[
            pl.BlockSpec((1, tm, D_MODEL), lambda b, i: (b, i, 0)),
            _resident((1, D_MODEL)),
            pl.BlockSpec((1, mod_rows, D_MODEL), mod_map),
            pl.BlockSpec((1, mod_rows, D_MODEL), mod_map),
            _resident((D_MODEL, N_PACK)),
            _resident((1, HEAD_DIM)),
            _resident((1, HEAD_DIM)),
        ],
        out_specs=out_specs,
        out_shape=out_shape,
        compiler_params=_cparams("parallel", "parallel"),
        name="in_proj",
    )(x, norm_g.reshape(1, D_MODEL), sc, sh, w_pack, qg.reshape(1, HEAD_DIM), kg.reshape(1, HEAD_DIM))


def _ssd_prompt_kernel(xbc_ref, dt_ref, zg_ref, cw_ref, cb_ref, dtb_ref, alog_ref, dsk_ref, ng_ref,
                       ex_ref, ext_ref, y_o, st_o, xe_scr, st_scr):
    c = pl.program_id(1)
    q_ = SSD_CHUNK
    halo = SUBLANES

    @pl.when(c == 0)
    def _():
        xe_scr[0:halo, :] = jnp.zeros((halo, CONV_DIM), F32)
        st_scr[...] = jnp.zeros_like(st_scr)

    xe_scr[halo:halo + q_, :] = xbc_ref[0]
    conv = cb_ref[...] + sum(
        cw_ref[j:j + 1, :] * xe_scr[halo - (SSM_CONV - 1) + j:halo - (SSM_CONV - 1) + j + q_, :]
        for j in range(SSM_CONV))
    xe_scr[0:halo, :] = xe_scr[q_:q_ + halo, :]
    xc = _silu(conv)
    xs = xc[:, :D_SSM]
    bm = xc[:, D_SSM:D_SSM + SSM_GROUPS * SSM_STATE]
    cm = xc[:, D_SSM + SSM_GROUPS * SSM_STATE:]

    dt = _softplus(dt_ref[0] + dtb_ref[...])
    da = dt * (-jnp.exp(alog_ref[...]))
    row = lax.broadcasted_iota(jnp.int32, (q_, q_), 0)
    col = lax.broadcasted_iota(jnp.int32, (q_, q_), 1)
    causal = row >= col
    tri = jnp.where(causal, 1.0, 0.0).astype(BF16)
    acs = _dot_sel_lhs(tri, da)
    acs_t = acs.T
    acs_last = acs[q_ - 1:q_, :]
    ex = ex_ref[...]
    dt_x = _dot_sel_rhs(dt, ex)
    eacs_x = _dot_sel_rhs(jnp.exp(acs), ex)
    dte_x = _dot_sel_rhs(jnp.exp(acs_last - acs), ex)
    cdec = jnp.exp(jnp.broadcast_to(acs_t[:, q_ - 1:q_], (LANES, SSM_STATE)))
    cdec_rows = _dot_sel_lhs(ext_ref[...], cdec)

    xdt = xs * dt_x
    xdtd = xdt * dte_x
    gw = SSM_HEADS // SSM_GROUPS * SSM_HEAD_DIM
    gated = []
    for g in range(SSM_GROUPS):
        bg = bm[:, g * SSM_STATE:(g + 1) * SSM_STATE].astype(BF16)
        cg = cm[:, g * SSM_STATE:(g + 1) * SSM_STATE].astype(BF16)
        cb = _dot_nt(cg, bg)
        yd = []
        for hl in range(SSM_HEADS // SSM_GROUPS):
            h = g * (SSM_HEADS // SSM_GROUPS) + hl
            seg = acs[:, h:h + 1] - acs_t[h:h + 1, :]
            lmat = jnp.exp(jnp.where(causal, seg, -jnp.inf))
            yd.append(jnp.dot((cb * lmat).astype(BF16),
                              xdt[:, h * SSM_HEAD_DIM:(h + 1) * SSM_HEAD_DIM].astype(BF16),
                              preferred_element_type=F32))
        y_diag = jnp.concatenate(yd, axis=-1)
        st_old = st_scr[g * gw:(g + 1) * gw, :]
        y_off = _dot_nt(cg, st_old.astype(BF16)) * eacs_x[:, g * gw:(g + 1) * gw]
        st_new = _dot_tn(xdtd[:, g * gw:(g + 1) * gw].astype(BF16), bg)
        st_scr[g * gw:(g + 1) * gw, :] = st_old * cdec_rows[g * gw:(g + 1) * gw, :] + st_new
        yg = y_diag + y_off + dsk_ref[:, g * gw:(g + 1) * gw] * xs[:, g * gw:(g + 1) * gw]
        gated.append(yg * _silu(zg_ref[0, :, g * gw:(g + 1) * gw]))

    y_o[0] = (_rms(jnp.concatenate(gated, axis=-1)) * ng_ref[...]).astype(y_o.dtype)

    @pl.when(c == pl.num_programs(1) - 1)
    def _():
        st_o[0] = st_scr[...]


def _ssd_prompt(xbc, dt, zg, conv_w, conv_b, dtb_pad, alog_pad, dsk_x, norm_g, ex, ext):
    b_, s_, _ = xbc.shape
    nc = s_ // SSD_CHUNK
    return pl.pallas_call(
        _ssd_prompt_kernel,
        grid=(b_, nc),
        in_specs=[
            pl.BlockSpec((1, SSD_CHUNK, CONV_DIM), lambda b, c: (b, c, 0)),
            pl.BlockSpec((1, SSD_CHUNK, LANES), lambda b, c: (b, c, 0)),
            pl.BlockSpec((1, SSD_CHUNK, D_SSM), lambda b, c: (b, c, 0)),
            _resident((SSM_CONV, CONV_DIM)),
            _resident((1, CONV_DIM)),
            _resident((1, LANES)),
            _resident((1, LANES)),
            _resident((1, D_SSM)),
            _resident((1, D_SSM)),
            _resident((LANES, D_SSM)),
            _resident((D_SSM, LANES)),
        ],
        out_specs=[
            pl.BlockSpec((1, SSD_CHUNK, D_SSM), lambda b, c: (b, c, 0)),
            pl.BlockSpec((1, D_SSM, SSM_STATE), lambda b, c: (b, 0, 0)),
        ],
        out_shape=[
            jax.ShapeDtypeStruct((b_, s_, D_SSM), F32),
            jax.ShapeDtypeStruct((b_, D_SSM, SSM_STATE), F32),
        ],
        scratch_shapes=[
            pltpu.VMEM((SSD_CHUNK + SUBLANES, CONV_DIM), F32),
            pltpu.VMEM((D_SSM, SSM_STATE), F32),
        ],
        compiler_params=_cparams("parallel", "arbitrary"),
        name="ssd_prompt",
    )(xbc, dt, zg, conv_w, conv_b, dtb_pad, alog_pad, dsk_x, norm_g, ex, ext)


def _f32_key(x):
    b = lax.bitcast_convert_type(x, jnp.int32)
    return b ^ ((b >> 31) & 0x7FFFFFFF)


def _key_f32(k):
    return lax.bitcast_convert_type(k ^ ((k >> 31) & 0x7FFFFFFF), F32)


def _fold(x, rows, op):
    return op(x.reshape(x.shape[0] // rows, rows, x.shape[1]), axis=0)


def _kth_largest(scan, k_row, take_all):
    def count_ge(x):
        return scan(lambda t: jnp.where(t >= x, 1.0, 0.0), jnp.sum, 0.0)

    mx = scan(lambda t: t, jnp.max, -jnp.inf)
    mn = scan(lambda t: jnp.where(t == -jnp.inf, jnp.inf, t), jnp.min, jnp.inf)
    top_tied = count_ge(mx) >= k_row
    state = (mn, mx, jnp.where(top_tied, mx, mn), jnp.where(jnp.logical_or(top_tied, take_all), 1, 0).astype(jnp.int32))

    def value_step(_, s):
        lo, hi, thr, done = s
        mid = 0.5 * lo + 0.5 * hi
        inside = jnp.where(mid > lo, jnp.where(mid < hi, 1, 0), 0)
        cnt = count_ge(mid)
        ge = cnt >= k_row
        finished = jnp.where(cnt == k_row, inside, 0) * (1 - done)
        lo = jnp.where(inside == 1, jnp.where(ge, mid, lo), lo)
        hi = jnp.where(inside == 1, jnp.where(ge, hi, mid), hi)
        return lo, hi, jnp.where(finished == 1, mid, thr), done + finished

    def snap_step(s):
        lo, hi, thr, done = s
        cand = scan(lambda t: jnp.where(t < hi, t, -jnp.inf), jnp.max, -jnp.inf)
        ge = count_ge(cand) >= k_row
        finished = jnp.where(ge, 1, 0) * (1 - done)
        return lo, jnp.where(ge, hi, cand), jnp.where(finished == 1, cand, thr), done + finished

    def key_step(s):
        lo, hi, thr, done = s
        klo, khi = _f32_key(lo), _f32_key(hi)
        mid_k = (klo >> 1) + (khi >> 1) + (klo & khi & 1)
        mid = _key_f32(mid_k)
        cnt = count_ge(mid)
        ge = cnt >= k_row
        collapsed = mid_k == klo
        finished = jnp.where(collapsed, 1, jnp.where(cnt == k_row, 1, 0)) * (1 - done)
        thr = jnp.where(finished == 1, jnp.where(collapsed, lo, mid), thr)
        return jnp.where(ge, mid, lo), jnp.where(ge, hi, mid), thr, done + finished

    state = lax.fori_loop(0, VALUE_STEPS, value_step, state)

    def refine(s):
        it = s[4]
        st = snap_step(s[:4])
        st = lax.cond(it >= SNAP_ONLY_TRIPS, key_step, lambda x: x, st)
        return (*st, it + 1)

    return lax.while_loop(lambda s: jnp.logical_and(jnp.min(s[3]) == 0, s[4] < SNAP_ONLY_TRIPS + 40), refine,
                          (*state, jnp.int32(0)))[2]


def _dsa_prompt_kernel(topk, qi_ref, wi_ref, ki_ref, q_ref, k_ref, v_ref, tz_ref, o_ref,
                       sc_scr, s_scr, acc_scr):
    qb = pl.program_id(1)
    tq = q_ref.shape[2]
    ch = ATT_CHUNK
    n_ch = qb + 1

    def chunk_at(j):
        return pl.ds(pl.multiple_of(j * ch, ch), ch)

    qi_all = jnp.concatenate([qi_ref[0, h * IDX_DIM:(h + 1) * IDX_DIM, :] for h in range(IDX_HEADS)], axis=1)
    w_rows = [wi_ref[0, h:h + 1, :] for h in range(IDX_HEADS)]

    def score_chunk(j, diagonal):
        d = jnp.dot(ki_ref[0, chunk_at(j), :], qi_all, preferred_element_type=F32)
        acc = w_rows[0] * jnp.maximum(d[:, 0:tq], 0.0)
        for h in range(1, IDX_HEADS):
            acc = acc + w_rows[h] * jnp.maximum(d[:, h * tq:(h + 1) * tq], 0.0)
        if diagonal:
            causal = (lax.broadcasted_iota(jnp.int32, (ch, tq), 0) <= lax.broadcasted_iota(jnp.int32, (ch, tq), 1))
            acc = jnp.where(causal, acc, -jnp.inf)
        sc_scr[chunk_at(j), :] = acc

    def score_body(j, c):
        score_chunk(j, False)
        return c

    lax.fori_loop(0, qb, score_body, 0)
    score_chunk(qb, True)

    def chunks_reduce(fn, op, init):
        def body(j, acc):
            rows = pl.ds(pl.multiple_of(j * (2 * ch), 2 * ch), 2 * ch)
            return op(jnp.stack([acc, _fold(fn(sc_scr[rows, :]), SCAN_FOLD, op)]), axis=0)
        return op(lax.fori_loop(0, (n_ch + 1) // 2, body, jnp.full((SCAN_FOLD, tq), init, F32)), axis=0, keepdims=True)

    @pl.when(n_ch % 2 == 1)
    def _():
        sc_scr[chunk_at(n_ch), :] = jnp.full((ch, tq), -jnp.inf, F32)

    def count(cmp, thr):
        return chunks_reduce(lambda t: jnp.where(cmp(t, thr), 1.0, 0.0), jnp.sum, 0.0)

    count_ge = functools.partial(count, lambda t, x: t >= x)

    n_valid = qb * tq + lax.broadcasted_iota(jnp.int32, (1, tq), 1) + 1
    k_row = jnp.minimum(n_valid, topk).astype(F32)
    thr = _kth_largest(chunks_reduce, k_row, n_valid <= topk)

    surplus = jnp.max(count_ge(thr) - k_row)

    @pl.when(surplus <= 0.0)
    def _():
        def select_chunk(j, c):
            sc_scr[chunk_at(j), :] = jnp.where(sc_scr[chunk_at(j), :] >= thr, 0.0, NEG_BIG)
            return c
        lax.fori_loop(0, n_ch, select_chunk, 0)

    @pl.when(surplus > 0.0)
    def _():
        need = k_row - count(lambda t, x: t > x, thr)
        lower = jnp.where(lax.broadcasted_iota(jnp.int32, (ch, ch), 1) <= lax.broadcasted_iota(jnp.int32, (ch, ch), 0),
                          1.0, 0.0).astype(BF16)
        ones_r = jnp.ones((SUBLANES, ch), BF16)

        def select_chunk(j, run):
            t = sc_scr[chunk_at(j), :]
            eq = t == thr
            eqb = jnp.where(eq, 1.0, 0.0).astype(BF16)
            rank = jnp.dot(lower, eqb, preferred_element_type=F32) + run
            tie_bias = jnp.where(rank <= need, 0.0, NEG_BIG)
            sc_scr[chunk_at(j), :] = jnp.where(t > thr, 0.0, jnp.where(eq, tie_bias, NEG_BIG))
            return run + jnp.dot(ones_r, eqb, preferred_element_type=F32)[0:1, :]

        lax.fori_loop(0, n_ch, select_chunk, jnp.zeros((1, tq), F32))

    scale = HEAD_DIM ** -0.5 * LOG2_E
    cols = KV_REP * tq
    n_far = jnp.maximum(qb - 1, 0)
    for gp in range(N_KV_HEADS // 2):
        groups = (2 * gp, 2 * gp + 1)
        q_cols = [jnp.concatenate([q_ref[0, (g * KV_REP + r) * HEAD_DIM:(g * KV_REP + r + 1) * HEAD_DIM, :]
                                   for r in range(KV_REP)], axis=1) for g in groups]

        def logits_chunks(jj, mx, near, width):
            mx = list(mx)
            for j in [jj * width + i for i in range(width)]:
                selb = sc_scr[chunk_at(j), :]
                for gi, g in enumerate(groups):
                    s = jnp.dot(k_ref[0, chunk_at(j), g * HEAD_DIM:(g + 1) * HEAD_DIM], q_cols[gi],
                                preferred_element_type=F32) * scale
                    if near:
                        bias = jnp.concatenate([tz_ref[qb - j, g * KV_REP + r] + selb for r in range(KV_REP)], axis=1)
                    else:
                        bias = jnp.concatenate([selb] * KV_REP, axis=1)
                    s = s + bias
                    s_scr[gi, chunk_at(j), :] = s
                    mx[gi] = jnp.maximum(mx[gi], _fold(s, ATT_FOLD, jnp.max))
            return tuple(mx)

        mx = (jnp.full((ATT_FOLD, cols), NEG_BIG, F32),) * 2
        mx = lax.fori_loop(0, n_far // 2, functools.partial(logits_chunks, near=False, width=2), mx)
        mx = lax.fori_loop(n_far // 2 * 2, n_far, functools.partial(logits_chunks, near=False, width=1), mx)
        mx = lax.fori_loop(n_far, n_ch, functools.partial(logits_chunks, near=True, width=1), mx)
        m_row = [jnp.max(m, axis=0, keepdims=True) for m in mx]
        acc_scr[...] = jnp.zeros_like(acc_scr)

        def weights_chunks(jj, l8, width):
            l8 = list(l8)
            for j in [jj * width + i for i in range(width)]:
                for gi, g in enumerate(groups):
                    p = jnp.exp2(s_scr[gi, chunk_at(j), :] - m_row[gi])
                    l8[gi] = l8[gi] + _fold(p, ATT_FOLD, jnp.sum)
                    acc_scr[gi] += jnp.dot(v_ref[0, g * HEAD_DIM:(g + 1) * HEAD_DIM, chunk_at(j)], p.astype(BF16),
                                           preferred_element_type=F32)
            return tuple(l8)

        l8 = (jnp.zeros((ATT_FOLD, cols), F32),) * 2
        l8 = lax.fori_loop(0, n_ch // 2, functools.partial(weights_chunks, width=2), l8)
        l8 = lax.fori_loop(n_ch // 2 * 2, n_ch, functools.partial(weights_chunks, width=1), l8)
        for gi, g in enumerate(groups):
            out = acc_scr[gi] * (1.0 / jnp.sum(l8[gi], axis=0, keepdims=True))
            for r in range(KV_REP):
                h = g * KV_REP + r
                o_ref[0, :, h * HEAD_DIM:(h + 1) * HEAD_DIM] = out[:, r * tq:(r + 1) * tq].T.astype(o_ref.dtype)


def _dsa_prompt(qi_t, wi_t, ki_b, q_t, k_b, v_t, tz, topk):
    b_, s_, _ = k_b.shape
    tq = min(Q_BLOCK, s_)
    assert tq == ATT_CHUNK and s_ % (2 * tq) == 0
    once = pl.Buffered(1)
    q_tile = lambda rows: pl.BlockSpec((1, rows, tq), lambda b, i: (b, 0, i))
    return pl.pallas_call(
        functools.partial(_dsa_prompt_kernel, topk),
        grid=(b_, s_ // tq),
        in_specs=[
            q_tile(IDX_HEADS * IDX_DIM),
            q_tile(IDX_HEADS),
            pl.BlockSpec((1, s_, IDX_DIM), lambda b, i: (b, 0, 0), pipeline_mode=once),
            q_tile(D_ATT),
            pl.BlockSpec((1, s_, D_KV), lambda b, i: (b, 0, 0), pipeline_mode=once),
            pl.BlockSpec((1, D_KV, s_), lambda b, i: (b, 0, 0), pipeline_mode=once),
            _resident(tz.shape),
        ],
        out_specs=pl.BlockSpec((1, tq, D_ATT), lambda b, i: (b, i, 0)),
        out_shape=jax.ShapeDtypeStruct((b_, s_, D_ATT), BF16),
        scratch_shapes=[
            pltpu.VMEM((s_, tq), F32),
            pltpu.VMEM((2, s_, KV_REP * tq), F32),
            pltpu.VMEM((2, HEAD_DIM, KV_REP * tq), F32),
        ],
        compiler_params=_cparams("parallel", "arbitrary"),
        name="dsa_prompt",
    )(qi_t, wi_t, ki_b, q_t, k_b, v_t, tz)


def _mix_and_norm(att_ref, ssd_ref, x_ref, g1_ref, sc2_ref, sh2_ref, n2_ref, wo_ref):
    o = (jnp.dot(att_ref[0].astype(BF16), wo_ref[:D_ATT, :], preferred_element_type=F32)
         + jnp.dot(ssd_ref[0].astype(BF16), wo_ref[D_ATT:, :], preferred_element_type=F32))
    x1 = x_ref[0] + g1_ref[0] * o
    h2 = ((_rms(x1) * n2_ref[...]) * (1.0 + sc2_ref[0]) + sh2_ref[0]).astype(BF16)
    return x1, h2


def _ffn_prompt_kernel(att_ref, ssd_ref, x_ref, g1_ref, sc2_ref, sh2_ref, g2_ref, n2_ref, wo_ref, wu_ref,
                       cw_ref, cb_ref, wd_ref, y_o, tail_o, a_scr, ug_scr, uv_scr, carry_scr):
    i = pl.program_id(1)
    tm = x_ref.shape[1]
    halo = SUBLANES

    @pl.when(i == 0)
    def _():
        carry_scr[...] = jnp.zeros_like(carry_scr)

    x1, h2 = _mix_and_norm(att_ref, ssd_ref, x_ref, g1_ref, sc2_ref, sh2_ref, n2_ref, wo_ref)

    def conv_cols(scr, c0):
        u = jnp.dot(h2, wu_ref[:, c0:c0 + FF_CHUNK], preferred_element_type=F32)
        scr[0:halo, :] = carry_scr[:, c0:c0 + FF_CHUNK]
        scr[halo:halo + tm, :] = u
        carry_scr[:, c0:c0 + FF_CHUNK] = u[tm - halo:, :]
        return cb_ref[:, c0:c0 + FF_CHUNK] + sum(
            cw_ref[j:j + 1, c0:c0 + FF_CHUNK] * scr[halo - (FFN_CONV - 1) + j:halo - (FFN_CONV - 1) + j + tm, :]
            for j in range(FFN_CONV))

    for jc in range(D_FF // FF_CHUNK):
        gate = conv_cols(ug_scr.at[jc % 2], jc * FF_CHUNK)
        val = conv_cols(uv_scr.at[jc % 2], D_FF + jc * FF_CHUNK)
        a_scr[:, jc * FF_CHUNK:(jc + 1) * FF_CHUNK] = (_silu(gate) * val).astype(BF16)

    y_o[0] = x1 + g2_ref[0] * jnp.dot(a_scr[...], wd_ref[...], preferred_element_type=F32)
    tail_o[0] = carry_scr[...]


def _ffn_prompt(att, ssd, x, g1, sc2, sh2, g2, norm2_g, wo_b, wu_b, cw, cb, wd_b):
    b_, s_, _ = x.shape
    tm = min(ROW_TILE, s_)
    nt = s_ // tm
    row = lambda w: pl.BlockSpec((1, tm, w), lambda b, i: (b, i, 0))
    mod = pl.BlockSpec((1, 1, D_MODEL), lambda b, i: (b, 0, 0))
    return pl.pallas_call(
        _ffn_prompt_kernel,
        grid=(b_, nt),
        in_specs=[row(D_ATT), row(D_SSM), row(D_MODEL), mod, mod, mod, mod, _resident((1, D_MODEL)),
                  _resident((D_MIX, D_MODEL)), _resident((D_MODEL, 2 * D_FF)), _resident((FFN_CONV, 2 * D_FF)),
                  _resident((1, 2 * D_FF)), _resident((D_FF, D_MODEL))],
        out_specs=[row(D_MODEL), pl.BlockSpec((1, SUBLANES, 2 * D_FF), lambda b, i: (b, 0, 0))],
        out_shape=[jax.ShapeDtypeStruct((b_, s_, D_MODEL), F32),
                   jax.ShapeDtypeStruct((b_, SUBLANES, 2 * D_FF), F32)],
        scratch_shapes=[pltpu.VMEM((tm, D_FF), BF16), pltpu.VMEM((2, tm + SUBLANES, FF_CHUNK), F32),
                        pltpu.VMEM((2, tm + SUBLANES, FF_CHUNK), F32), pltpu.VMEM((SUBLANES, 2 * D_FF), F32)],
        compiler_params=_cparams("parallel", "arbitrary"),
        name="ffn_prompt",
    )(att, ssd, x, g1, sc2, sh2, g2, norm2_g.reshape(1, D_MODEL), wo_b, wu_b, cw, cb, wd_b)


def _ffn_sample_kernel(att_ref, ssd_ref, x_ref, g1_ref, sc2_ref, sh2_ref, g2_ref, n2_ref, wo_ref, wu_ref,
                       cw_ref, cb_ref, wd_ref, h0_ref, h1_ref, y_o, u_o, acc_scr):
    x1, h2 = _mix_and_norm(att_ref, ssd_ref, x_ref, g1_ref, sc2_ref, sh2_ref, n2_ref, wo_ref)
    acc_scr[...] = jnp.zeros_like(acc_scr)

    def conv_cols(c0):
        u = jnp.dot(h2, wu_ref[:, c0:c0 + FF_CHUNK], preferred_element_type=F32)
        u_o[:, c0:c0 + FF_CHUNK] = u
        sl = slice(c0, c0 + FF_CHUNK)
        return (cb_ref[:, sl] + cw_ref[0:1, sl] * h0_ref[:, sl] + cw_ref[1:2, sl] * h1_ref[:, sl]
                + cw_ref[2:3, sl] * u)

    for jc in range(D_FF // FF_CHUNK):
        gate = conv_cols(jc * FF_CHUNK)
        val = conv_cols(D_FF + jc * FF_CHUNK)
        a = (_silu(gate) * val).astype(BF16)
        acc_scr[...] += jnp.dot(a, wd_ref[jc * FF_CHUNK:(jc + 1) * FF_CHUNK, :], preferred_element_type=F32)

    y_o[0] = x1 + g2_ref[0] * acc_scr[...]


def _ffn_sample(att, ssd, x, g1, sc2, sh2, g2, norm2_g, wo_b, wu_b, cw, cb, wd_b, hist0, hist1):
    n = x.shape[1]
    full = lambda *shape: pl.BlockSpec(shape, lambda: (0,) * len(shape))
    r3 = lambda w: full(1, n, w)
    return pl.pallas_call(
        _ffn_sample_kernel,
        in_specs=[r3(D_ATT), r3(D_SSM), r3(D_MODEL), r3(D_MODEL), r3(D_MODEL), r3(D_MODEL), r3(D_MODEL),
                  full(1, D_MODEL), full(D_MIX, D_MODEL), full(D_MODEL, 2 * D_FF), full(FFN_CONV, 2 * D_FF),
                  full(1, 2 * D_FF), full(D_FF, D_MODEL), full(n, 2 * D_FF), full(n, 2 * D_FF)],
        out_specs=[r3(D_MODEL), full(n, 2 * D_FF)],
        out_shape=[jax.ShapeDtypeStruct((1, n, D_MODEL), F32), jax.ShapeDtypeStruct((n, 2 * D_FF), F32)],
        scratch_shapes=[pltpu.VMEM((n, D_MODEL), F32)],
        compiler_params=pltpu.CompilerParams(vmem_limit_bytes=VMEM_LIMIT),
        name="ffn_sample",
    )(att, ssd, x, g1, sc2, sh2, g2, norm2_g.reshape(1, D_MODEL), wo_b, wu_b, cw, cb, wd_b, hist0, hist1)


def _ssd_sample_kernel(xbc_ref, h0_ref, h1_ref, h2_ref, dt_ref, zg_ref, cw_ref, cb_ref, dtb_ref, alog_ref,
                       dsk_ref, ng_ref, ex_ref, st_ref, y_o, st_o):
    nb = xbc_ref.shape[0]
    conv = (cb_ref[...] + cw_ref[0:1, :] * h0_ref[...] + cw_ref[1:2, :] * h1_ref[...]
            + cw_ref[2:3, :] * h2_ref[...] + cw_ref[3:4, :] * xbc_ref[...])
    xc = _silu(conv)
    xs = xc[:, :D_SSM]
    bm = xc[:, D_SSM:D_SSM + SSM_GROUPS * SSM_STATE]
    cm = xc[:, D_SSM + SSM_GROUPS * SSM_STATE:]
    dt = _softplus(dt_ref[...] + dtb_ref[...])
    dec = jnp.exp(dt * (-jnp.exp(alog_ref[...])))
    ex = ex_ref[...]
    xdt = xs * _dot_sel_rhs(dt, ex)
    dec_x = _dot_sel_rhs(dec, ex)
    gw = SSM_HEADS // SSM_GROUPS * SSM_HEAD_DIM
    ones_n = jnp.ones((SUBLANES, SSM_STATE), BF16)
    row_id = lax.broadcasted_iota(jnp.int32, (SUBLANES, 1), 0)

    def stack(rows):
        out = jnp.zeros((SUBLANES, rows[0].shape[1]), F32)
        for r, v in enumerate(rows):
            out = jnp.where(row_id == r, v.astype(F32), out)
        return out.astype(BF16)

    ys = []
    for i in range(nb):
        yrow = []
        for g in range(SSM_GROUPS):
            hs = st_ref[i, g * gw:(g + 1) * gw, :]
            d0, d1, d2 = _split3(dec_x[i:i + 1, g * gw:(g + 1) * gw])
            x0, x1, _ = _split3(xdt[i:i + 1, g * gw:(g + 1) * gw])
            b0, b1, _ = _split3(bm[i:i + 1, g * SSM_STATE:(g + 1) * SSM_STATE])
            c0, c1, _ = _split3(cm[i:i + 1, g * SSM_STATE:(g + 1) * SSM_STATE])
            dec_b = _dot_tn(stack([d0, d1, d2]), ones_n)
            upd = _dot_tn(stack([x0, x0, x1, x1]), stack([b0, b1, b0, b1]))
            hn = hs * dec_b + upd
            st_o[i, g * gw:(g + 1) * gw, :] = hn
            h0, h1, _ = _split3(hn)
            c_rows = stack([c0, c1])
            yg = _dot_nt(c_rows, h0) + _dot_nt(c_rows, h1)
            yrow.append(yg[0:1, :] + yg[1:2, :])
        ys.append(jnp.concatenate(yrow, axis=-1))
    y = jnp.concatenate(ys, axis=0) + dsk_ref[...] * xs
    y = y * _silu(zg_ref[...])
    y_o[...] = _rms(y) * ng_ref[...]


def _ssd_sample(xbc, hist, dt, zg, conv_w, conv_b, dtb_pad, alog_pad, dsk_x, norm_g, ex, state):
    n = xbc.shape[0]
    nb = SUBLANES
    rows = lambda w: pl.BlockSpec((nb, w), lambda i: (i, 0))
    st = pl.BlockSpec((nb, D_SSM, SSM_STATE), lambda i: (i, 0, 0))
    return pl.pallas_call(
        _ssd_sample_kernel,
        grid=(n // nb,),
        in_specs=[rows(CONV_DIM), rows(CONV_DIM), rows(CONV_DIM), rows(CONV_DIM), rows(LANES), rows(D_SSM),
                  _resident((SSM_CONV, CONV_DIM)), _resident((1, CONV_DIM)), _resident((1, LANES)),
                  _resident((1, LANES)), _resident((1, D_SSM)), _resident((1, D_SSM)), _resident((LANES, D_SSM)), st],
        out_specs=[rows(D_SSM), st],
        out_shape=[jax.ShapeDtypeStruct((n, D_SSM), F32), jax.ShapeDtypeStruct((n, D_SSM, SSM_STATE), F32)],
        compiler_params=_cparams("parallel"),
        name="ssd_sample",
    )(xbc, hist[0], hist[1], hist[2], dt, zg, conv_w, conv_b, dtb_pad, alog_pad, dsk_x, norm_g, ex, state)


def _idx_sample_kernel(pt_ref, qi_ref, wi_ref, kin_ref, pool_ref, o_ref, kbuf, sem):
    b = pl.program_id(0)
    nb = pl.num_programs(0)
    n_pages = kbuf.shape[1]
    page = kbuf.shape[3]

    def page_copy(seq, slot, j):
        return pltpu.make_async_copy(pool_ref.at[pt_ref[seq, j]], kbuf.at[slot, j], sem.at[slot])

    def fetch(seq, slot):
        def body(j, c):
            page_copy(seq, slot, j).start()
            return c
        lax.fori_loop(0, n_pages, body, 0, unroll=8)

    @pl.when(b == 0)
    def _():
        fetch(0, 0)

    @pl.when(b + 1 < nb)
    def _():
        fetch(b + 1, (b + 1) % 2)

    slot = b % 2

    def wait_body(j, c):
        page_copy(b, slot, j).wait()
        return c
    lax.fori_loop(0, n_pages, wait_body, 0)

    qi = qi_ref[0]
    w = wi_ref[0]

    span = PAGES_PER_DOT * page

    def page_scores(j, c):
        keys = jnp.concatenate([kbuf[slot, PAGES_PER_DOT * j + i] for i in range(PAGES_PER_DOT)], axis=1)
        d = jnp.dot(qi, keys.astype(BF16), preferred_element_type=F32)
        o_ref[0, :, pl.ds(pl.multiple_of(j * span, span), span)] = jnp.sum(w * jnp.maximum(d, 0.0), axis=0, keepdims=True)
        return c

    lax.fori_loop(0, n_pages // PAGES_PER_DOT, page_scores, 0, unroll=True)
    knew = kin_ref[0].astype(BF16).astype(F32)
    dn = jnp.sum(qi.astype(F32) * knew, axis=-1, keepdims=True)
    s_new = jnp.sum(w * jnp.maximum(dn, 0.0), axis=0, keepdims=True)
    lane = lax.broadcasted_iota(jnp.int32, (1, SCORE_PAD), 1)
    o_ref[0, :, n_pages * page:] = jnp.where(lane == 0, s_new, -jnp.inf)


def _idx_sample(page_table, qi3, wi3, ki_new3, pool_t):
    n, n_pages = page_table.shape
    page = pool_t.shape[2]
    assert n_pages % (8 * PAGES_PER_DOT) == 0
    past = n_pages * page
    grid_spec = pltpu.PrefetchScalarGridSpec(
        num_scalar_prefetch=1,
        grid=(n,),
        in_specs=[
            pl.BlockSpec((1, IDX_HEADS, IDX_DIM), lambda b, pt: (b, 0, 0)),
            pl.BlockSpec((1, IDX_HEADS, 1), lambda b, pt: (b, 0, 0)),
            pl.BlockSpec((1, 1, IDX_DIM), lambda b, pt: (b, 0, 0)),
            pl.BlockSpec(memory_space=pl.ANY),
        ],
        out_specs=pl.BlockSpec((1, 1, past + SCORE_PAD), lambda b, pt: (b, 0, 0)),
        scratch_shapes=[pltpu.VMEM((2, n_pages, IDX_DIM, page), F32), pltpu.SemaphoreType.DMA((2,))],
    )
    return pl.pallas_call(
        _idx_sample_kernel,
        grid_spec=grid_spec,
        out_shape=jax.ShapeDtypeStruct((n, 1, past + SCORE_PAD), F32),
        compiler_params=_cparams("arbitrary"),
        name="idx_sample",
    )(page_table, qi3, wi3, ki_new3, pool_t)


def _att_sample_kernel(q_ref, k_ref, v_ref, knew_ref, vnew_ref, isnew_ref, bias_ref, o_ref):
    nb = q_ref.shape[0]
    scale = HEAD_DIM ** -0.5
    head_group = lax.broadcasted_iota(jnp.int32, (N_HEADS, 1), 0) // KV_REP
    for i in range(nb):
        q = q_ref[i].astype(BF16)
        bias = bias_ref[i]
        is_new = isnew_ref[i] > 0.5
        out = jnp.zeros((N_HEADS, HEAD_DIM), F32)
        for g in range(N_KV_HEADS):
            kg = jnp.where(is_new, knew_ref[i, g:g + 1, :], k_ref[i, :, g, :]).astype(BF16)
            vg = jnp.where(is_new, vnew_ref[i, g:g + 1, :], v_ref[i, :, g, :]).astype(BF16)
            s = _dot_nt(q, kg) * scale + bias
            m = jnp.max(s, axis=-1, keepdims=True)
            p = jnp.exp(s - m)
            p = p * (1.0 / jnp.sum(p, axis=-1, keepdims=True))
            og = jnp.dot(p.astype(BF16), vg, preferred_element_type=F32)
            out = jnp.where(head_group == g, og, out)
        o_ref[i] = out


def _att_sample(q3, k_sel, v_sel, k_new, v_new, is_new, bias):
    n, kk = k_sel.shape[:2]
    nb = SUBLANES // 2
    rows4 = pl.BlockSpec((nb, kk, N_KV_HEADS, HEAD_DIM), lambda i: (i, 0, 0, 0))
    new3 = pl.BlockSpec((nb, N_KV_HEADS, HEAD_DIM), lambda i: (i, 0, 0))
    return pl.pallas_call(
        _att_sample_kernel,
        grid=(n // nb,),
        in_specs=[
            pl.BlockSpec((nb, N_HEADS, HEAD_DIM), lambda i: (i, 0, 0)),
            rows4, rows4, new3, new3,
            pl.BlockSpec((nb, kk, 1), lambda i: (i, 0, 0)),
            pl.BlockSpec((nb, N_HEADS, kk), lambda i: (i, 0, 0)),
        ],
        out_specs=pl.BlockSpec((nb, N_HEADS, HEAD_DIM), lambda i: (i, 0, 0)),
        out_shape=jax.ShapeDtypeStruct((n, N_HEADS, HEAD_DIM), F32),
        compiler_params=_cparams("parallel"),
        name="att_sample",
    )(q3, k_sel, v_sel, k_new, v_new, is_new, bias)


def _sel_sample_kernel(k_keep, n_valid, sc_ref, o_ref):
    rows, n = sc_ref.shape
    ch = ATT_CHUNK
    span = 2 * ch

    def scan(fn, op, init):
        def body(j, acc):
            r = pl.ds(pl.multiple_of(j * span, span), span)
            return op(jnp.stack([acc, _fold(fn(sc_ref[r, :]), SCAN_FOLD, op)]), axis=0)
        return op(lax.fori_loop(0, rows // span, body, jnp.full((SCAN_FOLD, n), init, F32)), axis=0, keepdims=True)

    k_row = jnp.full((1, n), float(k_keep), F32)
    thr = _kth_largest(scan, k_row, jnp.full((1, n), n_valid <= k_keep))
    need = k_row - scan(lambda t: jnp.where(t > thr, 1.0, 0.0), jnp.sum, 0.0)
    lower = jnp.where(lax.broadcasted_iota(jnp.int32, (ch, ch), 1) <= lax.broadcasted_iota(jnp.int32, (ch, ch), 0),
                      1.0, 0.0).astype(BF16)
    ones_r = jnp.ones((SUBLANES, ch), BF16)
    key_local = lax.broadcasted_iota(jnp.int32, (ch, n), 0).astype(F32)
    slot_iota = lax.broadcasted_iota(jnp.int32, (k_keep, n), 0).astype(F32)
    none = float(rows)

    def chunk_body(c, carry):
        run, slot, out = carry
        t = sc_ref[pl.ds(pl.multiple_of(c * ch, ch), ch), :]
        eq = t == thr
        eqb = jnp.where(eq, 1.0, 0.0).astype(BF16)
        rank = jnp.dot(lower, eqb, preferred_element_type=F32) + run
        cand = jnp.where(t > thr, key_local, jnp.where(eq, jnp.where(rank <= need, key_local, none), none))
        left = _fold(jnp.where(cand < none, 1.0, 0.0), SUBLANES, jnp.sum).sum(axis=0, keepdims=True)
        base = jnp.asarray(c * ch, F32)

        def extract(s):
            cand, out, slot, left = s
            cur = jnp.min(cand, axis=0, keepdims=True)
            has = cur < none
            out = jnp.where(slot_iota == slot, jnp.where(has, cur + base, out), out)
            inc = jnp.where(has, 1.0, 0.0)
            return jnp.where(cand == cur, none, cand), out, slot + inc, left - inc

        _, out, slot, _ = lax.while_loop(lambda s: jnp.max(s[3]) > 0.0, extract, (cand, out, slot, left))
        return run + jnp.dot(ones_r, eqb, preferred_element_type=F32)[0:1, :], slot, out

    zeros_row = jnp.zeros((1, n), F32)
    _, _, out = lax.fori_loop(0, rows // ch, chunk_body, (zeros_row, zeros_row, jnp.zeros((k_keep, n), F32)))
    o_ref[...] = out.astype(jnp.int32)


def _sel_sample(scores_t, k_keep, n_valid):
    rows, n = scores_t.shape
    assert rows % (2 * ATT_CHUNK) == 0
    return pl.pallas_call(
        functools.partial(_sel_sample_kernel, k_keep, n_valid),
        in_specs=[pl.BlockSpec((rows, n), lambda: (0, 0))],
        out_specs=pl.BlockSpec((k_keep, n), lambda: (0, 0)),
        out_shape=jax.ShapeDtypeStruct((k_keep, n), jnp.int32),
        compiler_params=pltpu.CompilerParams(vmem_limit_bytes=VMEM_LIMIT),
        name="sel_sample",
    )(scores_t)


def _pack_w_in(w_in):
    offs = np.cumsum((0,) + IN_SPLITS)
    parts = [w_in[:, offs[i]:offs[i + 1]] for i in range(len(IN_SPLITS))]
    pad = lambda a: jnp.pad(a, ((0, 0), (0, LANES - a.shape[1])))
    q, k, v, qi, ki, wi, zg, xbc, dt = parts
    return jnp.concatenate([q, k, v, qi, pad(ki), pad(wi), zg, xbc, pad(dt)], axis=1).astype(BF16)


def _t5_bucket(dist):
    dist = jnp.maximum(dist, 0)
    max_exact = N_BUCKETS // 2
    d = jnp.maximum(dist, 1).astype(F32)
    large = max_exact + (jnp.log(d / max_exact) / math.log(MAX_DISTANCE / max_exact) * (N_BUCKETS - max_exact)).astype(jnp.int32)
    large = jnp.minimum(large, N_BUCKETS - 1)
    return jnp.where(dist < max_exact, dist, large)


def _rel_bias_tables(rel_bias, tq):
    assert ATT_CHUNK == tq and tq >= MAX_DISTANCE
    n = 2 * tq
    idx = jnp.arange(n, dtype=jnp.int32)
    query_minus_key = jnp.where(idx < tq, idx, idx - n)
    far = rel_bias[_t5_bucket(jnp.int32(MAX_DISTANCE))]
    tabs = []
    for d0 in (0, tq):
        f = (rel_bias[_t5_bucket(d0 + query_minus_key)] - far[None, :]) * LOG2_E
        flat = jnp.tile(f.T, (1, ATT_CHUNK))
        tabs.append(flat[:, :ATT_CHUNK * (n - 1)].reshape(-1, ATT_CHUNK, n - 1)[:, :, :tq])
    return jnp.stack(tabs)


def _head_expand():
    h = jnp.arange(LANES, dtype=jnp.int32)[:, None]
    c = jnp.arange(D_SSM, dtype=jnp.int32)[None, :]
    ex = (c // SSM_HEAD_DIM == h).astype(BF16)
    return ex, ex.T


def _pad_lanes(v):
    return jnp.pad(v.astype(F32), (0, LANES - v.shape[0])).reshape(1, LANES)


def kernel(x_prompt, x_sample, cache_k, cache_v, cache_kidx, state_ssm, state_conv_ssd, state_conv_ffn, page_table,
           c_prompt, c_sample, rel_bias, norm1_g, w_ada, b_ada, w_in, q_norm_g, k_norm_g, conv_ssd_w, conv_ssd_b,
           dt_bias, a_log, d_skip, ssd_norm_g, w_out, norm2_g, w_up, conv_ffn_w, conv_ffn_b, w_down):
    depth = w_in.shape[0]
    assert depth == 1
    bp, s_, _ = x_prompt.shape
    ns = x_sample.shape[0]
    assert x_sample.shape[1] == 1
    lyr = 0
    n_pages = page_table.shape[1]
    page = cache_k.shape[2]
    past = n_pages * page
    topk_p = min(TOPK_MAX, s_ // TOPK_DIV)
    topk_s = min(TOPK_MAX, (past + 1) // TOPK_DIV)

    w_pack = _pack_w_in(w_in[lyr])
    wo_b = w_out[lyr].astype(BF16)
    wu_b = w_up[lyr].astype(BF16)
    wd_b = w_down[lyr].astype(BF16)
    ex, ext = _head_expand()
    dtb_pad = _pad_lanes(dt_bias[lyr])
    alog_pad = _pad_lanes(a_log[lyr])
    dsk_x = jnp.repeat(d_skip[lyr].astype(F32), SSM_HEAD_DIM).reshape(1, D_SSM)
    ssd_g = ssd_norm_g[lyr].reshape(1, D_SSM)
    cw_ssd, cb_ssd = conv_ssd_w[lyr], conv_ssd_b[lyr].reshape(1, CONV_DIM)
    cw_ffn, cb_ffn = conv_ffn_w[lyr], conv_ffn_b[lyr].reshape(1, 2 * D_FF)
    tz = _rel_bias_tables(rel_bias.astype(F32), min(Q_BLOCK, s_))

    n_c = bp + ns
    c_all = jnp.pad(jnp.concatenate([c_prompt, c_sample], axis=0), ((0, (-n_c) % SUBLANES), (0, 0)))
    mod = _ada(c_all, w_ada[lyr], b_ada[lyr])
    mods_p = [mod[:bp, i * D_MODEL:(i + 1) * D_MODEL].reshape(bp, 1, D_MODEL) for i in range(6)]
    mods_s = [mod[bp:bp + ns, i * D_MODEL:(i + 1) * D_MODEL].reshape(1, ns, D_MODEL) for i in range(6)]

    sh1, sc1, g1, sh2, sc2, g2 = mods_p
    (q_t, k_f, v_f, k_b, v_t, qi_t, ki_f, ki_b, wi_t, zg, xbc, dt) = _inproj(
        x_prompt, norm1_g[lyr], sc1, sh1, w_pack, q_norm_g[lyr], k_norm_g[lyr], rows_per_mod=1, queries_on_lanes=True)
    att_p = _dsa_prompt(qi_t, wi_t, ki_b, q_t, k_b, v_t, tz, topk_p)
    ssd_p, ssm_p = _ssd_prompt(xbc, dt, zg, cw_ssd, cb_ssd, dtb_pad, alog_pad, dsk_x, ssd_g, ex, ext)
    y_p, tail_p = _ffn_prompt(att_p, ssd_p, x_prompt, g1, sc2, sh2, g2, norm2_g[lyr], wo_b, wu_b, cw_ffn, cb_ffn, wd_b)

    k_p = k_f.reshape(1, bp, s_, N_KV_HEADS, HEAD_DIM)
    v_p = v_f.reshape(1, bp, s_, N_KV_HEADS, HEAD_DIM)
    kidx_p = ki_f.reshape(1, bp, s_, IDX_DIM)
    ssm_p = ssm_p.reshape(1, bp, SSM_HEADS, SSM_HEAD_DIM, SSM_STATE)
    cssd_p = xbc[:, s_ - (SSM_CONV - 1):, :].reshape(1, bp, SSM_CONV - 1, CONV_DIM)
    cffn_p = tail_p[:, SUBLANES - (FFN_CONV - 1):, :].reshape(1, bp, FFN_CONV - 1, 2 * D_FF)

    sh1, sc1, g1, sh2, sc2, g2 = mods_s
    xs3 = x_sample.reshape(1, ns, D_MODEL)
    (q_b, k_f, v_f, _, _, qi_b, ki_f, _, wi_f, zg, xbc, dt) = _inproj(
        xs3, norm1_g[lyr], sc1, sh1, w_pack, q_norm_g[lyr], k_norm_g[lyr], rows_per_mod=ns, queries_on_lanes=False)
    k_new, v_new, ki_new = k_f[0], v_f[0], ki_f[0]

    scores = _idx_sample(page_table, qi_b[0].reshape(ns, IDX_HEADS, IDX_DIM), wi_f[0].reshape(ns, IDX_HEADS, 1),
                         ki_new.reshape(ns, 1, IDX_DIM), jnp.swapaxes(cache_kidx[lyr], 1, 2))[:, 0, :]
    idx = _sel_sample(scores.T, topk_s, past + 1).T
    is_new = (idx >= past).astype(F32)[:, :, None]
    pidx = jnp.minimum(idx, past - 1)
    page_hit = (pidx // page)[:, :, None] == jnp.arange(n_pages, dtype=jnp.int32)
    phys = jnp.sum(jnp.where(page_hit, page_table[:, None, :], 0), axis=-1)
    off = pidx % page
    bucket_hit = _t5_bucket(past - idx)[:, None, :, None] == jnp.arange(N_BUCKETS, dtype=jnp.int32)
    bias = jnp.sum(jnp.where(bucket_hit, rel_bias.astype(F32).T[None, :, None, :], 0.0), axis=-1)
    att_s = _att_sample(q_b[0].astype(F32).reshape(ns, N_HEADS, HEAD_DIM), cache_k[lyr][phys, off], cache_v[lyr][phys, off],
                        k_new.reshape(ns, N_KV_HEADS, HEAD_DIM), v_new.reshape(ns, N_KV_HEADS, HEAD_DIM), is_new,
                        bias).reshape(1, ns, D_ATT)

    hist_ssd = state_conv_ssd[lyr]
    ssd_s, ssm_s = _ssd_sample(xbc[0], [hist_ssd[:, j, :] for j in range(SSM_CONV - 1)], dt[0], zg[0], cw_ssd, cb_ssd,
                               dtb_pad, alog_pad, dsk_x, ssd_g, ex,
                               state_ssm[lyr].reshape(ns, D_SSM, SSM_STATE))
    hist_ffn = state_conv_ffn[lyr]
    y_s, u_s = _ffn_sample(att_s, ssd_s.reshape(1, ns, D_SSM), xs3, g1, sc2, sh2, g2, norm2_g[lyr], wo_b, wu_b,
                           cw_ffn, cb_ffn, wd_b, hist_ffn[:, 0, :], hist_ffn[:, 1, :])

    k_s = k_new.reshape(1, ns, 1, N_KV_HEADS, HEAD_DIM)
    v_s = v_new.reshape(1, ns, 1, N_KV_HEADS, HEAD_DIM)
    kidx_s = ki_new.reshape(1, ns, 1, IDX_DIM)
    ssm_s = ssm_s.reshape(1, ns, SSM_HEADS, SSM_HEAD_DIM, SSM_STATE)
    cssd_s = jnp.concatenate([hist_ssd[:, 1:, :], xbc[0][:, None, :]], axis=1)[None]
    cffn_s = jnp.stack([hist_ffn[:, 1, :], u_s], axis=1)[None]

    return (y_p, y_s.reshape(ns, 1, D_MODEL), k_p, v_p, kidx_p, ssm_p, cssd_p, cffn_p,
            k_s, v_s, kidx_s, ssm_s, cssd_s, cffn_s)
```

```python
import functools
import math

import jax
import jax.numpy as jnp
import numpy as np
from jax import lax
from jax.experimental import pallas as pl
from jax.experimental.pallas import tpu as pltpu

F32 = jnp.float32
BF16 = jnp.bfloat16

D_MODEL = 1024
N_HEADS = 8
N_KV_HEADS = 4
HEAD_DIM = 128
KV_REP = N_HEADS // N_KV_HEADS
IDX_HEADS = 8
IDX_DIM = 64
TOPK_MAX = 256
TOPK_DIV = 4
N_BUCKETS = 32
MAX_DISTANCE = 128
SSM_HEADS = 16
SSM_HEAD_DIM = 64
SSM_GROUPS = 4
SSM_STATE = 128
SSM_CONV = 4
SSD_CHUNK = 128
D_ATT = N_HEADS * HEAD_DIM
D_KV = N_KV_HEADS * HEAD_DIM
D_SSM = SSM_HEADS * SSM_HEAD_DIM
D_MIX = D_ATT + D_SSM
CONV_DIM = D_SSM + 2 * SSM_GROUPS * SSM_STATE
D_FF = 2816
FFN_CONV = 3
EPS = 1e-6
IN_SPLITS = (D_ATT, D_KV, D_KV, IDX_HEADS * IDX_DIM, IDX_DIM, IDX_HEADS, D_SSM, CONV_DIM, SSM_HEADS)

LANES = 128
SUBLANES = 8
VMEM_LIMIT = 52 * 1024 * 1024

OFF_Q = 0
OFF_K = OFF_Q + D_ATT
OFF_V = OFF_K + D_KV
OFF_QI = OFF_V + D_KV
OFF_KI = OFF_QI + IDX_HEADS * IDX_DIM
OFF_WI = OFF_KI + LANES
OFF_ZG = OFF_WI + LANES
OFF_XBC = OFF_ZG + D_SSM
OFF_DT = OFF_XBC + CONV_DIM
N_PACK = OFF_DT + LANES

NEG_BIG = -1e30
LOG2_E = math.log2(math.e)
ATT_CHUNK = 256
FF_CHUNK = 256
ROW_TILE = 256
Q_BLOCK = 256
PAGES_PER_DOT = 4
SCORE_PAD = 2 * ATT_CHUNK
VALUE_STEPS = 14
SNAP_ONLY_TRIPS = 4
SCAN_FOLD = 64
ATT_FOLD = 16
TRIP_WIDTHS = (4, 2, 1)


def _cparams(*sem):
    return pltpu.CompilerParams(dimension_semantics=sem, vmem_limit_bytes=VMEM_LIMIT)


def _resident(shape):
    nd = len(shape)
    return pl.BlockSpec(shape, lambda *_: (0,) * nd, pipeline_mode=pl.Buffered(1))


def _silu(x):
    return x * (1.0 / (1.0 + jnp.exp(-x)))


def _softplus(x):
    return jnp.maximum(x, 0.0) + jnp.log1p(jnp.exp(-jnp.abs(x)))


def _rms(x):
    return x * lax.rsqrt(jnp.mean(x * x, axis=-1, keepdims=True) + EPS)


def _split3(x):
    a = x.astype(BF16)
    r = x - a.astype(F32)
    b = r.astype(BF16)
    c = (r - b.astype(F32)).astype(BF16)
    return a, b, c


def _dot_sel_rhs(x, sel):
    return sum(jnp.dot(p, sel, preferred_element_type=F32) for p in _split3(x))


def _dot_sel_lhs(sel, x):
    return sum(jnp.dot(sel, p, preferred_element_type=F32) for p in _split3(x))


def _dot_nt(a, b):
    return lax.dot_general(a, b, (((1,), (1,)), ((), ())), preferred_element_type=F32)


def _dot_tn(a, b):
    return lax.dot_general(a, b, (((0,), (0,)), ((), ())), preferred_element_type=F32)


def _ada_kernel(c_ref, w_ref, b_ref, o_ref):
    s = _silu(c_ref[...]).astype(BF16)
    o_ref[...] = jnp.dot(s, w_ref[...].astype(BF16), preferred_element_type=F32) + b_ref[...]


def _ada(c_all, w_ada, b_ada):
    rows = c_all.shape[0]
    n = w_ada.shape[1]
    return pl.pallas_call(
        _ada_kernel,
        grid=(n // D_MODEL,),
        in_specs=[
            pl.BlockSpec((rows, D_MODEL), lambda j: (0, 0)),
            pl.BlockSpec((D_MODEL, D_MODEL), lambda j: (0, j)),
            pl.BlockSpec((1, D_MODEL), lambda j: (0, j)),
        ],
        out_specs=pl.BlockSpec((rows, D_MODEL), lambda j: (0, j)),
        out_shape=jax.ShapeDtypeStruct((rows, n), F32),
        compiler_params=_cparams("arbitrary"),
        name="ada_mod",
    )(c_all, w_ada, b_ada.reshape(1, n))


def _inproj_kernel(queries_on_lanes, x_ref, g_ref, sc_ref, sh_ref, w_ref, qg_ref, kg_ref,
                   q_o, k_o, v_o, kb_o, vb_o, qi_o, ki_o, kib_o, wi_o, zg_o, xbc_o, dt_o):
    x = x_ref[0]
    h = _rms(x) * g_ref[...]
    hb = (h * (1.0 + sc_ref[0]) + sh_ref[0]).astype(BF16)

    def sec(a, b):
        return jnp.dot(hb, w_ref[:, a:b], preferred_element_type=F32)

    zq = sec(OFF_Q, OFF_K)
    for i in range(N_HEADS):
        qn = _rms(zq[:, i * HEAD_DIM:(i + 1) * HEAD_DIM]) * qg_ref[...]
        if queries_on_lanes:
            q_o[0, i * HEAD_DIM:(i + 1) * HEAD_DIM, :] = qn.T.astype(BF16)
        else:
            q_o[0, :, i * HEAD_DIM:(i + 1) * HEAD_DIM] = qn.astype(BF16)
    zk = sec(OFF_K, OFF_V)
    for i in range(N_KV_HEADS):
        zh = zk[:, i * HEAD_DIM:(i + 1) * HEAD_DIM]
        kn = _rms(zh) * kg_ref[...]
        k_o[0, :, i, :] = kn
        kb_o[0, :, i * HEAD_DIM:(i + 1) * HEAD_DIM] = kn.astype(BF16)
    zv = sec(OFF_V, OFF_QI)
    for i in range(N_KV_HEADS):
        v_o[0, :, i, :] = zv[:, i * HEAD_DIM:(i + 1) * HEAD_DIM]
    zqi = sec(OFF_QI, OFF_KI)
    zwi = sec(OFF_WI, OFF_ZG)
    if queries_on_lanes:
        vb_o[0] = zv.T.astype(BF16)
        qi_o[0] = zqi.T.astype(BF16)
        wi_o[0] = zwi.T[:IDX_HEADS, :]
    else:
        vb_o[0] = zv.astype(BF16)
        qi_o[0] = zqi.astype(BF16)
        wi_o[0] = zwi[:, :IDX_HEADS]
    zki = sec(OFF_KI, OFF_WI)[:, :IDX_DIM]
    ki_o[0] = zki
    kib_o[0] = zki.astype(BF16)
    zg_o[0] = sec(OFF_ZG, OFF_XBC)
    xbc_o[0] = sec(OFF_XBC, OFF_DT)
    dt_o[0] = sec(OFF_DT, N_PACK)


def _inproj(x, norm_g, sc, sh, w_pack, qg, kg, rows_per_mod, queries_on_lanes):
    g_, r_, _ = x.shape
    tm = min(ROW_TILE, r_)
    nt = r_ // tm
    mod_rows = 1 if rows_per_mod == 1 else tm
    mod_map = (lambda b, i: (b, 0, 0)) if rows_per_mod == 1 else (lambda b, i: (b, i, 0))
    outs = [(D_ATT, BF16, True), (D_KV, F32, False), (D_KV, F32, False), (D_KV, BF16, False), (D_KV, BF16, True),
            (IDX_HEADS * IDX_DIM, BF16, True), (IDX_DIM, F32, False), (IDX_DIM, BF16, False), (IDX_HEADS, F32, True),
            (D_SSM, F32, False), (CONV_DIM, F32, False), (LANES, F32, False)]
    out_specs, out_shape = [], []
    for n_out, (w, dt, tr) in enumerate(outs):
        if n_out in (1, 2):
            out_specs.append(pl.BlockSpec((1, tm, N_KV_HEADS, HEAD_DIM), lambda b, i: (b, i, 0, 0)))
            out_shape.append(jax.ShapeDtypeStruct((g_, r_, N_KV_HEADS, HEAD_DIM), dt))
        elif tr and queries_on_lanes:
            out_specs.append(pl.BlockSpec((1, w, tm), lambda b, i: (b, 0, i)))
            out_shape.append(jax.ShapeDtypeStruct((g_, w, r_), dt))
        else:
            out_specs.append(pl.BlockSpec((1, tm, w), lambda b, i: (b, i, 0)))
            out_shape.append(jax.ShapeDtypeStruct((g_, r_, w), dt))
    return pl.pallas_call(
        functools.partial(_inproj_kernel, queries_on_lanes),
        grid=(g_, nt),
        in_specs=[
            pl.BlockSpec((1, tm, D_MODEL), lambda b, i: (b, i, 0)),
            _resident((1, D_MODEL)),
            pl.BlockSpec((1, mod_rows, D_MODEL), mod_map),
            pl.BlockSpec((1, mod_rows, D_MODEL), mod_map),
            _resident((D_MODEL, N_PACK)),
            _resident((1, HEAD_DIM)),
            _resident((1, HEAD_DIM)),
        ],
        out_specs=out_specs,
        out_shape=out_shape,
        compiler_params=_cparams("parallel", "parallel"),
        name="in_proj",
    )(x, norm_g.reshape(1, D_MODEL), sc, sh, w_pack, qg.reshape(1, HEAD_DIM), kg.reshape(1, HEAD_DIM))


def _ssd_prompt_kernel(xbc_ref, dt_ref, zg_ref, cw_ref, cb_ref, dtb_ref, alog_ref, dsk_ref, ng_ref,
                       ex_ref, ext_ref, y_o, st_o, xe_scr, st_scr):
    c = pl.program_id(1)
    q_ = SSD_CHUNK
    halo = SUBLANES

    @pl.when(c == 0)
    def _():
        xe_scr[0:halo, :] = jnp.zeros((halo, CONV_DIM), F32)
        st_scr[...] = jnp.zeros_like(st_scr)

    xe_scr[halo:halo + q_, :] = xbc_ref[0]
    xe = xe_scr[...]
    taps = cw_ref[0:1, :] * xe
    for j in range(1, SSM_CONV):
        taps = cw_ref[j:j + 1, :] * xe + pltpu.roll(taps, 1, axis=0)
    conv = cb_ref[...] + taps[halo:halo + q_, :]
    xe_scr[0:halo, :] = xe_scr[q_:q_ + halo, :]
    xc = _silu(conv)
    xs = xc[:, :D_SSM]
    bm = xc[:, D_SSM:D_SSM + SSM_GROUPS * SSM_STATE]
    cm = xc[:, D_SSM + SSM_GROUPS * SSM_STATE:]

    dt = _softplus(dt_ref[0] + dtb_ref[...])
    da = dt * (-jnp.exp(alog_ref[...]))
    row = lax.broadcasted_iota(jnp.int32, (q_, q_), 0)
    col = lax.broadcasted_iota(jnp.int32, (q_, q_), 1)
    causal = row >= col
    tri = jnp.where(causal, 1.0, 0.0).astype(BF16)
    acs = _dot_sel_lhs(tri, da)
    acs_t = acs.T
    acs_last = acs[q_ - 1:q_, :]
    ex = ex_ref[...]
    dt_x = _dot_sel_rhs(dt, ex)
    eacs_x = _dot_sel_rhs(jnp.exp(acs), ex)
    dte_x = _dot_sel_rhs(jnp.exp(acs_last - acs), ex)
    cdec = jnp.exp(jnp.broadcast_to(acs_t[:, q_ - 1:q_], (LANES, SSM_STATE)))
    cdec_rows = _dot_sel_lhs(ext_ref[...], cdec)

    xdt = xs * dt_x
    xdtd = xdt * dte_x
    gw = SSM_HEADS // SSM_GROUPS * SSM_HEAD_DIM
    gated = []
    for g in range(SSM_GROUPS):
        bg = bm[:, g * SSM_STATE:(g + 1) * SSM_STATE].astype(BF16)
        cg = cm[:, g * SSM_STATE:(g + 1) * SSM_STATE].astype(BF16)
        cb = _dot_nt(cg, bg)
        yd = []
        for hl in range(SSM_HEADS // SSM_GROUPS):
            h = g * (SSM_HEADS // SSM_GROUPS) + hl
            seg = acs[:, h:h + 1] - acs_t[h:h + 1, :]
            lmat = jnp.exp(jnp.where(causal, seg, -jnp.inf))
            yd.append(jnp.dot((cb * lmat).astype(BF16),
                              xdt[:, h * SSM_HEAD_DIM:(h + 1) * SSM_HEAD_DIM].astype(BF16),
                              preferred_element_type=F32))
        y_diag = jnp.concatenate(yd, axis=-1)
        st_old = st_scr[g * gw:(g + 1) * gw, :]
        y_off = _dot_nt(cg, st_old.astype(BF16)) * eacs_x[:, g * gw:(g + 1) * gw]
        st_new = _dot_tn(xdtd[:, g * gw:(g + 1) * gw].astype(BF16), bg)
        st_scr[g * gw:(g + 1) * gw, :] = st_old * cdec_rows[g * gw:(g + 1) * gw, :] + st_new
        yg = y_diag + y_off + dsk_ref[:, g * gw:(g + 1) * gw] * xs[:, g * gw:(g + 1) * gw]
        gated.append(yg * _silu(zg_ref[0, :, g * gw:(g + 1) * gw]))

    y_o[0] = (_rms(jnp.concatenate(gated, axis=-1)) * ng_ref[...]).astype(y_o.dtype)

    @pl.when(c == pl.num_programs(1) - 1)
    def _():
        st_o[0] = st_scr[...]


def _ssd_prompt(xbc, dt, zg, conv_w, conv_b, dtb_pad, alog_pad, dsk_x, norm_g, ex, ext):
    b_, s_, _ = xbc.shape
    nc = s_ // SSD_CHUNK
    return pl.pallas_call(
        _ssd_prompt_kernel,
        grid=(b_, nc),
        in_specs=[
            pl.BlockSpec((1, SSD_CHUNK, CONV_DIM), lambda b, c: (b, c, 0)),
            pl.BlockSpec((1, SSD_CHUNK, LANES), lambda b, c: (b, c, 0)),
            pl.BlockSpec((1, SSD_CHUNK, D_SSM), lambda b, c: (b, c, 0)),
            _resident((SSM_CONV, CONV_DIM)),
            _resident((1, CONV_DIM)),
            _resident((1, LANES)),
            _resident((1, LANES)),
            _resident((1, D_SSM)),
            _resident((1, D_SSM)),
            _resident((LANES, D_SSM)),
            _resident((D_SSM, LANES)),
        ],
        out_specs=[
            pl.BlockSpec((1, SSD_CHUNK, D_SSM), lambda b, c: (b, c, 0)),
            pl.BlockSpec((1, D_SSM, SSM_STATE), lambda b, c: (b, 0, 0)),
        ],
        out_shape=[
            jax.ShapeDtypeStruct((b_, s_, D_SSM), F32),
            jax.ShapeDtypeStruct((b_, D_SSM, SSM_STATE), F32),
        ],
        scratch_shapes=[
            pltpu.VMEM((SSD_CHUNK + SUBLANES, CONV_DIM), F32),
            pltpu.VMEM((D_SSM, SSM_STATE), F32),
        ],
        compiler_params=_cparams("parallel", "arbitrary"),
        name="ssd_prompt",
    )(xbc, dt, zg, conv_w, conv_b, dtb_pad, alog_pad, dsk_x, norm_g, ex, ext)


def _f32_key(x):
    b = lax.bitcast_convert_type(x, jnp.int32)
    return b ^ ((b >> 31) & 0x7FFFFFFF)


def _key_f32(k):
    return lax.bitcast_convert_type(k ^ ((k >> 31) & 0x7FFFFFFF), F32)


def _fold(x, rows, op):
    return op(x.reshape(x.shape[0] // rows, rows, x.shape[1]), axis=0)


def _chunk_trips(fn, lo, hi, carry, widths):
    assert widths[-1] == 1
    start = lo
    for w in widths:
        trips = (hi - start) // w
        carry = lax.fori_loop(0, trips, lambda t, c, s=start, w=w: fn(s + t * w, c, w), carry)
        start = start + trips * w
    return carry


def _kth_largest(scan, k_row, take_all, mn=None, mx=None):
    def count_ge(x):
        return scan(lambda t: jnp.where(t >= x, 1.0, 0.0), jnp.sum, 0.0)

    if mx is None:
        mx = scan(lambda t: t, jnp.max, -jnp.inf)
        mn = scan(lambda t: jnp.where(t == -jnp.inf, jnp.inf, t), jnp.min, jnp.inf)
    above = mx + jnp.maximum(jnp.abs(mx) * 2.0 ** -20, 1e-30)
    state = (mn, above, mn, jnp.where(take_all, 1, 0).astype(jnp.int32))

    def value_step(_, s):
        lo, hi, thr, done = s
        mid = 0.5 * lo + 0.5 * hi
        inside = jnp.where(mid > lo, jnp.where(mid < hi, 1, 0), 0)
        cnt = count_ge(mid)
        ge = cnt >= k_row
        finished = jnp.where(cnt == k_row, inside, 0) * (1 - done)
        lo = jnp.where(inside == 1, jnp.where(ge, mid, lo), lo)
        hi = jnp.where(inside == 1, jnp.where(ge, hi, mid), hi)
        return lo, hi, jnp.where(finished == 1, mid, thr), done + finished

    def snap_step(s):
        lo, hi, thr, done = s
        cand = scan(lambda t: jnp.where(t < hi, t, -jnp.inf), jnp.max, -jnp.inf)
        ge = count_ge(cand) >= k_row
        finished = jnp.where(ge, 1, 0) * (1 - done)
        return lo, jnp.where(ge, hi, cand), jnp.where(finished == 1, cand, thr), done + finished

    def key_step(s):
        lo, hi, thr, done = s
        klo, khi = _f32_key(lo), _f32_key(hi)
        mid_k = (klo >> 1) + (khi >> 1) + (klo & khi & 1)
        mid = _key_f32(mid_k)
        cnt = count_ge(mid)
        ge = cnt >= k_row
        collapsed = mid_k == klo
        finished = jnp.where(collapsed, 1, jnp.where(cnt == k_row, 1, 0)) * (1 - done)
        thr = jnp.where(finished == 1, jnp.where(collapsed, lo, mid), thr)
        return jnp.where(ge, mid, lo), jnp.where(ge, hi, mid), thr, done + finished

    state = lax.fori_loop(0, VALUE_STEPS, value_step, state)

    def refine(s):
        it = s[4]
        st = snap_step(s[:4])
        st = lax.cond(it >= SNAP_ONLY_TRIPS, key_step, lambda x: x, st)
        return (*st, it + 1)

    return lax.while_loop(lambda s: jnp.logical_and(jnp.min(s[3]) == 0, s[4] < SNAP_ONLY_TRIPS + 40), refine,
                          (*state, jnp.int32(0)))[2]


def _dsa_prompt_kernel(topk, qi_ref, wi_ref, ki_ref, q_ref, k_ref, v_ref, tz_ref, o_ref,
                       sc_scr, s_scr, acc_scr):
    qb = pl.program_id(1)
    tq = q_ref.shape[2]
    ch = ATT_CHUNK
    n_ch = qb + 1

    def chunk_at(j):
        return pl.ds(pl.multiple_of(j * ch, ch), ch)

    qi_all = jnp.concatenate([qi_ref[0, h * IDX_DIM:(h + 1) * IDX_DIM, :] for h in range(IDX_HEADS)], axis=1)
    w_rows = [wi_ref[0, h:h + 1, :] for h in range(IDX_HEADS)]

    def score_chunk(j, diagonal):
        d = jnp.dot(ki_ref[0, chunk_at(j), :], qi_all, preferred_element_type=F32)
        acc = w_rows[0] * jnp.maximum(d[:, 0:tq], 0.0)
        for h in range(1, IDX_HEADS):
            acc = acc + w_rows[h] * jnp.maximum(d[:, h * tq:(h + 1) * tq], 0.0)
        lo_src = acc
        if diagonal:
            causal = (lax.broadcasted_iota(jnp.int32, (ch, tq), 0) <= lax.broadcasted_iota(jnp.int32, (ch, tq), 1))
            acc, lo_src = jnp.where(causal, acc, -jnp.inf), jnp.where(causal, acc, jnp.inf)
        sc_scr[chunk_at(j), :] = acc
        return _fold(acc, SCAN_FOLD, jnp.max), _fold(lo_src, SCAN_FOLD, jnp.min)

    def score_chunks(jj, c, width):
        hi_acc, lo_acc = c
        for j in [jj * width + i for i in range(width)]:
            hi_j, lo_j = score_chunk(j, False)
            hi_acc, lo_acc = jnp.maximum(hi_acc, hi_j), jnp.minimum(lo_acc, lo_j)
        return hi_acc, lo_acc

    ext = (jnp.full((SCAN_FOLD, tq), -jnp.inf, F32), jnp.full((SCAN_FOLD, tq), jnp.inf, F32))
    ext = lax.fori_loop(0, qb // 2, functools.partial(score_chunks, width=2), ext)
    ext = lax.fori_loop(qb // 2 * 2, qb, functools.partial(score_chunks, width=1), ext)
    hi_d, lo_d = score_chunk(qb, True)
    score_max = jnp.max(jnp.maximum(ext[0], hi_d), axis=0, keepdims=True)
    score_min = jnp.min(jnp.minimum(ext[1], lo_d), axis=0, keepdims=True)

    def chunks_reduce(fn, op, init):
        def body(j, acc):
            rows = pl.ds(pl.multiple_of(j * (2 * ch), 2 * ch), 2 * ch)
            return op(jnp.stack([acc, _fold(fn(sc_scr[rows, :]), SCAN_FOLD, op)]), axis=0)
        return op(lax.fori_loop(0, (n_ch + 1) // 2, body, jnp.full((SCAN_FOLD, tq), init, F32)), axis=0, keepdims=True)

    @pl.when(n_ch % 2 == 1)
    def _():
        sc_scr[chunk_at(n_ch), :] = jnp.full((ch, tq), -jnp.inf, F32)

    def count(cmp, thr):
        return chunks_reduce(lambda t: jnp.where(cmp(t, thr), 1.0, 0.0), jnp.sum, 0.0)

    count_ge = functools.partial(count, lambda t, x: t >= x)

    n_valid = qb * tq + lax.broadcasted_iota(jnp.int32, (1, tq), 1) + 1
    k_row = jnp.minimum(n_valid, topk).astype(F32)
    thr = _kth_largest(chunks_reduce, k_row, n_valid <= topk, score_min, score_max)

    surplus = jnp.max(count_ge(thr) - k_row)

    @pl.when(surplus <= 0.0)
    def _():
        def select_chunk(j, c):
            sc_scr[chunk_at(j), :] = jnp.where(sc_scr[chunk_at(j), :] >= thr, 0.0, NEG_BIG)
            return c
        lax.fori_loop(0, n_ch, select_chunk, 0)

    @pl.when(surplus > 0.0)
    def _():
        need = k_row - count(lambda t, x: t > x, thr)
        lower = jnp.where(lax.broadcasted_iota(jnp.int32, (ch, ch), 1) <= lax.broadcasted_iota(jnp.int32, (ch, ch), 0),
                          1.0, 0.0).astype(BF16)
        ones_r = jnp.ones((SUBLANES, ch), BF16)

        def select_chunk(j, run):
            t = sc_scr[chunk_at(j), :]
            eq = t == thr
            eqb = jnp.where(eq, 1.0, 0.0).astype(BF16)
            rank = jnp.dot(lower, eqb, preferred_element_type=F32) + run
            tie_bias = jnp.where(rank <= need, 0.0, NEG_BIG)
            sc_scr[chunk_at(j), :] = jnp.where(t > thr, 0.0, jnp.where(eq, tie_bias, NEG_BIG))
            return run + jnp.dot(ones_r, eqb, preferred_element_type=F32)[0:1, :]

        lax.fori_loop(0, n_ch, select_chunk, jnp.zeros((1, tq), F32))

    scale = HEAD_DIM ** -0.5 * LOG2_E
    cols = KV_REP * tq
    n_far = jnp.maximum(qb - 1, 0)
    for gp in range(N_KV_HEADS // 2):
        groups = (2 * gp, 2 * gp + 1)
        q_cols = [jnp.concatenate([q_ref[0, (g * KV_REP + r) * HEAD_DIM:(g * KV_REP + r + 1) * HEAD_DIM, :]
                                   for r in range(KV_REP)], axis=1) for g in groups]

        def logits_chunks(j0, mx, width, near):
            mx = list(mx)
            for j in [j0 + i for i in range(width)]:
                selb = sc_scr[chunk_at(j), :]
                for gi, g in enumerate(groups):
                    s = jnp.dot(k_ref[0, chunk_at(j), g * HEAD_DIM:(g + 1) * HEAD_DIM], q_cols[gi],
                                preferred_element_type=F32) * scale
                    if near:
                        bias = jnp.concatenate([tz_ref[qb - j, g * KV_REP + r] + selb for r in range(KV_REP)], axis=1)
                    else:
                        bias = jnp.concatenate([selb] * KV_REP, axis=1)
                    s = s + bias
                    s_scr[gi, chunk_at(j), :] = s
                    mx[gi] = jnp.maximum(mx[gi], _fold(s, ATT_FOLD, jnp.max))
            return tuple(mx)

        mx = (jnp.full((ATT_FOLD, cols), NEG_BIG, F32),) * 2
        mx = _chunk_trips(functools.partial(logits_chunks, near=False), 0, n_far, mx, TRIP_WIDTHS)
        mx = _chunk_trips(functools.partial(logits_chunks, near=True), n_far, n_ch, mx, (1,))
        m_row = [jnp.max(m, axis=0, keepdims=True) for m in mx]
        acc_scr[...] = jnp.zeros_like(acc_scr)

        def weights_chunks(j0, l8, width):
            l8 = list(l8)
            for j in [j0 + i for i in range(width)]:
                for gi, g in enumerate(groups):
                    p = jnp.exp2(s_scr[gi, chunk_at(j), :] - m_row[gi])
                    l8[gi] = l8[gi] + _fold(p, ATT_FOLD, jnp.sum)
                    acc_scr[gi] += jnp.dot(v_ref[0, g * HEAD_DIM:(g + 1) * HEAD_DIM, chunk_at(j)], p.astype(BF16),
                                           preferred_element_type=F32)
            return tuple(l8)

        l8 = (jnp.zeros((ATT_FOLD, cols), F32),) * 2
        l8 = _chunk_trips(weights_chunks, 0, n_ch, l8, TRIP_WIDTHS)
        for gi, g in enumerate(groups):
            out = acc_scr[gi] * (1.0 / jnp.sum(l8[gi], axis=0, keepdims=True))
            for r in range(KV_REP):
                h = g * KV_REP + r
                o_ref[0, :, h * HEAD_DIM:(h + 1) * HEAD_DIM] = out[:, r * tq:(r + 1) * tq].T.astype(o_ref.dtype)


def _dsa_prompt(qi_t, wi_t, ki_b, q_t, k_b, v_t, tz, topk):
    b_, s_, _ = k_b.shape
    tq = min(Q_BLOCK, s_)
    assert tq == ATT_CHUNK and s_ % (2 * tq) == 0
    once = pl.Buffered(1)
    q_tile = lambda rows: pl.BlockSpec((1, rows, tq), lambda b, i: (b, 0, i))
    return pl.pallas_call(
        functools.partial(_dsa_prompt_kernel, topk),
        grid=(b_, s_ // tq),
        in_specs=[
            q_tile(IDX_HEADS * IDX_DIM),
            q_tile(IDX_HEADS),
            pl.BlockSpec((1, s_, IDX_DIM), lambda b, i: (b, 0, 0), pipeline_mode=once),
            q_tile(D_ATT),
            pl.BlockSpec((1, s_, D_KV), lambda b, i: (b, 0, 0), pipeline_mode=once),
            pl.BlockSpec((1, D_KV, s_), lambda b, i: (b, 0, 0), pipeline_mode=once),
            _resident(tz.shape),
        ],
        out_specs=pl.BlockSpec((1, tq, D_ATT), lambda b, i: (b, i, 0)),
        out_shape=jax.ShapeDtypeStruct((b_, s_, D_ATT), BF16),
        scratch_shapes=[
            pltpu.VMEM((s_, tq), F32),
            pltpu.VMEM((2, s_, KV_REP * tq), F32),
            pltpu.VMEM((2, HEAD_DIM, KV_REP * tq), F32),
        ],
        compiler_params=_cparams("parallel", "arbitrary"),
        name="dsa_prompt",
    )(qi_t, wi_t, ki_b, q_t, k_b, v_t, tz)


def _mix_and_norm(att_ref, ssd_ref, x_ref, g1_ref, sc2_ref, sh2_ref, n2_ref, wo_ref):
    o = (jnp.dot(att_ref[0].astype(BF16), wo_ref[:D_ATT, :], preferred_element_type=F32)
         + jnp.dot(ssd_ref[0].astype(BF16), wo_ref[D_ATT:, :], preferred_element_type=F32))
    x1 = x_ref[0] + g1_ref[0] * o
    h2 = ((_rms(x1) * n2_ref[...]) * (1.0 + sc2_ref[0]) + sh2_ref[0]).astype(BF16)
    return x1, h2


def _ffn_prompt_kernel(att_ref, ssd_ref, x_ref, g1_ref, sc2_ref, sh2_ref, g2_ref, n2_ref, wo_ref, wu_ref,
                       cw_ref, cb_ref, wd_ref, y_o, tail_o, a_scr, ug_scr, uv_scr, carry_scr):
    i = pl.program_id(1)
    tm = x_ref.shape[1]
    halo = SUBLANES

    @pl.when(i == 0)
    def _():
        carry_scr[...] = jnp.zeros_like(carry_scr)

    x1, h2 = _mix_and_norm(att_ref, ssd_ref, x_ref, g1_ref, sc2_ref, sh2_ref, n2_ref, wo_ref)

    def conv_cols(scr, c0):
        u = jnp.dot(h2, wu_ref[:, c0:c0 + FF_CHUNK], preferred_element_type=F32)
        scr[0:halo, :] = carry_scr[:, c0:c0 + FF_CHUNK]
        scr[halo:halo + tm, :] = u
        carry_scr[:, c0:c0 + FF_CHUNK] = u[tm - halo:, :]
        return cb_ref[:, c0:c0 + FF_CHUNK] + sum(
            cw_ref[j:j + 1, c0:c0 + FF_CHUNK] * scr[halo - (FFN_CONV - 1) + j:halo - (FFN_CONV - 1) + j + tm, :]
            for j in range(FFN_CONV))

    for jc in range(D_FF // FF_CHUNK):
        gate = conv_cols(ug_scr.at[jc % 2], jc * FF_CHUNK)
        val = conv_cols(uv_scr.at[jc % 2], D_FF + jc * FF_CHUNK)
        a_scr[:, jc * FF_CHUNK:(jc + 1) * FF_CHUNK] = (_silu(gate) * val).astype(BF16)

    y_o[0] = x1 + g2_ref[0] * jnp.dot(a_scr[...], wd_ref[...], preferred_element_type=F32)
    tail_o[0] = carry_scr[...]


def _ffn_prompt(att, ssd, x, g1, sc2, sh2, g2, norm2_g, wo_b, wu_b, cw, cb, wd_b):
    b_, s_, _ = x.shape
    tm = min(ROW_TILE, s_)
    nt = s_ // tm
    row = lambda w: pl.BlockSpec((1, tm, w), lambda b, i: (b, i, 0))
    mod = pl.BlockSpec((1, 1, D_MODEL), lambda b, i: (b, 0, 0))
    return pl.pallas_call(
        _ffn_prompt_kernel,
        grid=(b_, nt),
        in_specs=[row(D_ATT), row(D_SSM), row(D_MODEL), mod, mod, mod, mod, _resident((1, D_MODEL)),
                  _resident((D_MIX, D_MODEL)), _resident((D_MODEL, 2 * D_FF)), _resident((FFN_CONV, 2 * D_FF)),
                  _resident((1, 2 * D_FF)), _resident((D_FF, D_MODEL))],
        out_specs=[row(D_MODEL), pl.BlockSpec((1, SUBLANES, 2 * D_FF), lambda b, i: (b, 0, 0))],
        out_shape=[jax.ShapeDtypeStruct((b_, s_, D_MODEL), F32),
                   jax.ShapeDtypeStruct((b_, SUBLANES, 2 * D_FF), F32)],
        scratch_shapes=[pltpu.VMEM((tm, D_FF), BF16), pltpu.VMEM((2, tm + SUBLANES, FF_CHUNK), F32),
                        pltpu.VMEM((2, tm + SUBLANES, FF_CHUNK), F32), pltpu.VMEM((SUBLANES, 2 * D_FF), F32)],
        compiler_params=_cparams("parallel", "arbitrary"),
        name="ffn_prompt",
    )(att, ssd, x, g1, sc2, sh2, g2, norm2_g.reshape(1, D_MODEL), wo_b, wu_b, cw, cb, wd_b)


def _ffn_sample_kernel(att_ref, ssd_ref, x_ref, g1_ref, sc2_ref, sh2_ref, g2_ref, n2_ref, wo_ref, wu_ref,
                       cw_ref, cb_ref, wd_ref, h0_ref, h1_ref, y_o, u_o, acc_scr):
    x1, h2 = _mix_and_norm(att_ref, ssd_ref, x_ref, g1_ref, sc2_ref, sh2_ref, n2_ref, wo_ref)
    acc_scr[...] = jnp.zeros_like(acc_scr)

    def conv_cols(c0):
        u = jnp.dot(h2, wu_ref[:, c0:c0 + FF_CHUNK], preferred_element_type=F32)
        u_o[:, c0:c0 + FF_CHUNK] = u
        sl = slice(c0, c0 + FF_CHUNK)
        return (cb_ref[:, sl] + cw_ref[0:1, sl] * h0_ref[:, sl] + cw_ref[1:2, sl] * h1_ref[:, sl]
                + cw_ref[2:3, sl] * u)

    for jc in range(D_FF // FF_CHUNK):
        gate = conv_cols(jc * FF_CHUNK)
        val = conv_cols(D_FF + jc * FF_CHUNK)
        a = (_silu(gate) * val).astype(BF16)
        acc_scr[...] += jnp.dot(a, wd_ref[jc * FF_CHUNK:(jc + 1) * FF_CHUNK, :], preferred_element_type=F32)

    y_o[0] = x1 + g2_ref[0] * acc_scr[...]


def _ffn_sample(att, ssd, x, g1, sc2, sh2, g2, norm2_g, wo_b, wu_b, cw, cb, wd_b, hist0, hist1):
    n = x.shape[1]
    full = lambda *shape: pl.BlockSpec(shape, lambda: (0,) * len(shape))
    r3 = lambda w: full(1, n, w)
    return pl.pallas_call(
        _ffn_sample_kernel,
        in_specs=[r3(D_ATT), r3(D_SSM), r3(D_MODEL), r3(D_MODEL), r3(D_MODEL), r3(D_MODEL), r3(D_MODEL),
                  full(1, D_MODEL), full(D_MIX, D_MODEL), full(D_MODEL, 2 * D_FF), full(FFN_CONV, 2 * D_FF),
                  full(1, 2 * D_FF), full(D_FF, D_MODEL), full(n, 2 * D_FF), full(n, 2 * D_FF)],
        out_specs=[r3(D_MODEL), full(n, 2 * D_FF)],
        out_shape=[jax.ShapeDtypeStruct((1, n, D_MODEL), F32), jax.ShapeDtypeStruct((n, 2 * D_FF), F32)],
        scratch_shapes=[pltpu.VMEM((n, D_MODEL), F32)],
        compiler_params=pltpu.CompilerParams(vmem_limit_bytes=VMEM_LIMIT),
        name="ffn_sample",
    )(att, ssd, x, g1, sc2, sh2, g2, norm2_g.reshape(1, D_MODEL), wo_b, wu_b, cw, cb, wd_b, hist0, hist1)


def _ssd_sample_kernel(xbc_ref, h0_ref, h1_ref, h2_ref, dt_ref, zg_ref, cw_ref, cb_ref, dtb_ref, alog_ref,
                       dsk_ref, ng_ref, ex_ref, st_ref, y_o, st_o):
    nb = xbc_ref.shape[0]
    conv = (cb_ref[...] + cw_ref[0:1, :] * h0_ref[...] + cw_ref[1:2, :] * h1_ref[...]
            + cw_ref[2:3, :] * h2_ref[...] + cw_ref[3:4, :] * xbc_ref[...])
    xc = _silu(conv)
    xs = xc[:, :D_SSM]
    bm = xc[:, D_SSM:D_SSM + SSM_GROUPS * SSM_STATE]
    cm = xc[:, D_SSM + SSM_GROUPS * SSM_STATE:]
    dt = _softplus(dt_ref[...] + dtb_ref[...])
    dec = jnp.exp(dt * (-jnp.exp(alog_ref[...])))
    ex = ex_ref[...]
    xdt = xs * _dot_sel_rhs(dt, ex)
    dec_x = _dot_sel_rhs(dec, ex)
    gw = SSM_HEADS // SSM_GROUPS * SSM_HEAD_DIM
    ones_n = jnp.ones((SUBLANES, SSM_STATE), BF16)
    row_id = lax.broadcasted_iota(jnp.int32, (SUBLANES, 1), 0)

    def stack(rows):
        out = jnp.zeros((SUBLANES, rows[0].shape[1]), F32)
        for r, v in enumerate(rows):
            out = jnp.where(row_id == r, v.astype(F32), out)
        return out.astype(BF16)

    ys = []
    for i in range(nb):
        yrow = []
        for g in range(SSM_GROUPS):
            hs = st_ref[i, g * gw:(g + 1) * gw, :]
            d0, d1, d2 = _split3(dec_x[i:i + 1, g * gw:(g + 1) * gw])
            x0, x1, _ = _split3(xdt[i:i + 1, g * gw:(g + 1) * gw])
            b0, b1, _ = _split3(bm[i:i + 1, g * SSM_STATE:(g + 1) * SSM_STATE])
            c0, c1, _ = _split3(cm[i:i + 1, g * SSM_STATE:(g + 1) * SSM_STATE])
            dec_b = _dot_tn(stack([d0, d1, d2]), ones_n)
            upd = _dot_tn(stack([x0, x0, x1, x1]), stack([b0, b1, b0, b1]))
            hn = hs * dec_b + upd
            st_o[i, g * gw:(g + 1) * gw, :] = hn
            h0, h1, _ = _split3(hn)
            c_rows = stack([c0, c1])
            yg = _dot_nt(c_rows, h0) + _dot_nt(c_rows, h1)
            yrow.append(yg[0:1, :] + yg[1:2, :])
        ys.append(jnp.concatenate(yrow, axis=-1))
    y = jnp.concatenate(ys, axis=0) + dsk_ref[...] * xs
    y = y * _silu(zg_ref[...])
    y_o[...] = _rms(y) * ng_ref[...]


def _ssd_sample(xbc, hist, dt, zg, conv_w, conv_b, dtb_pad, alog_pad, dsk_x, norm_g, ex, state):
    n = xbc.shape[0]
    nb = SUBLANES
    rows = lambda w: pl.BlockSpec((nb, w), lambda i: (i, 0))
    st = pl.BlockSpec((nb, D_SSM, SSM_STATE), lambda i: (i, 0, 0))
    return pl.pallas_call(
        _ssd_sample_kernel,
        grid=(n // nb,),
        in_specs=[rows(CONV_DIM), rows(CONV_DIM), rows(CONV_DIM), rows(CONV_DIM), rows(LANES), rows(D_SSM),
                  _resident((SSM_CONV, CONV_DIM)), _resident((1, CONV_DIM)), _resident((1, LANES)),
                  _resident((1, LANES)), _resident((1, D_SSM)), _resident((1, D_SSM)), _resident((LANES, D_SSM)), st],
        out_specs=[rows(D_SSM), st],
        out_shape=[jax.ShapeDtypeStruct((n, D_SSM), F32), jax.ShapeDtypeStruct((n, D_SSM, SSM_STATE), F32)],
        compiler_params=_cparams("parallel"),
        name="ssd_sample",
    )(xbc, hist[0], hist[1], hist[2], dt, zg, conv_w, conv_b, dtb_pad, alog_pad, dsk_x, norm_g, ex, state)


def _idx_sample_kernel(pt_ref, qi_ref, wi_ref, kin_ref, pool_ref, o_ref, kbuf, sem):
    b = pl.program_id(0)
    nb = pl.num_programs(0)
    n_pages = kbuf.shape[1]
    page = kbuf.shape[3]

    def page_copy(seq, slot, j):
        return pltpu.make_async_copy(pool_ref.at[pt_ref[seq, j]], kbuf.at[slot, j], sem.at[slot])

    def fetch(seq, slot):
        def body(j, c):
            page_copy(seq, slot, j).start()
            return c
        lax.fori_loop(0, n_pages, body, 0, unroll=8)

    @pl.when(b == 0)
    def _():
        fetch(0, 0)

    @pl.when(b + 1 < nb)
    def _():
        fetch(b + 1, (b + 1) % 2)

    slot = b % 2

    def wait_body(j, c):
        page_copy(b, slot, j).wait()
        return c
    lax.fori_loop(0, n_pages, wait_body, 0)

    qi = qi_ref[0]
    w = wi_ref[0]

    span = PAGES_PER_DOT * page

    def page_scores(j, c):
        keys = jnp.concatenate([kbuf[slot, PAGES_PER_DOT * j + i] for i in range(PAGES_PER_DOT)], axis=1)
        d = jnp.dot(qi, keys.astype(BF16), preferred_element_type=F32)
        o_ref[0, :, pl.ds(pl.multiple_of(j * span, span), span)] = jnp.sum(w * jnp.maximum(d, 0.0), axis=0, keepdims=True)
        return c

    lax.fori_loop(0, n_pages // PAGES_PER_DOT, page_scores, 0, unroll=True)
    knew = kin_ref[0].astype(BF16).astype(F32)
    dn = jnp.sum(qi.astype(F32) * knew, axis=-1, keepdims=True)
    s_new = jnp.sum(w * jnp.maximum(dn, 0.0), axis=0, keepdims=True)
    lane = lax.broadcasted_iota(jnp.int32, (1, SCORE_PAD), 1)
    o_ref[0, :, n_pages * page:] = jnp.where(lane == 0, s_new, -jnp.inf)


def _idx_sample(page_table, qi3, wi3, ki_new3, pool_t):
    n, n_pages = page_table.shape
    page = pool_t.shape[2]
    assert n_pages % (8 * PAGES_PER_DOT) == 0
    past = n_pages * page
    grid_spec = pltpu.PrefetchScalarGridSpec(
        num_scalar_prefetch=1,
        grid=(n,),
        in_specs=[
            pl.BlockSpec((1, IDX_HEADS, IDX_DIM), lambda b, pt: (b, 0, 0)),
            pl.BlockSpec((1, IDX_HEADS, 1), lambda b, pt: (b, 0, 0)),
            pl.BlockSpec((1, 1, IDX_DIM), lambda b, pt: (b, 0, 0)),
            pl.BlockSpec(memory_space=pl.ANY),
        ],
        out_specs=pl.BlockSpec((1, 1, past + SCORE_PAD), lambda b, pt: (b, 0, 0)),
        scratch_shapes=[pltpu.VMEM((2, n_pages, IDX_DIM, page), F32), pltpu.SemaphoreType.DMA((2,))],
    )
    return pl.pallas_call(
        _idx_sample_kernel,
        grid_spec=grid_spec,
        out_shape=jax.ShapeDtypeStruct((n, 1, past + SCORE_PAD), F32),
        compiler_params=_cparams("arbitrary"),
        name="idx_sample",
    )(page_table, qi3, wi3, ki_new3, pool_t)


def _att_sample_kernel(q_ref, k_ref, v_ref, knew_ref, vnew_ref, isnew_ref, bias_ref, o_ref):
    nb = q_ref.shape[0]
    scale = HEAD_DIM ** -0.5
    head_group = lax.broadcasted_iota(jnp.int32, (N_HEADS, 1), 0) // KV_REP
    for i in range(nb):
        q = q_ref[i].astype(BF16)
        bias = bias_ref[i]
        is_new = isnew_ref[i] > 0.5
        out = jnp.zeros((N_HEADS, HEAD_DIM), F32)
        for g in range(N_KV_HEADS):
            kg = jnp.where(is_new, knew_ref[i, g:g + 1, :], k_ref[i, :, g, :]).astype(BF16)
            vg = jnp.where(is_new, vnew_ref[i, g:g + 1, :], v_ref[i, :, g, :]).astype(BF16)
            s = _dot_nt(q, kg) * scale + bias
            m = jnp.max(s, axis=-1, keepdims=True)
            p = jnp.exp(s - m)
            p = p * (1.0 / jnp.sum(p, axis=-1, keepdims=True))
            og = jnp.dot(p.astype(BF16), vg, preferred_element_type=F32)
            out = jnp.where(head_group == g, og, out)
        o_ref[i] = out


def _att_sample(q3, k_sel, v_sel, k_new, v_new, is_new, bias):
    n, kk = k_sel.shape[:2]
    nb = SUBLANES // 2
    rows4 = pl.BlockSpec((nb, kk, N_KV_HEADS, HEAD_DIM), lambda i: (i, 0, 0, 0))
    new3 = pl.BlockSpec((nb, N_KV_HEADS, HEAD_DIM), lambda i: (i, 0, 0))
    return pl.pallas_call(
        _att_sample_kernel,
        grid=(n // nb,),
        in_specs=[
            pl.BlockSpec((nb, N_HEADS, HEAD_DIM), lambda i: (i, 0, 0)),
            rows4, rows4, new3, new3,
            pl.BlockSpec((nb, kk, 1), lambda i: (i, 0, 0)),
            pl.BlockSpec((nb, N_HEADS, kk), lambda i: (i, 0, 0)),
        ],
        out_specs=pl.BlockSpec((nb, N_HEADS, HEAD_DIM), lambda i: (i, 0, 0)),
        out_shape=jax.ShapeDtypeStruct((n, N_HEADS, HEAD_DIM), F32),
        compiler_params=_cparams("parallel"),
        name="att_sample",
    )(q3, k_sel, v_sel, k_new, v_new, is_new, bias)


def _sel_sample_kernel(k_keep, n_valid, sc_ref, o_ref):
    rows, n = sc_ref.shape
    ch = ATT_CHUNK
    span = 2 * ch

    def scan(fn, op, init):
        def body(j, acc):
            r = pl.ds(pl.multiple_of(j * span, span), span)
            return op(jnp.stack([acc, _fold(fn(sc_ref[r, :]), SCAN_FOLD, op)]), axis=0)
        return op(lax.fori_loop(0, rows // span, body, jnp.full((SCAN_FOLD, n), init, F32)), axis=0, keepdims=True)

    k_row = jnp.full((1, n), float(k_keep), F32)
    thr = _kth_largest(scan, k_row, jnp.full((1, n), n_valid <= k_keep))
    need = k_row - scan(lambda t: jnp.where(t > thr, 1.0, 0.0), jnp.sum, 0.0)
    lower = jnp.where(lax.broadcasted_iota(jnp.int32, (ch, ch), 1) <= lax.broadcasted_iota(jnp.int32, (ch, ch), 0),
                      1.0, 0.0).astype(BF16)
    ones_r = jnp.ones((SUBLANES, ch), BF16)
    key_local = lax.broadcasted_iota(jnp.int32, (ch, n), 0).astype(F32)
    slot_iota = lax.broadcasted_iota(jnp.int32, (k_keep, n), 0).astype(F32)
    none = float(rows)

    def chunk_body(c, carry):
        run, slot, out = carry
        t = sc_ref[pl.ds(pl.multiple_of(c * ch, ch), ch), :]
        eq = t == thr
        eqb = jnp.where(eq, 1.0, 0.0).astype(BF16)
        rank = jnp.dot(lower, eqb, preferred_element_type=F32) + run
        cand = jnp.where(t > thr, key_local, jnp.where(eq, jnp.where(rank <= need, key_local, none), none))
        left = _fold(jnp.where(cand < none, 1.0, 0.0), SUBLANES, jnp.sum).sum(axis=0, keepdims=True)
        base = jnp.asarray(c * ch, F32)

        def extract(_, s):
            cand, out, slot = s
            cur = jnp.min(cand, axis=0, keepdims=True)
            has = cur < none
            out = jnp.where(slot_iota == slot, jnp.where(has, cur + base, out), out)
            return jnp.where(cand == cur, none, cand), out, slot + jnp.where(has, 1.0, 0.0)

        _, out, slot = lax.fori_loop(0, jnp.max(left).astype(jnp.int32), extract, (cand, out, slot))
        return run + jnp.dot(ones_r, eqb, preferred_element_type=F32)[0:1, :], slot, out

    zeros_row = jnp.zeros((1, n), F32)
    _, _, out = lax.fori_loop(0, rows // ch, chunk_body, (zeros_row, zeros_row, jnp.zeros((k_keep, n), F32)))
    o_ref[...] = out.astype(jnp.int32)


def _sel_sample(scores_t, k_keep, n_valid):
    rows, n = scores_t.shape
    assert rows % (2 * ATT_CHUNK) == 0
    return pl.pallas_call(
        functools.partial(_sel_sample_kernel, k_keep, n_valid),
        in_specs=[pl.BlockSpec((rows, n), lambda: (0, 0))],
        out_specs=pl.BlockSpec((k_keep, n), lambda: (0, 0)),
        out_shape=jax.ShapeDtypeStruct((k_keep, n), jnp.int32),
        compiler_params=pltpu.CompilerParams(vmem_limit_bytes=VMEM_LIMIT),
        name="sel_sample",
    )(scores_t)


def _pack_w_in(w_in):
    offs = np.cumsum((0,) + IN_SPLITS)
    parts = [w_in[:, offs[i]:offs[i + 1]] for i in range(len(IN_SPLITS))]
    pad = lambda a: jnp.pad(a, ((0, 0), (0, LANES - a.shape[1])))
    q, k, v, qi, ki, wi, zg, xbc, dt = parts
    return jnp.concatenate([q, k, v, qi, pad(ki), pad(wi), zg, xbc, pad(dt)], axis=1).astype(BF16)


def _t5_bucket(dist):
    dist = jnp.maximum(dist, 0)
    max_exact = N_BUCKETS // 2
    d = jnp.maximum(dist, 1).astype(F32)
    large = max_exact + (jnp.log(d / max_exact) / math.log(MAX_DISTANCE / max_exact) * (N_BUCKETS - max_exact)).astype(jnp.int32)
    large = jnp.minimum(large, N_BUCKETS - 1)
    return jnp.where(dist < max_exact, dist, large)


def _rel_bias_tables(rel_bias, tq):
    assert ATT_CHUNK == tq and tq >= MAX_DISTANCE
    n = 2 * tq
    idx = jnp.arange(n, dtype=jnp.int32)
    query_minus_key = jnp.where(idx < tq, idx, idx - n)
    far = rel_bias[_t5_bucket(jnp.int32(MAX_DISTANCE))]
    tabs = []
    for d0 in (0, tq):
        f = (rel_bias[_t5_bucket(d0 + query_minus_key)] - far[None, :]) * LOG2_E
        flat = jnp.tile(f.T, (1, ATT_CHUNK))
        tabs.append(flat[:, :ATT_CHUNK * (n - 1)].reshape(-1, ATT_CHUNK, n - 1)[:, :, :tq])
    return jnp.stack(tabs)


def _head_expand():
    h = jnp.arange(LANES, dtype=jnp.int32)[:, None]
    c = jnp.arange(D_SSM, dtype=jnp.int32)[None, :]
    ex = (c // SSM_HEAD_DIM == h).astype(BF16)
    return ex, ex.T


def _pad_lanes(v):
    return jnp.pad(v.astype(F32), (0, LANES - v.shape[0])).reshape(1, LANES)


def kernel(x_prompt, x_sample, cache_k, cache_v, cache_kidx, state_ssm, state_conv_ssd, state_conv_ffn, page_table,
           c_prompt, c_sample, rel_bias, norm1_g, w_ada, b_ada, w_in, q_norm_g, k_norm_g, conv_ssd_w, conv_ssd_b,
           dt_bias, a_log, d_skip, ssd_norm_g, w_out, norm2_g, w_up, conv_ffn_w, conv_ffn_b, w_down):
    depth = w_in.shape[0]
    assert depth == 1
    bp, s_, _ = x_prompt.shape
    ns = x_sample.shape[0]
    assert x_sample.shape[1] == 1
    lyr = 0
    n_pages = page_table.shape[1]
    page = cache_k.shape[2]
    past = n_pages * page
    topk_p = min(TOPK_MAX, s_ // TOPK_DIV)
    topk_s = min(TOPK_MAX, (past + 1) // TOPK_DIV)

    w_pack = _pack_w_in(w_in[lyr])
    wo_b = w_out[lyr].astype(BF16)
    wu_b = w_up[lyr].astype(BF16)
    wd_b = w_down[lyr].astype(BF16)
    ex, ext = _head_expand()
    dtb_pad = _pad_lanes(dt_bias[lyr])
    alog_pad = _pad_lanes(a_log[lyr])
    dsk_x = jnp.repeat(d_skip[lyr].astype(F32), SSM_HEAD_DIM).reshape(1, D_SSM)
    ssd_g = ssd_norm_g[lyr].reshape(1, D_SSM)
    cw_ssd, cb_ssd = conv_ssd_w[lyr], conv_ssd_b[lyr].reshape(1, CONV_DIM)
    cw_ffn, cb_ffn = conv_ffn_w[lyr], conv_ffn_b[lyr].reshape(1, 2 * D_FF)
    tz = _rel_bias_tables(rel_bias.astype(F32), min(Q_BLOCK, s_))

    n_c = bp + ns
    c_all = jnp.pad(jnp.concatenate([c_prompt, c_sample], axis=0), ((0, (-n_c) % SUBLANES), (0, 0)))
    mod = _ada(c_all, w_ada[lyr], b_ada[lyr])
    mods_p = [mod[:bp, i * D_MODEL:(i + 1) * D_MODEL].reshape(bp, 1, D_MODEL) for i in range(6)]
    mods_s = [mod[bp:bp + ns, i * D_MODEL:(i + 1) * D_MODEL].reshape(1, ns, D_MODEL) for i in range(6)]

    sh1, sc1, g1, sh2, sc2, g2 = mods_p
    (q_t, k_f, v_f, k_b, v_t, qi_t, ki_f, ki_b, wi_t, zg, xbc, dt) = _inproj(
        x_prompt, norm1_g[lyr], sc1, sh1, w_pack, q_norm_g[lyr], k_norm_g[lyr], rows_per_mod=1, queries_on_lanes=True)
    att_p = _dsa_prompt(qi_t, wi_t, ki_b, q_t, k_b, v_t, tz, topk_p)
    ssd_p, ssm_p = _ssd_prompt(xbc, dt, zg, cw_ssd, cb_ssd, dtb_pad, alog_pad, dsk_x, ssd_g, ex, ext)
    y_p, tail_p = _ffn_prompt(att_p, ssd_p, x_prompt, g1, sc2, sh2, g2, norm2_g[lyr], wo_b, wu_b, cw_ffn, cb_ffn, wd_b)

    k_p = k_f.reshape(1, bp, s_, N_KV_HEADS, HEAD_DIM)
    v_p = v_f.reshape(1, bp, s_, N_KV_HEADS, HEAD_DIM)
    kidx_p = ki_f.reshape(1, bp, s_, IDX_DIM)
    ssm_p = ssm_p.reshape(1, bp, SSM_HEADS, SSM_HEAD_DIM, SSM_STATE)
    cssd_p = xbc[:, s_ - (SSM_CONV - 1):, :].reshape(1, bp, SSM_CONV - 1, CONV_DIM)
    cffn_p = tail_p[:, SUBLANES - (FFN_CONV - 1):, :].reshape(1, bp, FFN_CONV - 1, 2 * D_FF)

    sh1, sc1, g1, sh2, sc2, g2 = mods_s
    xs3 = x_sample.reshape(1, ns, D_MODEL)
    (q_b, k_f, v_f, _, _, qi_b, ki_f, _, wi_f, zg, xbc, dt) = _inproj(
        xs3, norm1_g[lyr], sc1, sh1, w_pack, q_norm_g[lyr], k_norm_g[lyr], rows_per_mod=ns, queries_on_lanes=False)
    k_new, v_new, ki_new = k_f[0], v_f[0], ki_f[0]

    scores = _idx_sample(page_table, qi_b[0].reshape(ns, IDX_HEADS, IDX_DIM), wi_f[0].reshape(ns, IDX_HEADS, 1),
                         ki_new.reshape(ns, 1, IDX_DIM), jnp.swapaxes(cache_kidx[lyr], 1, 2))[:, 0, :]
    idx = _sel_sample(scores.T, topk_s, past + 1).T
    is_new = (idx >= past).astype(F32)[:, :, None]
    pidx = jnp.minimum(idx, past - 1)
    page_hit = (pidx // page)[:, :, None] == jnp.arange(n_pages, dtype=jnp.int32)
    phys = jnp.sum(jnp.where(page_hit, page_table[:, None, :], 0), axis=-1)
    off = pidx % page
    bucket_hit = _t5_bucket(past - idx)[:, None, :, None] == jnp.arange(N_BUCKETS, dtype=jnp.int32)
    bias = jnp.sum(jnp.where(bucket_hit, rel_bias.astype(F32).T[None, :, None, :], 0.0), axis=-1)
    att_s = _att_sample(q_b[0].astype(F32).reshape(ns, N_HEADS, HEAD_DIM), cache_k[lyr][phys, off], cache_v[lyr][phys, off],
                        k_new.reshape(ns, N_KV_HEADS, HEAD_DIM), v_new.reshape(ns, N_KV_HEADS, HEAD_DIM), is_new,
                        bias).reshape(1, ns, D_ATT)

    hist_ssd = state_conv_ssd[lyr]
    ssd_s, ssm_s = _ssd_sample(xbc[0], [hist_ssd[:, j, :] for j in range(SSM_CONV - 1)], dt[0], zg[0], cw_ssd, cb_ssd,
                               dtb_pad, alog_pad, dsk_x, ssd_g, ex,
                               state_ssm[lyr].reshape(ns, D_SSM, SSM_STATE))
    hist_ffn = state_conv_ffn[lyr]
    y_s, u_s = _ffn_sample(att_s, ssd_s.reshape(1, ns, D_SSM), xs3, g1, sc2, sh2, g2, norm2_g[lyr], wo_b, wu_b,
                           cw_ffn, cb_ffn, wd_b, hist_ffn[:, 0, :], hist_ffn[:, 1, :])

    k_s = k_new.reshape(1, ns, 1, N_KV_HEADS, HEAD_DIM)
    v_s = v_new.reshape(1, ns, 1, N_KV_HEADS, HEAD_DIM)
    kidx_s = ki_new.reshape(1, ns, 1, IDX_DIM)
    ssm_s = ssm_s.reshape(1, ns, SSM_HEADS, SSM_HEAD_DIM, SSM_STATE)
    cssd_s = jnp.concatenate([hist_ssd[:, 1:, :], xbc[0][:, None, :]], axis=1)[None]
    cffn_s = jnp.stack([hist_ffn[:, 1, :], u_s], axis=1)[None]

    return (y_p, y_s.reshape(ns, 1, D_MODEL), k_p, v_p, kidx_p, ssm_p, cssd_p, cffn_p,
            k_s, v_s, kidx_s, ssm_s, cssd_s, cffn_s)
```

```python
import functools
import math

import jax
import jax.numpy as jnp
import numpy as np
from jax import lax
from jax.experimental import pallas as pl
from jax.experimental.pallas import tpu as pltpu

F32 = jnp.float32
BF16 = jnp.bfloat16

D_MODEL = 1024
N_HEADS = 8
N_KV_HEADS = 4
HEAD_DIM = 128
KV_REP = N_HEADS // N_KV_HEADS
IDX_HEADS = 8
IDX_DIM = 64
TOPK_MAX = 256
TOPK_DIV = 4
N_BUCKETS = 32
MAX_DISTANCE = 128
SSM_HEADS = 16
SSM_HEAD_DIM = 64
SSM_GROUPS = 4
SSM_STATE = 128
SSM_CONV = 4
SSD_CHUNK = 128
D_ATT = N_HEADS * HEAD_DIM
D_KV = N_KV_HEADS * HEAD_DIM
D_SSM = SSM_HEADS * SSM_HEAD_DIM
D_MIX = D_ATT + D_SSM
CONV_DIM = D_SSM + 2 * SSM_GROUPS * SSM_STATE
D_FF = 2816
FFN_CONV = 3
EPS = 1e-6
IN_SPLITS = (D_ATT, D_KV, D_KV, IDX_HEADS * IDX_DIM, IDX_DIM, IDX_HEADS, D_SSM, CONV_DIM, SSM_HEADS)

LANES = 128
SUBLANES = 8
VMEM_LIMIT = 52 * 1024 * 1024

OFF_Q = 0
OFF_K = OFF_Q + D_ATT
OFF_V = OFF_K + D_KV
OFF_QI = OFF_V + D_KV
OFF_KI = OFF_QI + IDX_HEADS * IDX_DIM
OFF_WI = OFF_KI + LANES
OFF_ZG = OFF_WI + LANES
OFF_XBC = OFF_ZG + D_SSM
OFF_DT = OFF_XBC + CONV_DIM
N_PACK = OFF_DT + LANES

NEG_BIG = -1e30
LOG2_E = math.log2(math.e)
ATT_CHUNK = 256
FF_CHUNK = 256
ROW_TILE = 256
Q_BLOCK = 256
PAGES_PER_DOT = 4
SCORE_PAD = 2 * ATT_CHUNK
VALUE_STEPS = 14
SNAP_ONLY_TRIPS = 4
SCAN_FOLD = 64
ATT_FOLD = 16
TRIP_WIDTHS = (4, 2, 1)


def _cparams(*sem):
    return pltpu.CompilerParams(dimension_semantics=sem, vmem_limit_bytes=VMEM_LIMIT)


def _resident(shape):
    nd = len(shape)
    return pl.BlockSpec(shape, lambda *_: (0,) * nd, pipeline_mode=pl.Buffered(1))


def _silu(x):
    return x * (1.0 / (1.0 + jnp.exp(-x)))


def _softplus(x):
    return jnp.maximum(x, 0.0) + jnp.log1p(jnp.exp(-jnp.abs(x)))


def _rms(x):
    return x * lax.rsqrt(jnp.mean(x * x, axis=-1, keepdims=True) + EPS)


def _split3(x):
    a = x.astype(BF16)
    r = x - a.astype(F32)
    b = r.astype(BF16)
    c = (r - b.astype(F32)).astype(BF16)
    return a, b, c


def _dot_sel_rhs(x, sel):
    return sum(jnp.dot(p, sel, preferred_element_type=F32) for p in _split3(x))


def _dot_sel_lhs(sel, x):
    return sum(jnp.dot(sel, p, preferred_element_type=F32) for p in _split3(x))


def _dot_nt(a, b):
    return lax.dot_general(a, b, (((1,), (1,)), ((), ())), preferred_element_type=F32)


def _dot_tn(a, b):
    return lax.dot_general(a, b, (((0,), (0,)), ((), ())), preferred_element_type=F32)


def _ada_kernel(c_ref, w_ref, b_ref, o_ref):
    s = _silu(c_ref[...]).astype(BF16)
    o_ref[...] = jnp.dot(s, w_ref[...].astype(BF16), preferred_element_type=F32) + b_ref[...]


def _ada(c_all, w_ada, b_ada):
    rows = c_all.shape[0]
    n = w_ada.shape[1]
    return pl.pallas_call(
        _ada_kernel,
        grid=(n // D_MODEL,),
        in_specs=[
            pl.BlockSpec((rows, D_MODEL), lambda j: (0, 0)),
            pl.BlockSpec((D_MODEL, D_MODEL), lambda j: (0, j)),
            pl.BlockSpec((1, D_MODEL), lambda j: (0, j)),
        ],
        out_specs=pl.BlockSpec((rows, D_MODEL), lambda j: (0, j)),
        out_shape=jax.ShapeDtypeStruct((rows, n), F32),
        compiler_params=_cparams("arbitrary"),
        name="ada_mod",
    )(c_all, w_ada, b_ada.reshape(1, n))


def _inproj_kernel(queries_on_lanes, x_ref, g_ref, sc_ref, sh_ref, w_ref, qg_ref, kg_ref,
                   q_o, k_o, v_o, kb_o, vb_o, qi_o, ki_o, kib_o, wi_o, zg_o, xbc_o, dt_o):
    x = x_ref[0]
    h = _rms(x) * g_ref[...]
    hb = (h * (1.0 + sc_ref[0]) + sh_ref[0]).astype(BF16)

    def sec(a, b):
        return jnp.dot(hb, w_ref[:, a:b], preferred_element_type=F32)

    zq = sec(OFF_Q, OFF_K)
    for i in range(N_HEADS):
        qn = _rms(zq[:, i * HEAD_DIM:(i + 1) * HEAD_DIM]) * qg_ref[...]
        if queries_on_lanes:
            q_o[0, i * HEAD_DIM:(i + 1) * HEAD_DIM, :] = qn.T.astype(BF16)
        else:
            q_o[0, :, i * HEAD_DIM:(i + 1) * HEAD_DIM] = qn.astype(BF16)
    zk = sec(OFF_K, OFF_V)
    for i in range(N_KV_HEADS):
        zh = zk[:, i * HEAD_DIM:(i + 1) * HEAD_DIM]
        kn = _rms(zh) * kg_ref[...]
        k_o[0, :, i, :] = kn
        kb_o[0, :, i * HEAD_DIM:(i + 1) * HEAD_DIM] = kn.astype(BF16)
    zv = sec(OFF_V, OFF_QI)
    for i in range(N_KV_HEADS):
        v_o[0, :, i, :] = zv[:, i * HEAD_DIM:(i + 1) * HEAD_DIM]
    zqi = sec(OFF_QI, OFF_KI)
    zwi = sec(OFF_WI, OFF_ZG)
    if queries_on_lanes:
        vb_o[0] = zv.T.astype(BF16)
        qi_o[0] = zqi.T.astype(BF16)
        wi_o[0] = zwi.T[:IDX_HEADS, :]
    else:
        vb_o[0] = zv.astype(BF16)
        qi_o[0] = zqi.astype(BF16)
        wi_o[0] = zwi[:, :IDX_HEADS]
    zki = sec(OFF_KI, OFF_WI)[:, :IDX_DIM]
    ki_o[0] = zki
    kib_o[0] = zki.astype(BF16)
    zg_o[0] = sec(OFF_ZG, OFF_XBC)
    xbc_o[0] = sec(OFF_XBC, OFF_DT)
    dt_o[0] = sec(OFF_DT, N_PACK)


def _inproj(x, norm_g, sc, sh, w_pack, qg, kg, rows_per_mod, queries_on_lanes):
    g_, r_, _ = x.shape
    tm = min(ROW_TILE, r_)
    nt = r_ // tm
    mod_rows = 1 if rows_per_mod == 1 else tm
    mod_map = (lambda b, i: (b, 0, 0)) if rows_per_mod == 1 else (lambda b, i: (b, i, 0))
    outs = [(D_ATT, BF16, True), (D_KV, F32, False), (D_KV, F32, False), (D_KV, BF16, False), (D_KV, BF16, True),
            (IDX_HEADS * IDX_DIM, BF16, True), (IDX_DIM, F32, False), (IDX_DIM, BF16, False), (IDX_HEADS, F32, True),
            (D_SSM, F32, False), (CONV_DIM, F32, False), (LANES, F32, False)]
    out_specs, out_shape = [], []
    for n_out, (w, dt, tr) in enumerate(outs):
        if n_out in (1, 2):
            out_specs.append(pl.BlockSpec((1, tm, N_KV_HEADS, HEAD_DIM), lambda b, i: (b, i, 0, 0)))
            out_shape.append(jax.ShapeDtypeStruct((g_, r_, N_KV_HEADS, HEAD_DIM), dt))
        elif tr and queries_on_lanes:
            out_specs.append(pl.BlockSpec((1, w, tm), lambda b, i: (b, 0, i)))
            out_shape.append(jax.ShapeDtypeStruct((g_, w, r_), dt))
        else:
            out_specs.append(pl.BlockSpec((1, tm, w), lambda b, i: (b, i, 0)))
            out_shape.append(jax.ShapeDtypeStruct((g_, r_, w), dt))
    return pl.pallas_call(
        functools.partial(_inproj_kernel, queries_on_lanes),
        grid=(g_, nt),
        in_specs=[
            pl.BlockSpec((1, tm, D_MODEL), lambda b, i: (b, i, 0)),
            _resident((1, D_MODEL)),
            pl.BlockSpec((1, mod_rows, D_MODEL), mod_map),
            pl.BlockSpec((1, mod_rows, D_MODEL), mod_map),
            _resident((D_MODEL, N_PACK)),
            _resident((1, HEAD_DIM)),
            _resident((1, HEAD_DIM)),
        ],
        out_specs=out_specs,
        out_shape=out_shape,
        compiler_params=_cparams("parallel", "parallel"),
        name="in_proj",
    )(x, norm_g.reshape(1, D_MODEL), sc, sh, w_pack, qg.reshape(1, HEAD_DIM), kg.reshape(1, HEAD_DIM))


def _ssd_prompt_kernel(xbc_ref, dt_ref, zg_ref, cw_ref, cb_ref, dtb_ref, alog_ref, dsk_ref, ng_ref,
                       ex_ref, ext_ref, y_o, st_o, xe_scr, st_scr):
    c = pl.program_id(1)
    q_ = SSD_CHUNK
    halo = SUBLANES

    @pl.when(c == 0)
    def _():
        xe_scr[0:halo, :] = jnp.zeros((halo, CONV_DIM), F32)
        st_scr[...] = jnp.zeros_like(st_scr)

    xe_scr[halo:halo + q_, :] = xbc_ref[0]
    xe = xe_scr[...]
    taps = cw_ref[0:1, :] * xe
    for j in range(1, SSM_CONV):
        taps = cw_ref[j:j + 1, :] * xe + pltpu.roll(taps, 1, axis=0)
    conv = cb_ref[...] + taps[halo:halo + q_, :]
    xe_scr[0:halo, :] = xe_scr[q_:q_ + halo, :]
    xc = _silu(conv)
    xs = xc[:, :D_SSM]
    bm = xc[:, D_SSM:D_SSM + SSM_GROUPS * SSM_STATE]
    cm = xc[:, D_SSM + SSM_GROUPS * SSM_STATE:]

    dt = _softplus(dt_ref[0] + dtb_ref[...])
    da = dt * (-jnp.exp(alog_ref[...]))
    row = lax.broadcasted_iota(jnp.int32, (q_, q_), 0)
    col = lax.broadcasted_iota(jnp.int32, (q_, q_), 1)
    causal = row >= col
    tri = jnp.where(causal, 1.0, 0.0).astype(BF16)
    acs = _dot_sel_lhs(tri, da)
    acs_t = acs.T
    acs_last = acs[q_ - 1:q_, :]
    ex = ex_ref[...]
    dt_x = _dot_sel_rhs(dt, ex)
    eacs_x = _dot_sel_rhs(jnp.exp(acs), ex)
    dte_x = _dot_sel_rhs(jnp.exp(acs_last - acs), ex)
    cdec = jnp.exp(jnp.broadcast_to(acs_t[:, q_ - 1:q_], (LANES, SSM_STATE)))
    cdec_rows = _dot_sel_lhs(ext_ref[...], cdec)

    xdt = xs * dt_x
    xdtd = xdt * dte_x
    gw = SSM_HEADS // SSM_GROUPS * SSM_HEAD_DIM
    gated = []
    for g in range(SSM_GROUPS):
        bg = bm[:, g * SSM_STATE:(g + 1) * SSM_STATE].astype(BF16)
        cg = cm[:, g * SSM_STATE:(g + 1) * SSM_STATE].astype(BF16)
        cb = _dot_nt(cg, bg)
        yd = []
        for hl in range(SSM_HEADS // SSM_GROUPS):
            h = g * (SSM_HEADS // SSM_GROUPS) + hl
            seg = acs[:, h:h + 1] - acs_t[h:h + 1, :]
            lmat = jnp.exp(jnp.where(causal, seg, -jnp.inf))
            yd.append(jnp.dot((cb * lmat).astype(BF16),
                              xdt[:, h * SSM_HEAD_DIM:(h + 1) * SSM_HEAD_DIM].astype(BF16),
                              preferred_element_type=F32))
        y_diag = jnp.concatenate(yd, axis=-1)
        st_old = st_scr[g * gw:(g + 1) * gw, :]
        y_off = _dot_nt(cg, st_old.astype(BF16)) * eacs_x[:, g * gw:(g + 1) * gw]
        st_new = _dot_tn(xdtd[:, g * gw:(g + 1) * gw].astype(BF16), bg)
        st_scr[g * gw:(g + 1) * gw, :] = st_old * cdec_rows[g * gw:(g + 1) * gw, :] + st_new
        yg = y_diag + y_off + dsk_ref[:, g * gw:(g + 1) * gw] * xs[:, g * gw:(g + 1) * gw]
        gated.append(yg * _silu(zg_ref[0, :, g * gw:(g + 1) * gw]))

    y_o[0] = (_rms(jnp.concatenate(gated, axis=-1)) * ng_ref[...]).astype(y_o.dtype)

    @pl.when(c == pl.num_programs(1) - 1)
    def _():
        st_o[0] = st_scr[...]


def _ssd_prompt(xbc, dt, zg, conv_w, conv_b, dtb_pad, alog_pad, dsk_x, norm_g, ex, ext):
    b_, s_, _ = xbc.shape
    nc = s_ // SSD_CHUNK
    return pl.pallas_call(
        _ssd_prompt_kernel,
        grid=(b_, nc),
        in_specs=[
            pl.BlockSpec((1, SSD_CHUNK, CONV_DIM), lambda b, c: (b, c, 0)),
            pl.BlockSpec((1, SSD_CHUNK, LANES), lambda b, c: (b, c, 0)),
            pl.BlockSpec((1, SSD_CHUNK, D_SSM), lambda b, c: (b, c, 0)),
            _resident((SSM_CONV, CONV_DIM)),
            _resident((1, CONV_DIM)),
            _resident((1, LANES)),
            _resident((1, LANES)),
            _resident((1, D_SSM)),
            _resident((1, D_SSM)),
            _resident((LANES, D_SSM)),
            _resident((D_SSM, LANES)),
        ],
        out_specs=[
            pl.BlockSpec((1, SSD_CHUNK, D_SSM), lambda b, c: (b, c, 0)),
            pl.BlockSpec((1, D_SSM, SSM_STATE), lambda b, c: (b, 0, 0)),
        ],
        out_shape=[
            jax.ShapeDtypeStruct((b_, s_, D_SSM), F32),
            jax.ShapeDtypeStruct((b_, D_SSM, SSM_STATE), F32),
        ],
        scratch_shapes=[
            pltpu.VMEM((SSD_CHUNK + SUBLANES, CONV_DIM), F32),
            pltpu.VMEM((D_SSM, SSM_STATE), F32),
        ],
        compiler_params=_cparams("parallel", "arbitrary"),
        name="ssd_prompt",
    )(xbc, dt, zg, conv_w, conv_b, dtb_pad, alog_pad, dsk_x, norm_g, ex, ext)


def _f32_key(x):
    b = lax.bitcast_convert_type(x, jnp.int32)
    return b ^ ((b >> 31) & 0x7FFFFFFF)


def _key_f32(k):
    return lax.bitcast_convert_type(k ^ ((k >> 31) & 0x7FFFFFFF), F32)


def _fold(x, rows, op):
    return op(x.reshape(x.shape[0] // rows, rows, x.shape[1]), axis=0)


def _chunk_trips(fn, lo, hi, carry, widths):
    assert widths[-1] == 1
    start = lo
    for w in widths:
        trips = (hi - start) // w
        carry = lax.fori_loop(0, trips, lambda t, c, s=start, w=w: fn(s + t * w, c, w), carry)
        start = start + trips * w
    return carry


def _kth_largest(scan, k_row, take_all, mn=None, mx=None):
    def count_ge(x):
        return scan(lambda t: jnp.where(t >= x, 1.0, 0.0), jnp.sum, 0.0)

    if mx is None:
        mx = scan(lambda t: t, jnp.max, -jnp.inf)
        mn = scan(lambda t: jnp.where(t == -jnp.inf, jnp.inf, t), jnp.min, jnp.inf)
    above = mx + jnp.maximum(jnp.abs(mx) * 2.0 ** -20, 1e-30)
    state = (mn, above, mn, jnp.where(take_all, 1, 0).astype(jnp.int32))

    def value_step(_, s):
        lo, hi, thr, done = s
        mid = 0.5 * lo + 0.5 * hi
        inside = jnp.where(mid > lo, jnp.where(mid < hi, 1, 0), 0)
        cnt = count_ge(mid)
        ge = cnt >= k_row
        finished = jnp.where(cnt == k_row, inside, 0) * (1 - done)
        lo = jnp.where(inside == 1, jnp.where(ge, mid, lo), lo)
        hi = jnp.where(inside == 1, jnp.where(ge, hi, mid), hi)
        return lo, hi, jnp.where(finished == 1, mid, thr), done + finished

    def snap_step(s):
        lo, hi, thr, done = s
        cand = scan(lambda t: jnp.where(t < hi, t, -jnp.inf), jnp.max, -jnp.inf)
        ge = count_ge(cand) >= k_row
        finished = jnp.where(ge, 1, 0) * (1 - done)
        return lo, jnp.where(ge, hi, cand), jnp.where(finished == 1, cand, thr), done + finished

    def key_step(s):
        lo, hi, thr, done = s
        klo, khi = _f32_key(lo), _f32_key(hi)
        mid_k = (klo >> 1) + (khi >> 1) + (klo & khi & 1)
        mid = _key_f32(mid_k)
        cnt = count_ge(mid)
        ge = cnt >= k_row
        collapsed = mid_k == klo
        finished = jnp.where(collapsed, 1, jnp.where(cnt == k_row, 1, 0)) * (1 - done)
        thr = jnp.where(finished == 1, jnp.where(collapsed, lo, mid), thr)
        return jnp.where(ge, mid, lo), jnp.where(ge, hi, mid), thr, done + finished

    state = lax.fori_loop(0, VALUE_STEPS, value_step, state)

    def refine(s):
        it = s[4]
        st = snap_step(s[:4])
        st = lax.cond(it >= SNAP_ONLY_TRIPS, key_step, lambda x: x, st)
        return (*st, it + 1)

    return lax.while_loop(lambda s: jnp.logical_and(jnp.min(s[3]) == 0, s[4] < SNAP_ONLY_TRIPS + 40), refine,
                          (*state, jnp.int32(0)))[2]


def _dsa_prompt_kernel(topk, qi_ref, wi_ref, ki_ref, q_ref, k_ref, v_ref, tz_ref, o_ref,
                       sc_scr, s_scr, acc_scr):
    qb = pl.program_id(1)
    tq = q_ref.shape[2]
    ch = ATT_CHUNK
    n_ch = qb + 1

    def chunk_at(j):
        return pl.ds(pl.multiple_of(j * ch, ch), ch)

    qi_all = jnp.concatenate([qi_ref[0, h * IDX_DIM:(h + 1) * IDX_DIM, :] for h in range(IDX_HEADS)], axis=1)
    w_rows = [wi_ref[0, h:h + 1, :] for h in range(IDX_HEADS)]

    def score_chunk(j, diagonal):
        d = jnp.dot(ki_ref[0, chunk_at(j), :], qi_all, preferred_element_type=F32)
        acc = w_rows[0] * jnp.maximum(d[:, 0:tq], 0.0)
        for h in range(1, IDX_HEADS):
            acc = acc + w_rows[h] * jnp.maximum(d[:, h * tq:(h + 1) * tq], 0.0)
        lo_src = acc
        if diagonal:
            causal = (lax.broadcasted_iota(jnp.int32, (ch, tq), 0) <= lax.broadcasted_iota(jnp.int32, (ch, tq), 1))
            acc, lo_src = jnp.where(causal, acc, -jnp.inf), jnp.where(causal, acc, jnp.inf)
        sc_scr[chunk_at(j), :] = acc
        return _fold(acc, SCAN_FOLD, jnp.max), _fold(lo_src, SCAN_FOLD, jnp.min)

    def score_chunks(jj, c, width):
        hi_acc, lo_acc = c
        for j in [jj * width + i for i in range(width)]:
            hi_j, lo_j = score_chunk(j, False)
            hi_acc, lo_acc = jnp.maximum(hi_acc, hi_j), jnp.minimum(lo_acc, lo_j)
        return hi_acc, lo_acc

    ext = (jnp.full((SCAN_FOLD, tq), -jnp.inf, F32), jnp.full((SCAN_FOLD, tq), jnp.inf, F32))
    ext = lax.fori_loop(0, qb // 2, functools.partial(score_chunks, width=2), ext)
    ext = lax.fori_loop(qb // 2 * 2, qb, functools.partial(score_chunks, width=1), ext)
    hi_d, lo_d = score_chunk(qb, True)
    score_max = jnp.max(jnp.maximum(ext[0], hi_d), axis=0, keepdims=True)
    score_min = jnp.min(jnp.minimum(ext[1], lo_d), axis=0, keepdims=True)

    def chunks_reduce(fn, op, init):
        def body(j, acc):
            rows = pl.ds(pl.multiple_of(j * (2 * ch), 2 * ch), 2 * ch)
            return op(jnp.stack([acc, _fold(fn(sc_scr[rows, :]), SCAN_FOLD, op)]), axis=0)
        return op(lax.fori_loop(0, (n_ch + 1) // 2, body, jnp.full((SCAN_FOLD, tq), init, F32)), axis=0, keepdims=True)

    @pl.when(n_ch % 2 == 1)
    def _():
        sc_scr[chunk_at(n_ch), :] = jnp.full((ch, tq), -jnp.inf, F32)

    def count(cmp, thr):
        return chunks_reduce(lambda t: jnp.where(cmp(t, thr), 1.0, 0.0), jnp.sum, 0.0)

    count_ge = functools.partial(count, lambda t, x: t >= x)

    n_valid = qb * tq + lax.broadcasted_iota(jnp.int32, (1, tq), 1) + 1
    k_row = jnp.minimum(n_valid, topk).astype(F32)
    thr = _kth_largest(chunks_reduce, k_row, n_valid <= topk, score_min, score_max)

    surplus = jnp.max(count_ge(thr) - k_row)

    @pl.when(surplus <= 0.0)
    def _():
        def select_chunk(j, c):
            sc_scr[chunk_at(j), :] = jnp.where(sc_scr[chunk_at(j), :] >= thr, 0.0, NEG_BIG)
            return c
        lax.fori_loop(0, n_ch, select_chunk, 0)

    @pl.when(surplus > 0.0)
    def _():
        need = k_row - count(lambda t, x: t > x, thr)
        lower = jnp.where(lax.broadcasted_iota(jnp.int32, (ch, ch), 1) <= lax.broadcasted_iota(jnp.int32, (ch, ch), 0),
                          1.0, 0.0).astype(BF16)
        ones_r = jnp.ones((SUBLANES, ch), BF16)

        def select_chunk(j, run):
            t = sc_scr[chunk_at(j), :]
            eq = t == thr
            eqb = jnp.where(eq, 1.0, 0.0).astype(BF16)
            rank = jnp.dot(lower, eqb, preferred_element_type=F32) + run
            tie_bias = jnp.where(rank <= need, 0.0, NEG_BIG)
            sc_scr[chunk_at(j), :] = jnp.where(t > thr, 0.0, jnp.where(eq, tie_bias, NEG_BIG))
            return run + jnp.dot(ones_r, eqb, preferred_element_type=F32)[0:1, :]

        lax.fori_loop(0, n_ch, select_chunk, jnp.zeros((1, tq), F32))

    scale = HEAD_DIM ** -0.5 * LOG2_E
    cols = KV_REP * tq
    n_far = jnp.maximum(qb - 1, 0)
    for gp in range(N_KV_HEADS // 2):
        groups = (2 * gp, 2 * gp + 1)
        q_cols = [jnp.concatenate([q_ref[0, (g * KV_REP + r) * HEAD_DIM:(g * KV_REP + r + 1) * HEAD_DIM, :]
                                   for r in range(KV_REP)], axis=1) for g in groups]

        def logits_chunks(j0, mx, width, near):
            mx = list(mx)
            for j in [j0 + i for i in range(width)]:
                selb = sc_scr[chunk_at(j), :]
                for gi, g in enumerate(groups):
                    s = jnp.dot(k_ref[0, chunk_at(j), g * HEAD_DIM:(g + 1) * HEAD_DIM], q_cols[gi],
                                preferred_element_type=F32) * scale
                    if near:
                        bias = jnp.concatenate([tz_ref[qb - j, g * KV_REP + r] + selb for r in range(KV_REP)], axis=1)
                    else:
                        bias = jnp.concatenate([selb] * KV_REP, axis=1)
                    s = s + bias
                    s_scr[gi, chunk_at(j), :] = s
                    mx[gi] = jnp.maximum(mx[gi], _fold(s, ATT_FOLD, jnp.max))
            return tuple(mx)

        mx = (jnp.full((ATT_FOLD, cols), NEG_BIG, F32),) * 2
        mx = _chunk_trips(functools.partial(logits_chunks, near=False), 0, n_far, mx, TRIP_WIDTHS)
        mx = _chunk_trips(functools.partial(logits_chunks, near=True), n_far, n_ch, mx, (1,))
        m_row = [jnp.max(m, axis=0, keepdims=True) for m in mx]
        acc_scr[...] = jnp.zeros_like(acc_scr)

        def weights_chunks(j0, l8, width):
            l8 = list(l8)
            for j in [j0 + i for i in range(width)]:
                for gi, g in enumerate(groups):
                    p = jnp.exp2(s_scr[gi, chunk_at(j), :] - m_row[gi])
                    l8[gi] = l8[gi] + _fold(p, ATT_FOLD, jnp.sum)
                    acc_scr[gi] += jnp.dot(v_ref[0, g * HEAD_DIM:(g + 1) * HEAD_DIM, chunk_at(j)], p.astype(BF16),
                                           preferred_element_type=F32)
            return tuple(l8)

        l8 = (jnp.zeros((ATT_FOLD, cols), F32),) * 2
        l8 = _chunk_trips(weights_chunks, 0, n_ch, l8, TRIP_WIDTHS)
        for gi, g in enumerate(groups):
            out = acc_scr[gi] * (1.0 / jnp.sum(l8[gi], axis=0, keepdims=True))
            for r in range(KV_REP):
                h = g * KV_REP + r
                o_ref[0, :, h * HEAD_DIM:(h + 1) * HEAD_DIM] = out[:, r * tq:(r + 1) * tq].T.astype(o_ref.dtype)


def _dsa_prompt(qi_t, wi_t, ki_b, q_t, k_b, v_t, tz, topk):
    b_, s_, _ = k_b.shape
    tq = min(Q_BLOCK, s_)
    assert tq == ATT_CHUNK and s_ % (2 * tq) == 0
    once = pl.Buffered(1)
    q_tile = lambda rows: pl.BlockSpec((1, rows, tq), lambda b, i: (b, 0, i))
    return pl.pallas_call(
        functools.partial(_dsa_prompt_kernel, topk),
        grid=(b_, s_ // tq),
        in_specs=[
            q_tile(IDX_HEADS * IDX_DIM),
            q_tile(IDX_HEADS),
            pl.BlockSpec((1, s_, IDX_DIM), lambda b, i: (b, 0, 0), pipeline_mode=once),
            q_tile(D_ATT),
            pl.BlockSpec((1, s_, D_KV), lambda b, i: (b, 0, 0), pipeline_mode=once),
            pl.BlockSpec((1, D_KV, s_), lambda b, i: (b, 0, 0), pipeline_mode=once),
            _resident(tz.shape),
        ],
        out_specs=pl.BlockSpec((1, tq, D_ATT), lambda b, i: (b, i, 0)),
        out_shape=jax.ShapeDtypeStruct((b_, s_, D_ATT), BF16),
        scratch_shapes=[
            pltpu.VMEM((s_, tq), F32),
            pltpu.VMEM((2, s_, KV_REP * tq), F32),
            pltpu.VMEM((2, HEAD_DIM, KV_REP * tq), F32),
        ],
        compiler_params=_cparams("parallel", "arbitrary"),
        name="dsa_prompt",
    )(qi_t, wi_t, ki_b, q_t, k_b, v_t, tz)


def _mix_and_norm(att_ref, ssd_ref, x_ref, g1_ref, sc2_ref, sh2_ref, n2_ref, wo_ref):
    o = (jnp.dot(att_ref[0].astype(BF16), wo_ref[:D_ATT, :], preferred_element_type=F32)
         + jnp.dot(ssd_ref[0].astype(BF16), wo_ref[D_ATT:, :], preferred_element_type=F32))
    x1 = x_ref[0] + g1_ref[0] * o
    h2 = ((_rms(x1) * n2_ref[...]) * (1.0 + sc2_ref[0]) + sh2_ref[0]).astype(BF16)
    return x1, h2


def _ffn_prompt_kernel(att_ref, ssd_ref, x_ref, g1_ref, sc2_ref, sh2_ref, g2_ref, n2_ref, wo_ref, wu_ref,
                       cw_ref, cb_ref, wd_ref, y_o, tail_o, a_scr, ug_scr, uv_scr, carry_scr):
    i = pl.program_id(1)
    tm = x_ref.shape[1]
    halo = SUBLANES

    @pl.when(i == 0)
    def _():
        carry_scr[...] = jnp.zeros_like(carry_scr)

    x1, h2 = _mix_and_norm(att_ref, ssd_ref, x_ref, g1_ref, sc2_ref, sh2_ref, n2_ref, wo_ref)

    def conv_cols(scr, c0):
        u = jnp.dot(h2, wu_ref[:, c0:c0 + FF_CHUNK], preferred_element_type=F32)
        scr[0:halo, :] = carry_scr[:, c0:c0 + FF_CHUNK]
        scr[halo:halo + tm, :] = u
        carry_scr[:, c0:c0 + FF_CHUNK] = u[tm - halo:, :]
        ue = scr[...]
        taps = cw_ref[0:1, c0:c0 + FF_CHUNK] * ue
        for j in range(1, FFN_CONV):
            taps = cw_ref[j:j + 1, c0:c0 + FF_CHUNK] * ue + pltpu.roll(taps, 1, axis=0)
        return cb_ref[:, c0:c0 + FF_CHUNK] + taps[halo:, :]

    for jc in range(D_FF // FF_CHUNK):
        gate = conv_cols(ug_scr.at[jc % 2], jc * FF_CHUNK)
        val = conv_cols(uv_scr.at[jc % 2], D_FF + jc * FF_CHUNK)
        a_scr[:, jc * FF_CHUNK:(jc + 1) * FF_CHUNK] = (_silu(gate) * val).astype(BF16)

    y_o[0] = x1 + g2_ref[0] * jnp.dot(a_scr[...], wd_ref[...], preferred_element_type=F32)
    tail_o[0] = carry_scr[...]


def _ffn_prompt(att, ssd, x, g1, sc2, sh2, g2, norm2_g, wo_b, wu_b, cw, cb, wd_b):
    b_, s_, _ = x.shape
    tm = min(ROW_TILE, s_)
    nt = s_ // tm
    row = lambda w: pl.BlockSpec((1, tm, w), lambda b, i: (b, i, 0))
    mod = pl.BlockSpec((1, 1, D_MODEL), lambda b, i: (b, 0, 0))
    return pl.pallas_call(
        _ffn_prompt_kernel,
        grid=(b_, nt),
        in_specs=[row(D_ATT), row(D_SSM), row(D_MODEL), mod, mod, mod, mod, _resident((1, D_MODEL)),
                  _resident((D_MIX, D_MODEL)), _resident((D_MODEL, 2 * D_FF)), _resident((FFN_CONV, 2 * D_FF)),
                  _resident((1, 2 * D_FF)), _resident((D_FF, D_MODEL))],
        out_specs=[row(D_MODEL), pl.BlockSpec((1, SUBLANES, 2 * D_FF), lambda b, i: (b, 0, 0))],
        out_shape=[jax.ShapeDtypeStruct((b_, s_, D_MODEL), F32),
                   jax.ShapeDtypeStruct((b_, SUBLANES, 2 * D_FF), F32)],
        scratch_shapes=[pltpu.VMEM((tm, D_FF), BF16), pltpu.VMEM((2, tm + SUBLANES, FF_CHUNK), F32),
                        pltpu.VMEM((2, tm + SUBLANES, FF_CHUNK), F32), pltpu.VMEM((SUBLANES, 2 * D_FF), F32)],
        compiler_params=_cparams("parallel", "arbitrary"),
        name="ffn_prompt",
    )(att, ssd, x, g1, sc2, sh2, g2, norm2_g.reshape(1, D_MODEL), wo_b, wu_b, cw, cb, wd_b)


def _ffn_sample_kernel(att_ref, ssd_ref, x_ref, g1_ref, sc2_ref, sh2_ref, g2_ref, n2_ref, wo_ref, wu_ref,
                       cw_ref, cb_ref, wd_ref, h0_ref, h1_ref, y_o, u_o, acc_scr):
    x1, h2 = _mix_and_norm(att_ref, ssd_ref, x_ref, g1_ref, sc2_ref, sh2_ref, n2_ref, wo_ref)
    acc_scr[...] = jnp.zeros_like(acc_scr)

    def conv_cols(c0):
        u = jnp.dot(h2, wu_ref[:, c0:c0 + FF_CHUNK], preferred_element_type=F32)
        u_o[:, c0:c0 + FF_CHUNK] = u
        sl = slice(c0, c0 + FF_CHUNK)
        return (cb_ref[:, sl] + cw_ref[0:1, sl] * h0_ref[:, sl] + cw_ref[1:2, sl] * h1_ref[:, sl]
                + cw_ref[2:3, sl] * u)

    for jc in range(D_FF // FF_CHUNK):
        gate = conv_cols(jc * FF_CHUNK)
        val = conv_cols(D_FF + jc * FF_CHUNK)
        a = (_silu(gate) * val).astype(BF16)
        acc_scr[...] += jnp.dot(a, wd_ref[jc * FF_CHUNK:(jc + 1) * FF_CHUNK, :], preferred_element_type=F32)

    y_o[0] = x1 + g2_ref[0] * acc_scr[...]


def _ffn_sample(att, ssd, x, g1, sc2, sh2, g2, norm2_g, wo_b, wu_b, cw, cb, wd_b, hist0, hist1):
    n = x.shape[1]
    full = lambda *shape: pl.BlockSpec(shape, lambda: (0,) * len(shape))
    r3 = lambda w: full(1, n, w)
    return pl.pallas_call(
        _ffn_sample_kernel,
        in_specs=[r3(D_ATT), r3(D_SSM), r3(D_MODEL), r3(D_MODEL), r3(D_MODEL), r3(D_MODEL), r3(D_MODEL),
                  full(1, D_MODEL), full(D_MIX, D_MODEL), full(D_MODEL, 2 * D_FF), full(FFN_CONV, 2 * D_FF),
                  full(1, 2 * D_FF), full(D_FF, D_MODEL), full(n, 2 * D_FF), full(n, 2 * D_FF)],
        out_specs=[r3(D_MODEL), full(n, 2 * D_FF)],
        out_shape=[jax.ShapeDtypeStruct((1, n, D_MODEL), F32), jax.ShapeDtypeStruct((n, 2 * D_FF), F32)],
        scratch_shapes=[pltpu.VMEM((n, D_MODEL), F32)],
        compiler_params=pltpu.CompilerParams(vmem_limit_bytes=VMEM_LIMIT),
        name="ffn_sample",
    )(att, ssd, x, g1, sc2, sh2, g2, norm2_g.reshape(1, D_MODEL), wo_b, wu_b, cw, cb, wd_b, hist0, hist1)


def _ssd_sample_kernel(xbc_ref, h0_ref, h1_ref, h2_ref, dt_ref, zg_ref, cw_ref, cb_ref, dtb_ref, alog_ref,
                       dsk_ref, ng_ref, ex_ref, st_ref, y_o, st_o):
    nb = xbc_ref.shape[0]
    conv = (cb_ref[...] + cw_ref[0:1, :] * h0_ref[...] + cw_ref[1:2, :] * h1_ref[...]
            + cw_ref[2:3, :] * h2_ref[...] + cw_ref[3:4, :] * xbc_ref[...])
    xc = _silu(conv)
    xs = xc[:, :D_SSM]
    bm = xc[:, D_SSM:D_SSM + SSM_GROUPS * SSM_STATE]
    cm = xc[:, D_SSM + SSM_GROUPS * SSM_STATE:]
    dt = _softplus(dt_ref[...] + dtb_ref[...])
    dec = jnp.exp(dt * (-jnp.exp(alog_ref[...])))
    ex = ex_ref[...]
    xdt = xs * _dot_sel_rhs(dt, ex)
    dec_x = _dot_sel_rhs(dec, ex)
    gw = SSM_HEADS // SSM_GROUPS * SSM_HEAD_DIM
    ones_n = jnp.ones((SUBLANES, SSM_STATE), BF16)
    row_id = lax.broadcasted_iota(jnp.int32, (SUBLANES, 1), 0)

    def stack(rows):
        out = jnp.zeros((SUBLANES, rows[0].shape[1]), F32)
        for r, v in enumerate(rows):
            out = jnp.where(row_id == r, v.astype(F32), out)
        return out.astype(BF16)

    ys = []
    for i in range(nb):
        yrow = []
        for g in range(SSM_GROUPS):
            hs = st_ref[i, g * gw:(g + 1) * gw, :]
            d0, d1, d2 = _split3(dec_x[i:i + 1, g * gw:(g + 1) * gw])
            x0, x1, _ = _split3(xdt[i:i + 1, g * gw:(g + 1) * gw])
            b0, b1, _ = _split3(bm[i:i + 1, g * SSM_STATE:(g + 1) * SSM_STATE])
            c0, c1, _ = _split3(cm[i:i + 1, g * SSM_STATE:(g + 1) * SSM_STATE])
            dec_b = _dot_tn(stack([d0, d1, d2]), ones_n)
            upd = _dot_tn(stack([x0, x0, x1, x1]), stack([b0, b1, b0, b1]))
            hn = hs * dec_b + upd
            st_o[i, g * gw:(g + 1) * gw, :] = hn
            h0, h1, _ = _split3(hn)
            c_rows = stack([c0, c1])
            yg = _dot_nt(c_rows, h0) + _dot_nt(c_rows, h1)
            yrow.append(yg[0:1, :] + yg[1:2, :])
        ys.append(jnp.concatenate(yrow, axis=-1))
    y = jnp.concatenate(ys, axis=0) + dsk_ref[...] * xs
    y = y * _silu(zg_ref[...])
    y_o[...] = _rms(y) * ng_ref[...]


def _ssd_sample(xbc, hist, dt, zg, conv_w, conv_b, dtb_pad, alog_pad, dsk_x, norm_g, ex, state):
    n = xbc.shape[0]
    nb = SUBLANES
    rows = lambda w: pl.BlockSpec((nb, w), lambda i: (i, 0))
    st = pl.BlockSpec((nb, D_SSM, SSM_STATE), lambda i: (i, 0, 0))
    return pl.pallas_call(
        _ssd_sample_kernel,
        grid=(n // nb,),
        in_specs=[rows(CONV_DIM), rows(CONV_DIM), rows(CONV_DIM), rows(CONV_DIM), rows(LANES), rows(D_SSM),
                  _resident((SSM_CONV, CONV_DIM)), _resident((1, CONV_DIM)), _resident((1, LANES)),
                  _resident((1, LANES)), _resident((1, D_SSM)), _resident((1, D_SSM)), _resident((LANES, D_SSM)), st],
        out_specs=[rows(D_SSM), st],
        out_shape=[jax.ShapeDtypeStruct((n, D_SSM), F32), jax.ShapeDtypeStruct((n, D_SSM, SSM_STATE), F32)],
        compiler_params=_cparams("parallel"),
        name="ssd_sample",
    )(xbc, hist[0], hist[1], hist[2], dt, zg, conv_w, conv_b, dtb_pad, alog_pad, dsk_x, norm_g, ex, state)


def _idx_sample_kernel(pt_ref, qi_ref, wi_ref, kin_ref, pool_ref, o_ref, kbuf, sem):
    b = pl.program_id(0)
    nb = pl.num_programs(0)
    n_pages = kbuf.shape[1]
    page = kbuf.shape[3]

    def page_copy(seq, slot, j):
        return pltpu.make_async_copy(pool_ref.at[pt_ref[seq, j]], kbuf.at[slot, j], sem.at[slot])

    def fetch(seq, slot):
        def body(j, c):
            page_copy(seq, slot, j).start()
            return c
        lax.fori_loop(0, n_pages, body, 0, unroll=8)

    @pl.when(b == 0)
    def _():
        fetch(0, 0)

    @pl.when(b + 1 < nb)
    def _():
        fetch(b + 1, (b + 1) % 2)

    slot = b % 2

    def wait_body(j, c):
        page_copy(b, slot, j).wait()
        return c
    lax.fori_loop(0, n_pages, wait_body, 0)

    qi = qi_ref[0]
    w = wi_ref[0]

    span = PAGES_PER_DOT * page

    def page_scores(j, c):
        keys = jnp.concatenate([kbuf[slot, PAGES_PER_DOT * j + i] for i in range(PAGES_PER_DOT)], axis=1)
        d = jnp.dot(qi, keys.astype(BF16), preferred_element_type=F32)
        o_ref[0, :, pl.ds(pl.multiple_of(j * span, span), span)] = jnp.sum(w * jnp.maximum(d, 0.0), axis=0, keepdims=True)
        return c

    lax.fori_loop(0, n_pages // PAGES_PER_DOT, page_scores, 0, unroll=True)
    knew = kin_ref[0].astype(BF16).astype(F32)
    dn = jnp.sum(qi.astype(F32) * knew, axis=-1, keepdims=True)
    s_new = jnp.sum(w * jnp.maximum(dn, 0.0), axis=0, keepdims=True)
    lane = lax.broadcasted_iota(jnp.int32, (1, SCORE_PAD), 1)
    o_ref[0, :, n_pages * page:] = jnp.where(lane == 0, s_new, -jnp.inf)


def _idx_sample(page_table, qi3, wi3, ki_new3, pool_t):
    n, n_pages = page_table.shape
    page = pool_t.shape[2]
    assert n_pages % (8 * PAGES_PER_DOT) == 0
    past = n_pages * page
    grid_spec = pltpu.PrefetchScalarGridSpec(
        num_scalar_prefetch=1,
        grid=(n,),
        in_specs=[
            pl.BlockSpec((1, IDX_HEADS, IDX_DIM), lambda b, pt: (b, 0, 0)),
            pl.BlockSpec((1, IDX_HEADS, 1), lambda b, pt: (b, 0, 0)),
            pl.BlockSpec((1, 1, IDX_DIM), lambda b, pt: (b, 0, 0)),
            pl.BlockSpec(memory_space=pl.ANY),
        ],
        out_specs=pl.BlockSpec((1, 1, past + SCORE_PAD), lambda b, pt: (b, 0, 0)),
        scratch_shapes=[pltpu.VMEM((2, n_pages, IDX_DIM, page), F32), pltpu.SemaphoreType.DMA((2,))],
    )
    return pl.pallas_call(
        _idx_sample_kernel,
        grid_spec=grid_spec,
        out_shape=jax.ShapeDtypeStruct((n, 1, past + SCORE_PAD), F32),
        compiler_params=_cparams("arbitrary"),
        name="idx_sample",
    )(page_table, qi3, wi3, ki_new3, pool_t)


def _att_sample_kernel(q_ref, k_ref, v_ref, knew_ref, vnew_ref, isnew_ref, bias_ref, o_ref):
    nb = q_ref.shape[0]
    rows = k_ref.shape[1]
    scale = HEAD_DIM ** -0.5

    def tiled(new):
        pair = jnp.concatenate([new] * (SUBLANES // N_KV_HEADS), axis=0)
        return jnp.broadcast_to(pair[None], (rows // SUBLANES, SUBLANES, HEAD_DIM)).reshape(rows, HEAD_DIM)

    for i in range(nb):
        is_new = isnew_ref[i] > 0.5
        kb = jnp.where(is_new, tiled(knew_ref[i]), k_ref[i]).astype(BF16)
        vb = jnp.where(is_new, tiled(vnew_ref[i]), v_ref[i]).astype(BF16)
        s = _dot_nt(q_ref[i].astype(BF16), kb) * scale + bias_ref[i]
        p = jnp.exp(s - jnp.max(s, axis=-1, keepdims=True))
        p = p * (1.0 / jnp.sum(p, axis=-1, keepdims=True))
        o_ref[i] = jnp.dot(p.astype(BF16), vb, preferred_element_type=F32)


def _att_sample(q3, k_rows, v_rows, k_new, v_new, is_new, bias):
    n, rows = k_rows.shape[:2]
    nb = SUBLANES // 2
    gathered = pl.BlockSpec((nb, rows, HEAD_DIM), lambda i: (i, 0, 0))
    new3 = pl.BlockSpec((nb, N_KV_HEADS, HEAD_DIM), lambda i: (i, 0, 0))
    return pl.pallas_call(
        _att_sample_kernel,
        grid=(n // nb,),
        in_specs=[
            pl.BlockSpec((nb, N_HEADS, HEAD_DIM), lambda i: (i, 0, 0)),
            gathered, gathered, new3, new3,
            pl.BlockSpec((nb, rows, 1), lambda i: (i, 0, 0)),
            pl.BlockSpec((nb, N_HEADS, rows), lambda i: (i, 0, 0)),
        ],
        out_specs=pl.BlockSpec((nb, N_HEADS, HEAD_DIM), lambda i: (i, 0, 0)),
        out_shape=jax.ShapeDtypeStruct((n, N_HEADS, HEAD_DIM), F32),
        compiler_params=_cparams("parallel"),
        name="att_sample",
    )(q3, k_rows, v_rows, k_new, v_new, is_new, bias)


def _sel_sample_kernel(k_keep, n_valid, sc_ref, o_ref):
    rows, n = sc_ref.shape
    ch = ATT_CHUNK
    span = 2 * ch

    def scan(fn, op, init):
        def body(j, acc):
            r = pl.ds(pl.multiple_of(j * span, span), span)
            return op(jnp.stack([acc, _fold(fn(sc_ref[r, :]), SCAN_FOLD, op)]), axis=0)
        return op(lax.fori_loop(0, rows // span, body, jnp.full((SCAN_FOLD, n), init, F32)), axis=0, keepdims=True)

    k_row = jnp.full((1, n), float(k_keep), F32)
    thr = _kth_largest(scan, k_row, jnp.full((1, n), n_valid <= k_keep))
    need = k_row - scan(lambda t: jnp.where(t > thr, 1.0, 0.0), jnp.sum, 0.0)
    lower = jnp.where(lax.broadcasted_iota(jnp.int32, (ch, ch), 1) <= lax.broadcasted_iota(jnp.int32, (ch, ch), 0),
                      1.0, 0.0).astype(BF16)
    ones_r = jnp.ones((SUBLANES, ch), BF16)
    key_local = lax.broadcasted_iota(jnp.int32, (ch, n), 0).astype(F32)
    slot_iota = lax.broadcasted_iota(jnp.int32, (k_keep, n), 0).astype(F32)
    none = float(rows)

    def chunk_body(c, carry):
        run, slot, out = carry
        t = sc_ref[pl.ds(pl.multiple_of(c * ch, ch), ch), :]
        eq = t == thr
        eqb = jnp.where(eq, 1.0, 0.0).astype(BF16)
        rank = jnp.dot(lower, eqb, preferred_element_type=F32) + run
        cand = jnp.where(t > thr, key_local, jnp.where(eq, jnp.where(rank <= need, key_local, none), none))
        left = _fold(jnp.where(cand < none, 1.0, 0.0), SUBLANES, jnp.sum).sum(axis=0, keepdims=True)
        base = jnp.asarray(c * ch, F32)

        def extract(_, s):
            cand, out, slot = s
            cur = jnp.min(cand, axis=0, keepdims=True)
            has = cur < none
            out = jnp.where(slot_iota == slot, jnp.where(has, cur + base, out), out)
            return jnp.where(cand == cur, none, cand), out, slot + jnp.where(has, 1.0, 0.0)

        _, out, slot = lax.fori_loop(0, jnp.max(left).astype(jnp.int32), extract, (cand, out, slot))
        return run + jnp.dot(ones_r, eqb, preferred_element_type=F32)[0:1, :], slot, out

    zeros_row = jnp.zeros((1, n), F32)
    _, _, out = lax.fori_loop(0, rows // ch, chunk_body, (zeros_row, zeros_row, jnp.zeros((k_keep, n), F32)))
    o_ref[...] = out.astype(jnp.int32)


def _sel_sample(scores_t, k_keep, n_valid):
    rows, n = scores_t.shape
    assert rows % (2 * ATT_CHUNK) == 0
    return pl.pallas_call(
        functools.partial(_sel_sample_kernel, k_keep, n_valid),
        in_specs=[pl.BlockSpec((rows, n), lambda: (0, 0))],
        out_specs=pl.BlockSpec((k_keep, n), lambda: (0, 0)),
        out_shape=jax.ShapeDtypeStruct((k_keep, n), jnp.int32),
        compiler_params=pltpu.CompilerParams(vmem_limit_bytes=VMEM_LIMIT),
        name="sel_sample",
    )(scores_t)


def _pack_w_in(w_in):
    offs = np.cumsum((0,) + IN_SPLITS)
    parts = [w_in[:, offs[i]:offs[i + 1]] for i in range(len(IN_SPLITS))]
    pad = lambda a: jnp.pad(a, ((0, 0), (0, LANES - a.shape[1])))
    q, k, v, qi, ki, wi, zg, xbc, dt = parts
    return jnp.concatenate([q, k, v, qi, pad(ki), pad(wi), zg, xbc, pad(dt)], axis=1).astype(BF16)


def _t5_bucket(dist):
    dist = jnp.maximum(dist, 0)
    max_exact = N_BUCKETS // 2
    d = jnp.maximum(dist, 1).astype(F32)
    large = max_exact + (jnp.log(d / max_exact) / math.log(MAX_DISTANCE / max_exact) * (N_BUCKETS - max_exact)).astype(jnp.int32)
    large = jnp.minimum(large, N_BUCKETS - 1)
    return jnp.where(dist < max_exact, dist, large)


def _rel_bias_tables(rel_bias, tq):
    assert ATT_CHUNK == tq and tq >= MAX_DISTANCE
    n = 2 * tq
    idx = jnp.arange(n, dtype=jnp.int32)
    query_minus_key = jnp.where(idx < tq, idx, idx - n)
    far = rel_bias[_t5_bucket(jnp.int32(MAX_DISTANCE))]
    tabs = []
    for d0 in (0, tq):
        f = (rel_bias[_t5_bucket(d0 + query_minus_key)] - far[None, :]) * LOG2_E
        flat = jnp.tile(f.T, (1, ATT_CHUNK))
        tabs.append(flat[:, :ATT_CHUNK * (n - 1)].reshape(-1, ATT_CHUNK, n - 1)[:, :, :tq])
    return jnp.stack(tabs)


def _head_expand():
    h = jnp.arange(LANES, dtype=jnp.int32)[:, None]
    c = jnp.arange(D_SSM, dtype=jnp.int32)[None, :]
    ex = (c // SSM_HEAD_DIM == h).astype(BF16)
    return ex, ex.T


def _pad_lanes(v):
    return jnp.pad(v.astype(F32), (0, LANES - v.shape[0])).reshape(1, LANES)


def kernel(x_prompt, x_sample, cache_k, cache_v, cache_kidx, state_ssm, state_conv_ssd, state_conv_ffn, page_table,
           c_prompt, c_sample, rel_bias, norm1_g, w_ada, b_ada, w_in, q_norm_g, k_norm_g, conv_ssd_w, conv_ssd_b,
           dt_bias, a_log, d_skip, ssd_norm_g, w_out, norm2_g, w_up, conv_ffn_w, conv_ffn_b, w_down):
    depth = w_in.shape[0]
    assert depth == 1
    bp, s_, _ = x_prompt.shape
    ns = x_sample.shape[0]
    assert x_sample.shape[1] == 1
    lyr = 0
    n_pages = page_table.shape[1]
    page = cache_k.shape[2]
    past = n_pages * page
    topk_p = min(TOPK_MAX, s_ // TOPK_DIV)
    topk_s = min(TOPK_MAX, (past + 1) // TOPK_DIV)

    w_pack = _pack_w_in(w_in[lyr])
    wo_b = w_out[lyr].astype(BF16)
    wu_b = w_up[lyr].astype(BF16)
    wd_b = w_down[lyr].astype(BF16)
    ex, ext = _head_expand()
    dtb_pad = _pad_lanes(dt_bias[lyr])
    alog_pad = _pad_lanes(a_log[lyr])
    dsk_x = jnp.repeat(d_skip[lyr].astype(F32), SSM_HEAD_DIM).reshape(1, D_SSM)
    ssd_g = ssd_norm_g[lyr].reshape(1, D_SSM)
    cw_ssd, cb_ssd = conv_ssd_w[lyr], conv_ssd_b[lyr].reshape(1, CONV_DIM)
    cw_ffn, cb_ffn = conv_ffn_w[lyr], conv_ffn_b[lyr].reshape(1, 2 * D_FF)
    tz = _rel_bias_tables(rel_bias.astype(F32), min(Q_BLOCK, s_))

    n_c = bp + ns
    c_all = jnp.pad(jnp.concatenate([c_prompt, c_sample], axis=0), ((0, (-n_c) % SUBLANES), (0, 0)))
    mod = _ada(c_all, w_ada[lyr], b_ada[lyr])
    mods_p = [mod[:bp, i * D_MODEL:(i + 1) * D_MODEL].reshape(bp, 1, D_MODEL) for i in range(6)]
    mods_s = [mod[bp:bp + ns, i * D_MODEL:(i + 1) * D_MODEL].reshape(1, ns, D_MODEL) for i in range(6)]

    sh1, sc1, g1, sh2, sc2, g2 = mods_p
    (q_t, k_f, v_f, k_b, v_t, qi_t, ki_f, ki_b, wi_t, zg, xbc, dt) = _inproj(
        x_prompt, norm1_g[lyr], sc1, sh1, w_pack, q_norm_g[lyr], k_norm_g[lyr], rows_per_mod=1, queries_on_lanes=True)
    att_p = _dsa_prompt(qi_t, wi_t, ki_b, q_t, k_b, v_t, tz, topk_p)
    ssd_p, ssm_p = _ssd_prompt(xbc, dt, zg, cw_ssd, cb_ssd, dtb_pad, alog_pad, dsk_x, ssd_g, ex, ext)
    y_p, tail_p = _ffn_prompt(att_p, ssd_p, x_prompt, g1, sc2, sh2, g2, norm2_g[lyr], wo_b, wu_b, cw_ffn, cb_ffn, wd_b)

    k_p = k_f.reshape(1, bp, s_, N_KV_HEADS, HEAD_DIM)
    v_p = v_f.reshape(1, bp, s_, N_KV_HEADS, HEAD_DIM)
    kidx_p = ki_f.reshape(1, bp, s_, IDX_DIM)
    ssm_p = ssm_p.reshape(1, bp, SSM_HEADS, SSM_HEAD_DIM, SSM_STATE)
    cssd_p = xbc[:, s_ - (SSM_CONV - 1):, :].reshape(1, bp, SSM_CONV - 1, CONV_DIM)
    cffn_p = tail_p[:, SUBLANES - (FFN_CONV - 1):, :].reshape(1, bp, FFN_CONV - 1, 2 * D_FF)

    sh1, sc1, g1, sh2, sc2, g2 = mods_s
    xs3 = x_sample.reshape(1, ns, D_MODEL)
    (q_b, k_f, v_f, _, _, qi_b, ki_f, _, wi_f, zg, xbc, dt) = _inproj(
        xs3, norm1_g[lyr], sc1, sh1, w_pack, q_norm_g[lyr], k_norm_g[lyr], rows_per_mod=ns, queries_on_lanes=False)
    k_new, v_new, ki_new = k_f[0], v_f[0], ki_f[0]

    scores = _idx_sample(page_table, qi_b[0].reshape(ns, IDX_HEADS, IDX_DIM), wi_f[0].reshape(ns, IDX_HEADS, 1),
                         ki_new.reshape(ns, 1, IDX_DIM), jnp.swapaxes(cache_kidx[lyr], 1, 2))[:, 0, :]
    idx = _sel_sample(scores.T, topk_s, past + 1).T
    n_rows = topk_s * N_KV_HEADS
    is_new = jnp.repeat((idx >= past).astype(F32), N_KV_HEADS, axis=1)[:, :, None]
    pidx = jnp.minimum(idx, past - 1)
    page_hit = (pidx // page)[:, :, None] == jnp.arange(n_pages, dtype=jnp.int32)
    phys = jnp.sum(jnp.where(page_hit, page_table[:, None, :], 0), axis=-1)
    off = pidx % page
    bucket_hit = _t5_bucket(past - idx)[:, None, :, None] == jnp.arange(N_BUCKETS, dtype=jnp.int32)
    bias = jnp.sum(jnp.where(bucket_hit, rel_bias.astype(F32).T[None, :, None, :], 0.0), axis=-1)
    own_group = jnp.arange(N_KV_HEADS)[None, :] == (jnp.arange(N_HEADS) // KV_REP)[:, None]
    bias_rows = jnp.where(own_group[None, :, None, :], bias[:, :, :, None], NEG_BIG).reshape(ns, N_HEADS, n_rows)
    att_s = _att_sample(q_b[0].astype(F32).reshape(ns, N_HEADS, HEAD_DIM),
                        cache_k[lyr][phys, off].reshape(ns, n_rows, HEAD_DIM),
                        cache_v[lyr][phys, off].reshape(ns, n_rows, HEAD_DIM),
                        k_new.reshape(ns, N_KV_HEADS, HEAD_DIM), v_new.reshape(ns, N_KV_HEADS, HEAD_DIM), is_new,
                        bias_rows).reshape(1, ns, D_ATT)

    hist_ssd = state_conv_ssd[lyr]
    ssd_s, ssm_s = _ssd_sample(xbc[0], [hist_ssd[:, j, :] for j in range(SSM_CONV - 1)], dt[0], zg[0], cw_ssd, cb_ssd,
                               dtb_pad, alog_pad, dsk_x, ssd_g, ex,
                               state_ssm[lyr].reshape(ns, D_SSM, SSM_STATE))
    hist_ffn = state_conv_ffn[lyr]
    y_s, u_s = _ffn_sample(att_s, ssd_s.reshape(1, ns, D_SSM), xs3, g1, sc2, sh2, g2, norm2_g[lyr], wo_b, wu_b,
                           cw_ffn, cb_ffn, wd_b, hist_ffn[:, 0, :], hist_ffn[:, 1, :])

    k_s = k_new.reshape(1, ns, 1, N_KV_HEADS, HEAD_DIM)
    v_s = v_new.reshape(1, ns, 1, N_KV_HEADS, HEAD_DIM)
    kidx_s = ki_new.reshape(1, ns, 1, IDX_DIM)
    ssm_s = ssm_s.reshape(1, ns, SSM_HEADS, SSM_HEAD_DIM, SSM_STATE)
    cssd_s = jnp.concatenate([hist_ssd[:, 1:, :], xbc[0][:, None, :]], axis=1)[None]
    cffn_s = jnp.stack([hist_ffn[:, 1, :], u_s], axis=1)[None]

    return (y_p, y_s.reshape(ns, 1, D_MODEL), k_p, v_p, kidx_p, ssm_p, cssd_p, cffn_p,
            k_s, v_s, kidx_s, ssm_s, cssd_s, cffn_s)
```

```python
import functools
import math

import jax
import jax.numpy as jnp
import numpy as np
from jax import lax
from jax.experimental import pallas as pl
from jax.experimental.pallas import tpu as pltpu

F32 = jnp.float32
BF16 = jnp.bfloat16

D_MODEL = 1024
N_HEADS = 8
N_KV_HEADS = 4
HEAD_DIM = 128
KV_REP = N_HEADS // N_KV_HEADS
IDX_HEADS = 8
IDX_DIM = 64
TOPK_MAX = 256
TOPK_DIV = 4
N_BUCKETS = 32
MAX_DISTANCE = 128
SSM_HEADS = 16
SSM_HEAD_DIM = 64
SSM_GROUPS = 4
SSM_STATE = 128
SSM_CONV = 4
SSD_CHUNK = 128
D_ATT = N_HEADS * HEAD_DIM
D_KV = N_KV_HEADS * HEAD_DIM
D_SSM = SSM_HEADS * SSM_HEAD_DIM
D_MIX = D_ATT + D_SSM
CONV_DIM = D_SSM + 2 * SSM_GROUPS * SSM_STATE
D_FF = 2816
FFN_CONV = 3
EPS = 1e-6
IN_SPLITS = (D_ATT, D_KV, D_KV, IDX_HEADS * IDX_DIM, IDX_DIM, IDX_HEADS, D_SSM, CONV_DIM, SSM_HEADS)

LANES = 128
SUBLANES = 8
VMEM_LIMIT = 52 * 1024 * 1024

OFF_Q = 0
OFF_K = OFF_Q + D_ATT
OFF_V = OFF_K + D_KV
OFF_QI = OFF_V + D_KV
OFF_KI = OFF_QI + IDX_HEADS * IDX_DIM
OFF_WI = OFF_KI + LANES
OFF_ZG = OFF_WI + LANES
OFF_XBC = OFF_ZG + D_SSM
OFF_DT = OFF_XBC + CONV_DIM
N_PACK = OFF_DT + LANES

NEG_BIG = -1e30
LOG2_E = math.log2(math.e)
ATT_CHUNK = 256
FF_CHUNK = 256
ROW_TILE = 256
Q_BLOCK = 256
PAGES_PER_DOT = 4
SCORE_PAD = 2 * ATT_CHUNK
VALUE_STEPS = 14
SNAP_ONLY_TRIPS = 4
SCAN_FOLD = 64
ATT_FOLD = 16
TRIP_WIDTHS = (4, 2, 1)


def _cparams(*sem):
    return pltpu.CompilerParams(dimension_semantics=sem, vmem_limit_bytes=VMEM_LIMIT)


def _resident(shape):
    nd = len(shape)
    return pl.BlockSpec(shape, lambda *_: (0,) * nd, pipeline_mode=pl.Buffered(1))


def _silu(x):
    return x * (1.0 / (1.0 + jnp.exp(-x)))


def _softplus(x):
    return jnp.maximum(x, 0.0) + jnp.log1p(jnp.exp(-jnp.abs(x)))


def _rms(x):
    return x * lax.rsqrt(jnp.mean(x * x, axis=-1, keepdims=True) + EPS)


def _split3(x):
    a = x.astype(BF16)
    r = x - a.astype(F32)
    b = r.astype(BF16)
    c = (r - b.astype(F32)).astype(BF16)
    return a, b, c


def _dot_sel_rhs(x, sel):
    return sum(jnp.dot(p, sel, preferred_element_type=F32) for p in _split3(x))


def _dot_sel_lhs(sel, x):
    return sum(jnp.dot(sel, p, preferred_element_type=F32) for p in _split3(x))


def _dot_nt(a, b):
    return lax.dot_general(a, b, (((1,), (1,)), ((), ())), preferred_element_type=F32)


def _dot_tn(a, b):
    return lax.dot_general(a, b, (((0,), (0,)), ((), ())), preferred_element_type=F32)


def _ada_kernel(c_ref, w_ref, b_ref, o_ref):
    s = _silu(c_ref[...]).astype(BF16)
    o_ref[...] = jnp.dot(s, w_ref[...].astype(BF16), preferred_element_type=F32) + b_ref[...]


def _ada(c_all, w_ada, b_ada):
    rows = c_all.shape[0]
    n = w_ada.shape[1]
    return pl.pallas_call(
        _ada_kernel,
        grid=(n // D_MODEL,),
        in_specs=[
            pl.BlockSpec((rows, D_MODEL), lambda j: (0, 0)),
            pl.BlockSpec((D_MODEL, D_MODEL), lambda j: (0, j)),
            pl.BlockSpec((1, D_MODEL), lambda j: (0, j)),
        ],
        out_specs=pl.BlockSpec((rows, D_MODEL), lambda j: (0, j)),
        out_shape=jax.ShapeDtypeStruct((rows, n), F32),
        compiler_params=_cparams("arbitrary"),
        name="ada_mod",
    )(c_all, w_ada, b_ada.reshape(1, n))


def _inproj_kernel(queries_on_lanes, x_ref, g_ref, sc_ref, sh_ref, w_ref, qg_ref, kg_ref,
                   q_o, k_o, v_o, kb_o, vb_o, qi_o, ki_o, kib_o, wi_o, zg_o, xbc_o, dt_o):
    x = x_ref[0]
    h = _rms(x) * g_ref[...]
    hb = (h * (1.0 + sc_ref[0]) + sh_ref[0]).astype(BF16)

    def sec(a, b):
        return jnp.dot(hb, w_ref[:, a:b], preferred_element_type=F32)

    zq = sec(OFF_Q, OFF_K)
    for i in range(N_HEADS):
        qn = _rms(zq[:, i * HEAD_DIM:(i + 1) * HEAD_DIM]) * qg_ref[...]
        if queries_on_lanes:
            q_o[0, i * HEAD_DIM:(i + 1) * HEAD_DIM, :] = qn.T.astype(BF16)
        else:
            q_o[0, :, i * HEAD_DIM:(i + 1) * HEAD_DIM] = qn.astype(BF16)
    zk = sec(OFF_K, OFF_V)
    for i in range(N_KV_HEADS):
        zh = zk[:, i * HEAD_DIM:(i + 1) * HEAD_DIM]
        kn = _rms(zh) * kg_ref[...]
        k_o[0, :, i, :] = kn
        kb_o[0, :, i * HEAD_DIM:(i + 1) * HEAD_DIM] = kn.astype(BF16)
    zv = sec(OFF_V, OFF_QI)
    for i in range(N_KV_HEADS):
        v_o[0, :, i, :] = zv[:, i * HEAD_DIM:(i + 1) * HEAD_DIM]
    zqi = sec(OFF_QI, OFF_KI)
    zwi = sec(OFF_WI, OFF_ZG)
    if queries_on_lanes:
        vb_o[0] = zv.T.astype(BF16)
        qi_o[0] = zqi.T.astype(BF16)
        wi_o[0] = zwi.T[:IDX_HEADS, :]
    else:
        vb_o[0] = zv.astype(BF16)
        qi_o[0] = zqi.astype(BF16)
        wi_o[0] = zwi[:, :IDX_HEADS]
    zki = sec(OFF_KI, OFF_WI)[:, :IDX_DIM]
    ki_o[0] = zki
    kib_o[0] = zki.astype(BF16)
    zg_o[0] = sec(OFF_ZG, OFF_XBC)
    xbc_o[0] = sec(OFF_XBC, OFF_DT)
    dt_o[0] = sec(OFF_DT, N_PACK)


def _inproj(x, norm_g, sc, sh, w_pack, qg, kg, rows_per_mod, queries_on_lanes):
    g_, r_, _ = x.shape
    tm = min(ROW_TILE, r_)
    nt = r_ // tm
    mod_rows = 1 if rows_per_mod == 1 else tm
    mod_map = (lambda b, i: (b, 0, 0)) if rows_per_mod == 1 else (lambda b, i: (b, i, 0))
    outs = [(D_ATT, BF16, True), (D_KV, F32, False), (D_KV, F32, False), (D_KV, BF16, False), (D_KV, BF16, True),
            (IDX_HEADS * IDX_DIM, BF16, True), (IDX_DIM, F32, False), (IDX_DIM, BF16, False), (IDX_HEADS, F32, True),
            (D_SSM, F32, False), (CONV_DIM, F32, False), (LANES, F32, False)]
    out_specs, out_shape = [], []
    for n_out, (w, dt, tr) in enumerate(outs):
        if n_out in (1, 2):
            out_specs.append(pl.BlockSpec((1, tm, N_KV_HEADS, HEAD_DIM), lambda b, i: (b, i, 0, 0)))
            out_shape.append(jax.ShapeDtypeStruct((g_, r_, N_KV_HEADS, HEAD_DIM), dt))
        elif tr and queries_on_lanes:
            out_specs.append(pl.BlockSpec((1, w, tm), lambda b, i: (b, 0, i)))
            out_shape.append(jax.ShapeDtypeStruct((g_, w, r_), dt))
        else:
            out_specs.append(pl.BlockSpec((1, tm, w), lambda b, i: (b, i, 0)))
            out_shape.append(jax.ShapeDtypeStruct((g_, r_, w), dt))
    return pl.pallas_call(
        functools.partial(_inproj_kernel, queries_on_lanes),
        grid=(g_, nt),
        in_specs=[
            pl.BlockSpec((1, tm, D_MODEL), lambda b, i: (b, i, 0)),
            _resident((1, D_MODEL)),
            pl.BlockSpec((1, mod_rows, D_MODEL), mod_map),
            pl.BlockSpec((1, mod_rows, D_MODEL), mod_map),
            _resident((D_MODEL, N_PACK)),
            _resident((1, HEAD_DIM)),
            _resident((1, HEAD_DIM)),
        ],
        out_specs=out_specs,
        out_shape=out_shape,
        compiler_params=_cparams("parallel", "parallel"),
        name="in_proj",
    )(x, norm_g.reshape(1, D_MODEL), sc, sh, w_pack, qg.reshape(1, HEAD_DIM), kg.reshape(1, HEAD_DIM))


def _ssd_prompt_kernel(xbc_ref, dt_ref, zg_ref, cw_ref, cb_ref, dtb_ref, alog_ref, dsk_ref, ng_ref,
                       ex_ref, ext_ref, y_o, st_o, xe_scr, st_scr):
    c = pl.program_id(1)
    q_ = SSD_CHUNK
    halo = SUBLANES

    @pl.when(c == 0)
    def _():
        xe_scr[0:halo, :] = jnp.zeros((halo, CONV_DIM), F32)
        st_scr[...] = jnp.zeros_like(st_scr)

    xe_scr[halo:halo + q_, :] = xbc_ref[0]
    xe = xe_scr[...]
    taps = cw_ref[0:1, :] * xe
    for j in range(1, SSM_CONV):
        taps = cw_ref[j:j + 1, :] * xe + pltpu.roll(taps, 1, axis=0)
    conv = cb_ref[...] + taps[halo:halo + q_, :]
    xe_scr[0:halo, :] = xe_scr[q_:q_ + halo, :]
    xc = _silu(conv)
    xs = xc[:, :D_SSM]
    bm = xc[:, D_SSM:D_SSM + SSM_GROUPS * SSM_STATE]
    cm = xc[:, D_SSM + SSM_GROUPS * SSM_STATE:]

    dt = _softplus(dt_ref[0] + dtb_ref[...])
    da = dt * (-jnp.exp(alog_ref[...]))
    row = lax.broadcasted_iota(jnp.int32, (q_, q_), 0)
    col = lax.broadcasted_iota(jnp.int32, (q_, q_), 1)
    causal = row >= col
    tri = jnp.where(causal, 1.0, 0.0).astype(BF16)
    acs = _dot_sel_lhs(tri, da)
    acs_t = acs.T
    acs_last = acs[q_ - 1:q_, :]
    ex = ex_ref[...]
    dt_x = _dot_sel_rhs(dt, ex)
    eacs_x = _dot_sel_rhs(jnp.exp(acs), ex)
    dte_x = _dot_sel_rhs(jnp.exp(acs_last - acs), ex)
    cdec = jnp.exp(jnp.broadcast_to(acs_t[:, q_ - 1:q_], (LANES, SSM_STATE)))
    cdec_rows = _dot_sel_lhs(ext_ref[...], cdec)

    xdt = xs * dt_x
    xdtd = xdt * dte_x
    gw = SSM_HEADS // SSM_GROUPS * SSM_HEAD_DIM
    gated = []
    for g in range(SSM_GROUPS):
        bg = bm[:, g * SSM_STATE:(g + 1) * SSM_STATE].astype(BF16)
        cg = cm[:, g * SSM_STATE:(g + 1) * SSM_STATE].astype(BF16)
        cb = _dot_nt(cg, bg)
        yd = []
        for hl in range(SSM_HEADS // SSM_GROUPS):
            h = g * (SSM_HEADS // SSM_GROUPS) + hl
            seg = acs[:, h:h + 1] - acs_t[h:h + 1, :]
            lmat = jnp.exp(jnp.where(causal, seg, -jnp.inf))
            yd.append(jnp.dot((cb * lmat).astype(BF16),
                              xdt[:, h * SSM_HEAD_DIM:(h + 1) * SSM_HEAD_DIM].astype(BF16),
                              preferred_element_type=F32))
        y_diag = jnp.concatenate(yd, axis=-1)
        st_old = st_scr[g * gw:(g + 1) * gw, :]
        y_off = _dot_nt(cg, st_old.astype(BF16)) * eacs_x[:, g * gw:(g + 1) * gw]
        st_new = _dot_tn(xdtd[:, g * gw:(g + 1) * gw].astype(BF16), bg)
        st_scr[g * gw:(g + 1) * gw, :] = st_old * cdec_rows[g * gw:(g + 1) * gw, :] + st_new
        yg = y_diag + y_off + dsk_ref[:, g * gw:(g + 1) * gw] * xs[:, g * gw:(g + 1) * gw]
        gated.append(yg * _silu(zg_ref[0, :, g * gw:(g + 1) * gw]))

    y_o[0] = (_rms(jnp.concatenate(gated, axis=-1)) * ng_ref[...]).astype(y_o.dtype)

    @pl.when(c == pl.num_programs(1) - 1)
    def _():
        st_o[0] = st_scr[...]


def _ssd_prompt(xbc, dt, zg, conv_w, conv_b, dtb_pad, alog_pad, dsk_x, norm_g, ex, ext):
    b_, s_, _ = xbc.shape
    nc = s_ // SSD_CHUNK
    return pl.pallas_call(
        _ssd_prompt_kernel,
        grid=(b_, nc),
        in_specs=[
            pl.BlockSpec((1, SSD_CHUNK, CONV_DIM), lambda b, c: (b, c, 0)),
            pl.BlockSpec((1, SSD_CHUNK, LANES), lambda b, c: (b, c, 0)),
            pl.BlockSpec((1, SSD_CHUNK, D_SSM), lambda b, c: (b, c, 0)),
            _resident((SSM_CONV, CONV_DIM)),
            _resident((1, CONV_DIM)),
            _resident((1, LANES)),
            _resident((1, LANES)),
            _resident((1, D_SSM)),
            _resident((1, D_SSM)),
            _resident((LANES, D_SSM)),
            _resident((D_SSM, LANES)),
        ],
        out_specs=[
            pl.BlockSpec((1, SSD_CHUNK, D_SSM), lambda b, c: (b, c, 0)),
            pl.BlockSpec((1, D_SSM, SSM_STATE), lambda b, c: (b, 0, 0)),
        ],
        out_shape=[
            jax.ShapeDtypeStruct((b_, s_, D_SSM), BF16),
            jax.ShapeDtypeStruct((b_, D_SSM, SSM_STATE), F32),
        ],
        scratch_shapes=[
            pltpu.VMEM((SSD_CHUNK + SUBLANES, CONV_DIM), F32),
            pltpu.VMEM((D_SSM, SSM_STATE), F32),
        ],
        compiler_params=_cparams("parallel", "arbitrary"),
        name="ssd_prompt",
    )(xbc, dt, zg, conv_w, conv_b, dtb_pad, alog_pad, dsk_x, norm_g, ex, ext)


def _f32_key(x):
    b = lax.bitcast_convert_type(x, jnp.int32)
    return b ^ ((b >> 31) & 0x7FFFFFFF)


def _key_f32(k):
    return lax.bitcast_convert_type(k ^ ((k >> 31) & 0x7FFFFFFF), F32)


def _fold(x, rows, op):
    return op(x.reshape(x.shape[0] // rows, rows, x.shape[1]), axis=0)


def _chunk_trips(fn, lo, hi, carry, widths):
    assert widths[-1] == 1
    start = lo
    for w in widths:
        trips = (hi - start) // w
        carry = lax.fori_loop(0, trips, lambda t, c, s=start, w=w: fn(s + t * w, c, w), carry)
        start = start + trips * w
    return carry


def _kth_largest(scan, k_row, take_all, mn=None, mx=None):
    def count_ge(x):
        return scan(lambda t: jnp.where(t >= x, 1.0, 0.0), jnp.sum, 0.0)

    if mx is None:
        mx = scan(lambda t: t, jnp.max, -jnp.inf)
        mn = scan(lambda t: jnp.where(t == -jnp.inf, jnp.inf, t), jnp.min, jnp.inf)
    above = mx + jnp.maximum(jnp.abs(mx) * 2.0 ** -20, 1e-30)
    state = (mn, above, mn, jnp.where(take_all, 1, 0).astype(jnp.int32))

    def value_step(_, s):
        lo, hi, thr, done = s
        mid = 0.5 * lo + 0.5 * hi
        inside = jnp.where(mid > lo, jnp.where(mid < hi, 1, 0), 0)
        cnt = count_ge(mid)
        ge = cnt >= k_row
        finished = jnp.where(cnt == k_row, inside, 0) * (1 - done)
        lo = jnp.where(inside == 1, jnp.where(ge, mid, lo), lo)
        hi = jnp.where(inside == 1, jnp.where(ge, hi, mid), hi)
        return lo, hi, jnp.where(finished == 1, mid, thr), done + finished

    def snap_step(s):
        lo, hi, thr, done = s
        cand = scan(lambda t: jnp.where(t < hi, t, -jnp.inf), jnp.max, -jnp.inf)
        ge = count_ge(cand) >= k_row
        finished = jnp.where(ge, 1, 0) * (1 - done)
        return lo, jnp.where(ge, hi, cand), jnp.where(finished == 1, cand, thr), done + finished

    def key_step(s):
        lo, hi, thr, done = s
        klo, khi = _f32_key(lo), _f32_key(hi)
        mid_k = (klo >> 1) + (khi >> 1) + (klo & khi & 1)
        mid = _key_f32(mid_k)
        cnt = count_ge(mid)
        ge = cnt >= k_row
        collapsed = mid_k == klo
        finished = jnp.where(collapsed, 1, jnp.where(cnt == k_row, 1, 0)) * (1 - done)
        thr = jnp.where(finished == 1, jnp.where(collapsed, lo, mid), thr)
        return jnp.where(ge, mid, lo), jnp.where(ge, hi, mid), thr, done + finished

    state = lax.fori_loop(0, VALUE_STEPS, value_step, state)

    def refine(s):
        it = s[4]
        st = snap_step(s[:4])
        st = lax.cond(it >= SNAP_ONLY_TRIPS, key_step, lambda x: x, st)
        return (*st, it + 1)

    return lax.while_loop(lambda s: jnp.logical_and(jnp.min(s[3]) == 0, s[4] < SNAP_ONLY_TRIPS + 40), refine,
                          (*state, jnp.int32(0)))[2]


def _dsa_prompt_kernel(topk, qi_ref, wi_ref, ki_ref, q_ref, k_ref, v_ref, tz_ref, o_ref,
                       sc_scr, s_scr, acc_scr):
    qb = pl.program_id(1)
    tq = q_ref.shape[2]
    ch = ATT_CHUNK
    n_ch = qb + 1

    def chunk_at(j):
        return pl.ds(pl.multiple_of(j * ch, ch), ch)

    qi_all = jnp.concatenate([qi_ref[0, h * IDX_DIM:(h + 1) * IDX_DIM, :] for h in range(IDX_HEADS)], axis=1)
    w_rows = [wi_ref[0, h:h + 1, :] for h in range(IDX_HEADS)]

    def score_chunk(j, diagonal):
        d = jnp.dot(ki_ref[0, chunk_at(j), :], qi_all, preferred_element_type=F32)
        acc = w_rows[0] * jnp.maximum(d[:, 0:tq], 0.0)
        for h in range(1, IDX_HEADS):
            acc = acc + w_rows[h] * jnp.maximum(d[:, h * tq:(h + 1) * tq], 0.0)
        lo_src = acc
        if diagonal:
            causal = (lax.broadcasted_iota(jnp.int32, (ch, tq), 0) <= lax.broadcasted_iota(jnp.int32, (ch, tq), 1))
            acc, lo_src = jnp.where(causal, acc, -jnp.inf), jnp.where(causal, acc, jnp.inf)
        sc_scr[chunk_at(j), :] = acc
        return _fold(acc, SCAN_FOLD, jnp.max), _fold(lo_src, SCAN_FOLD, jnp.min)

    def score_chunks(jj, c, width):
        hi_acc, lo_acc = c
        for j in [jj * width + i for i in range(width)]:
            hi_j, lo_j = score_chunk(j, False)
            hi_acc, lo_acc = jnp.maximum(hi_acc, hi_j), jnp.minimum(lo_acc, lo_j)
        return hi_acc, lo_acc

    ext = (jnp.full((SCAN_FOLD, tq), -jnp.inf, F32), jnp.full((SCAN_FOLD, tq), jnp.inf, F32))
    ext = lax.fori_loop(0, qb // 2, functools.partial(score_chunks, width=2), ext)
    ext = lax.fori_loop(qb // 2 * 2, qb, functools.partial(score_chunks, width=1), ext)
    hi_d, lo_d = score_chunk(qb, True)
    score_max = jnp.max(jnp.maximum(ext[0], hi_d), axis=0, keepdims=True)
    score_min = jnp.min(jnp.minimum(ext[1], lo_d), axis=0, keepdims=True)

    def chunks_reduce(fn, op, init):
        def body(j, acc):
            rows = pl.ds(pl.multiple_of(j * (2 * ch), 2 * ch), 2 * ch)
            return op(jnp.stack([acc, _fold(fn(sc_scr[rows, :]), SCAN_FOLD, op)]), axis=0)
        return op(lax.fori_loop(0, (n_ch + 1) // 2, body, jnp.full((SCAN_FOLD, tq), init, F32)), axis=0, keepdims=True)

    @pl.when(n_ch % 2 == 1)
    def _():
        sc_scr[chunk_at(n_ch), :] = jnp.full((ch, tq), -jnp.inf, F32)

    def count(cmp, thr):
        return chunks_reduce(lambda t: jnp.where(cmp(t, thr), 1.0, 0.0), jnp.sum, 0.0)

    count_ge = functools.partial(count, lambda t, x: t >= x)

    n_valid = qb * tq + lax.broadcasted_iota(jnp.int32, (1, tq), 1) + 1
    k_row = jnp.minimum(n_valid, topk).astype(F32)
    thr = _kth_largest(chunks_reduce, k_row, n_valid <= topk, score_min, score_max)

    surplus = jnp.max(count_ge(thr) - k_row)

    @pl.when(surplus <= 0.0)
    def _():
        def select_chunk(j, c):
            sc_scr[chunk_at(j), :] = jnp.where(sc_scr[chunk_at(j), :] >= thr, 0.0, NEG_BIG)
            return c
        lax.fori_loop(0, n_ch, select_chunk, 0)

    @pl.when(surplus > 0.0)
    def _():
        need = k_row - count(lambda t, x: t > x, thr)
        lower = jnp.where(lax.broadcasted_iota(jnp.int32, (ch, ch), 1) <= lax.broadcasted_iota(jnp.int32, (ch, ch), 0),
                          1.0, 0.0).astype(BF16)
        ones_r = jnp.ones((SUBLANES, ch), BF16)

        def select_chunk(j, run):
            t = sc_scr[chunk_at(j), :]
            eq = t == thr
            eqb = jnp.where(eq, 1.0, 0.0).astype(BF16)
            rank = jnp.dot(lower, eqb, preferred_element_type=F32) + run
            tie_bias = jnp.where(rank <= need, 0.0, NEG_BIG)
            sc_scr[chunk_at(j), :] = jnp.where(t > thr, 0.0, jnp.where(eq, tie_bias, NEG_BIG))
            return run + jnp.dot(ones_r, eqb, preferred_element_type=F32)[0:1, :]

        lax.fori_loop(0, n_ch, select_chunk, jnp.zeros((1, tq), F32))

    scale = HEAD_DIM ** -0.5 * LOG2_E
    cols = KV_REP * tq
    n_far = jnp.maximum(qb - 1, 0)
    for gp in range(N_KV_HEADS // 2):
        groups = (2 * gp, 2 * gp + 1)
        q_cols = [jnp.concatenate([q_ref[0, (g * KV_REP + r) * HEAD_DIM:(g * KV_REP + r + 1) * HEAD_DIM, :]
                                   for r in range(KV_REP)], axis=1) for g in groups]

        def logits_chunks(j0, mx, width, near):
            mx = list(mx)
            for j in [j0 + i for i in range(width)]:
                selb = sc_scr[chunk_at(j), :]
                for gi, g in enumerate(groups):
                    s = jnp.dot(k_ref[0, chunk_at(j), g * HEAD_DIM:(g + 1) * HEAD_DIM], q_cols[gi],
                                preferred_element_type=F32) * scale
                    if near:
                        bias = jnp.concatenate([tz_ref[qb - j, g * KV_REP + r] + selb for r in range(KV_REP)], axis=1)
                    else:
                        bias = jnp.concatenate([selb] * KV_REP, axis=1)
                    s = s + bias
                    s_scr[gi, chunk_at(j), :] = s
                    mx[gi] = jnp.maximum(mx[gi], _fold(s, ATT_FOLD, jnp.max))
            return tuple(mx)

        mx = (jnp.full((ATT_FOLD, cols), NEG_BIG, F32),) * 2
        mx = _chunk_trips(functools.partial(logits_chunks, near=False), 0, n_far, mx, TRIP_WIDTHS)
        mx = _chunk_trips(functools.partial(logits_chunks, near=True), n_far, n_ch, mx, (1,))
        m_row = [jnp.max(m, axis=0, keepdims=True) for m in mx]
        acc_scr[...] = jnp.zeros_like(acc_scr)

        def weights_chunks(j0, l8, width):
            l8 = list(l8)
            for j in [j0 + i for i in range(width)]:
                for gi, g in enumerate(groups):
                    p = jnp.exp2(s_scr[gi, chunk_at(j), :] - m_row[gi])
                    l8[gi] = l8[gi] + _fold(p, ATT_FOLD, jnp.sum)
                    acc_scr[gi] += jnp.dot(v_ref[0, g * HEAD_DIM:(g + 1) * HEAD_DIM, chunk_at(j)], p.astype(BF16),
                                           preferred_element_type=F32)
            return tuple(l8)

        l8 = (jnp.zeros((ATT_FOLD, cols), F32),) * 2
        l8 = _chunk_trips(weights_chunks, 0, n_ch, l8, TRIP_WIDTHS)
        for gi, g in enumerate(groups):
            out = acc_scr[gi] * (1.0 / jnp.sum(l8[gi], axis=0, keepdims=True))
            for r in range(KV_REP):
                h = g * KV_REP + r
                o_ref[0, :, h * HEAD_DIM:(h + 1) * HEAD_DIM] = out[:, r * tq:(r + 1) * tq].T.astype(o_ref.dtype)


def _dsa_prompt(qi_t, wi_t, ki_b, q_t, k_b, v_t, tz, topk):
    b_, s_, _ = k_b.shape
    tq = min(Q_BLOCK, s_)
    assert tq == ATT_CHUNK and s_ % (2 * tq) == 0
    once = pl.Buffered(1)
    q_tile = lambda rows: pl.BlockSpec((1, rows, tq), lambda b, i: (b, 0, i))
    return pl.pallas_call(
        functools.partial(_dsa_prompt_kernel, topk),
        grid=(b_, s_ // tq),
        in_specs=[
            q_tile(IDX_HEADS * IDX_DIM),
            q_tile(IDX_HEADS),
            pl.BlockSpec((1, s_, IDX_DIM), lambda b, i: (b, 0, 0), pipeline_mode=once),
            q_tile(D_ATT),
            pl.BlockSpec((1, s_, D_KV), lambda b, i: (b, 0, 0), pipeline_mode=once),
            pl.BlockSpec((1, D_KV, s_), lambda b, i: (b, 0, 0), pipeline_mode=once),
            _resident(tz.shape),
        ],
        out_specs=pl.BlockSpec((1, tq, D_ATT), lambda b, i: (b, i, 0)),
        out_shape=jax.ShapeDtypeStruct((b_, s_, D_ATT), BF16),
        scratch_shapes=[
            pltpu.VMEM((s_, tq), F32),
            pltpu.VMEM((2, s_, KV_REP * tq), F32),
            pltpu.VMEM((2, HEAD_DIM, KV_REP * tq), F32),
        ],
        compiler_params=_cparams("parallel", "arbitrary"),
        name="dsa_prompt",
    )(qi_t, wi_t, ki_b, q_t, k_b, v_t, tz)


def _mix_and_norm(att_ref, ssd_ref, x_ref, g1_ref, sc2_ref, sh2_ref, n2_ref, wo_ref):
    o = (jnp.dot(att_ref[0].astype(BF16), wo_ref[:D_ATT, :], preferred_element_type=F32)
         + jnp.dot(ssd_ref[0].astype(BF16), wo_ref[D_ATT:, :], preferred_element_type=F32))
    x1 = x_ref[0] + g1_ref[0] * o
    h2 = ((_rms(x1) * n2_ref[...]) * (1.0 + sc2_ref[0]) + sh2_ref[0]).astype(BF16)
    return x1, h2


def _ffn_prompt_kernel(att_ref, ssd_ref, x_ref, g1_ref, sc2_ref, sh2_ref, g2_ref, n2_ref, wo_ref, wu_ref,
                       cw_ref, cb_ref, wd_ref, y_o, tail_o, a_scr, ug_scr, uv_scr, carry_scr):
    i = pl.program_id(1)
    tm = x_ref.shape[1]
    halo = SUBLANES

    @pl.when(i == 0)
    def _():
        carry_scr[...] = jnp.zeros_like(carry_scr)

    x1, h2 = _mix_and_norm(att_ref, ssd_ref, x_ref, g1_ref, sc2_ref, sh2_ref, n2_ref, wo_ref)

    def conv_cols(scr, c0):
        u = jnp.dot(h2, wu_ref[:, c0:c0 + FF_CHUNK], preferred_element_type=F32)
        scr[0:halo, :] = carry_scr[:, c0:c0 + FF_CHUNK]
        scr[halo:halo + tm, :] = u
        carry_scr[:, c0:c0 + FF_CHUNK] = u[tm - halo:, :]
        ue = scr[...]
        taps = cw_ref[0:1, c0:c0 + FF_CHUNK] * ue
        for j in range(1, FFN_CONV):
            taps = cw_ref[j:j + 1, c0:c0 + FF_CHUNK] * ue + pltpu.roll(taps, 1, axis=0)
        return cb_ref[:, c0:c0 + FF_CHUNK] + taps[halo:, :]

    for jc in range(D_FF // FF_CHUNK):
        gate = conv_cols(ug_scr.at[jc % 2], jc * FF_CHUNK)
        val = conv_cols(uv_scr.at[jc % 2], D_FF + jc * FF_CHUNK)
        a_scr[:, jc * FF_CHUNK:(jc + 1) * FF_CHUNK] = (_silu(gate) * val).astype(BF16)

    y_o[0] = x1 + g2_ref[0] * jnp.dot(a_scr[...], wd_ref[...], preferred_element_type=F32)
    tail_o[0] = carry_scr[...]


def _ffn_prompt(att, ssd, x, g1, sc2, sh2, g2, norm2_g, wo_b, wu_b, cw, cb, wd_b):
    b_, s_, _ = x.shape
    tm = min(ROW_TILE, s_)
    nt = s_ // tm
    row = lambda w: pl.BlockSpec((1, tm, w), lambda b, i: (b, i, 0))
    mod = pl.BlockSpec((1, 1, D_MODEL), lambda b, i: (b, 0, 0))
    return pl.pallas_call(
        _ffn_prompt_kernel,
        grid=(b_, nt),
        in_specs=[row(D_ATT), row(D_SSM), row(D_MODEL), mod, mod, mod, mod, _resident((1, D_MODEL)),
                  _resident((D_MIX, D_MODEL)), _resident((D_MODEL, 2 * D_FF)), _resident((FFN_CONV, 2 * D_FF)),
                  _resident((1, 2 * D_FF)), _resident((D_FF, D_MODEL))],
        out_specs=[row(D_MODEL), pl.BlockSpec((1, SUBLANES, 2 * D_FF), lambda b, i: (b, 0, 0))],
        out_shape=[jax.ShapeDtypeStruct((b_, s_, D_MODEL), F32),
                   jax.ShapeDtypeStruct((b_, SUBLANES, 2 * D_FF), F32)],
        scratch_shapes=[pltpu.VMEM((tm, D_FF), BF16), pltpu.VMEM((2, tm + SUBLANES, FF_CHUNK), F32),
                        pltpu.VMEM((2, tm + SUBLANES, FF_CHUNK), F32), pltpu.VMEM((SUBLANES, 2 * D_FF), F32)],
        compiler_params=_cparams("parallel", "arbitrary"),
        name="ffn_prompt",
    )(att, ssd, x, g1, sc2, sh2, g2, norm2_g.reshape(1, D_MODEL), wo_b, wu_b, cw, cb, wd_b)


def _ffn_sample_kernel(att_ref, ssd_ref, x_ref, g1_ref, sc2_ref, sh2_ref, g2_ref, n2_ref, wo_ref, wu_ref,
                       cw_ref, cb_ref, wd_ref, h0_ref, h1_ref, y_o, u_o, acc_scr):
    x1, h2 = _mix_and_norm(att_ref, ssd_ref, x_ref, g1_ref, sc2_ref, sh2_ref, n2_ref, wo_ref)
    acc_scr[...] = jnp.zeros_like(acc_scr)

    def conv_cols(c0):
        u = jnp.dot(h2, wu_ref[:, c0:c0 + FF_CHUNK], preferred_element_type=F32)
        u_o[:, c0:c0 + FF_CHUNK] = u
        sl = slice(c0, c0 + FF_CHUNK)
        return (cb_ref[:, sl] + cw_ref[0:1, sl] * h0_ref[:, sl] + cw_ref[1:2, sl] * h1_ref[:, sl]
                + cw_ref[2:3, sl] * u)

    for jc in range(D_FF // FF_CHUNK):
        gate = conv_cols(jc * FF_CHUNK)
        val = conv_cols(D_FF + jc * FF_CHUNK)
        a = (_silu(gate) * val).astype(BF16)
        acc_scr[...] += jnp.dot(a, wd_ref[jc * FF_CHUNK:(jc + 1) * FF_CHUNK, :], preferred_element_type=F32)

    y_o[0] = x1 + g2_ref[0] * acc_scr[...]


def _ffn_sample(att, ssd, x, g1, sc2, sh2, g2, norm2_g, wo_b, wu_b, cw, cb, wd_b, hist0, hist1):
    n = x.shape[1]
    full = lambda *shape: pl.BlockSpec(shape, lambda: (0,) * len(shape))
    r3 = lambda w: full(1, n, w)
    return pl.pallas_call(
        _ffn_sample_kernel,
        in_specs=[r3(D_ATT), r3(D_SSM), r3(D_MODEL), r3(D_MODEL), r3(D_MODEL), r3(D_MODEL), r3(D_MODEL),
                  full(1, D_MODEL), full(D_MIX, D_MODEL), full(D_MODEL, 2 * D_FF), full(FFN_CONV, 2 * D_FF),
                  full(1, 2 * D_FF), full(D_FF, D_MODEL), full(n, 2 * D_FF), full(n, 2 * D_FF)],
        out_specs=[r3(D_MODEL), full(n, 2 * D_FF)],
        out_shape=[jax.ShapeDtypeStruct((1, n, D_MODEL), F32), jax.ShapeDtypeStruct((n, 2 * D_FF), F32)],
        scratch_shapes=[pltpu.VMEM((n, D_MODEL), F32)],
        compiler_params=pltpu.CompilerParams(vmem_limit_bytes=VMEM_LIMIT),
        name="ffn_sample",
    )(att, ssd, x, g1, sc2, sh2, g2, norm2_g.reshape(1, D_MODEL), wo_b, wu_b, cw, cb, wd_b, hist0, hist1)


def _ssd_sample_kernel(xbc_ref, h0_ref, h1_ref, h2_ref, dt_ref, zg_ref, cw_ref, cb_ref, dtb_ref, alog_ref,
                       dsk_ref, ng_ref, ex_ref, st_ref, y_o, st_o):
    nb = xbc_ref.shape[0]
    conv = (cb_ref[...] + cw_ref[0:1, :] * h0_ref[...] + cw_ref[1:2, :] * h1_ref[...]
            + cw_ref[2:3, :] * h2_ref[...] + cw_ref[3:4, :] * xbc_ref[...])
    xc = _silu(conv)
    xs = xc[:, :D_SSM]
    bm = xc[:, D_SSM:D_SSM + SSM_GROUPS * SSM_STATE]
    cm = xc[:, D_SSM + SSM_GROUPS * SSM_STATE:]
    dt = _softplus(dt_ref[...] + dtb_ref[...])
    dec = jnp.exp(dt * (-jnp.exp(alog_ref[...])))
    ex = ex_ref[...]
    xdt = xs * _dot_sel_rhs(dt, ex)
    dec_x = _dot_sel_rhs(dec, ex)
    gw = SSM_HEADS // SSM_GROUPS * SSM_HEAD_DIM
    ones_n = jnp.ones((SUBLANES, SSM_STATE), BF16)
    row_id = lax.broadcasted_iota(jnp.int32, (SUBLANES, 1), 0)

    def stack(rows):
        out = jnp.zeros((SUBLANES, rows[0].shape[1]), F32)
        for r, v in enumerate(rows):
            out = jnp.where(row_id == r, v.astype(F32), out)
        return out.astype(BF16)

    ys = []
    for i in range(nb):
        yrow = []
        for g in range(SSM_GROUPS):
            hs = st_ref[i, g * gw:(g + 1) * gw, :]
            d0, d1, d2 = _split3(dec_x[i:i + 1, g * gw:(g + 1) * gw])
            x0, x1, _ = _split3(xdt[i:i + 1, g * gw:(g + 1) * gw])
            b0, b1, _ = _split3(bm[i:i + 1, g * SSM_STATE:(g + 1) * SSM_STATE])
            c0, c1, _ = _split3(cm[i:i + 1, g * SSM_STATE:(g + 1) * SSM_STATE])
            dec_b = _dot_tn(stack([d0, d1, d2]), ones_n)
            upd = _dot_tn(stack([x0, x0, x1, x1]), stack([b0, b1, b0, b1]))
            hn = hs * dec_b + upd
            st_o[i, g * gw:(g + 1) * gw, :] = hn
            h0, h1, _ = _split3(hn)
            c_rows = stack([c0, c1])
            yg = _dot_nt(c_rows, h0) + _dot_nt(c_rows, h1)
            yrow.append(yg[0:1, :] + yg[1:2, :])
        ys.append(jnp.concatenate(yrow, axis=-1))
    y = jnp.concatenate(ys, axis=0) + dsk_ref[...] * xs
    y = y * _silu(zg_ref[...])
    y_o[...] = _rms(y) * ng_ref[...]


def _ssd_sample(xbc, hist, dt, zg, conv_w, conv_b, dtb_pad, alog_pad, dsk_x, norm_g, ex, state):
    n = xbc.shape[0]
    nb = SUBLANES
    rows = lambda w: pl.BlockSpec((nb, w), lambda i: (i, 0))
    st = pl.BlockSpec((nb, D_SSM, SSM_STATE), lambda i: (i, 0, 0))
    return pl.pallas_call(
        _ssd_sample_kernel,
        grid=(n // nb,),
        in_specs=[rows(CONV_DIM), rows(CONV_DIM), rows(CONV_DIM), rows(CONV_DIM), rows(LANES), rows(D_SSM),
                  _resident((SSM_CONV, CONV_DIM)), _resident((1, CONV_DIM)), _resident((1, LANES)),
                  _resident((1, LANES)), _resident((1, D_SSM)), _resident((1, D_SSM)), _resident((LANES, D_SSM)), st],
        out_specs=[rows(D_SSM), st],
        out_shape=[jax.ShapeDtypeStruct((n, D_SSM), F32), jax.ShapeDtypeStruct((n, D_SSM, SSM_STATE), F32)],
        compiler_params=_cparams("parallel"),
        name="ssd_sample",
    )(xbc, hist[0], hist[1], hist[2], dt, zg, conv_w, conv_b, dtb_pad, alog_pad, dsk_x, norm_g, ex, state)


def _idx_sample_kernel(pt_ref, qi_ref, wi_ref, kin_ref, pool_ref, o_ref, kbuf, sem):
    b = pl.program_id(0)
    nb = pl.num_programs(0)
    n_pages = kbuf.shape[1]
    page = kbuf.shape[3]

    def page_copy(seq, slot, j):
        return pltpu.make_async_copy(pool_ref.at[pt_ref[seq, j]], kbuf.at[slot, j], sem.at[slot])

    def fetch(seq, slot):
        def body(j, c):
            page_copy(seq, slot, j).start()
            return c
        lax.fori_loop(0, n_pages, body, 0, unroll=True)

    @pl.when(b == 0)
    def _():
        fetch(0, 0)

    @pl.when(b + 1 < nb)
    def _():
        fetch(b + 1, (b + 1) % 2)

    slot = b % 2

    def wait_body(j, c):
        page_copy(b, slot, j).wait()
        return c
    lax.fori_loop(0, n_pages, wait_body, 0, unroll=True)

    qi = qi_ref[0]
    w = wi_ref[0]

    span = PAGES_PER_DOT * page

    def page_scores(j, c):
        keys = jnp.concatenate([kbuf[slot, PAGES_PER_DOT * j + i] for i in range(PAGES_PER_DOT)], axis=1)
        d = jnp.dot(qi, keys.astype(BF16), preferred_element_type=F32)
        o_ref[0, :, pl.ds(pl.multiple_of(j * span, span), span)] = jnp.sum(w * jnp.maximum(d, 0.0), axis=0, keepdims=True)
        return c

    lax.fori_loop(0, n_pages // PAGES_PER_DOT, page_scores, 0, unroll=True)
    knew = kin_ref[0].astype(BF16).astype(F32)
    dn = jnp.sum(qi.astype(F32) * knew, axis=-1, keepdims=True)
    s_new = jnp.sum(w * jnp.maximum(dn, 0.0), axis=0, keepdims=True)
    lane = lax.broadcasted_iota(jnp.int32, (1, SCORE_PAD), 1)
    o_ref[0, :, n_pages * page:] = jnp.where(lane == 0, s_new, -jnp.inf)


def _idx_sample(page_table, qi3, wi3, ki_new3, pool_t):
    n, n_pages = page_table.shape
    page = pool_t.shape[2]
    assert n_pages % (8 * PAGES_PER_DOT) == 0
    past = n_pages * page
    grid_spec = pltpu.PrefetchScalarGridSpec(
        num_scalar_prefetch=1,
        grid=(n,),
        in_specs=[
            pl.BlockSpec((1, IDX_HEADS, IDX_DIM), lambda b, pt: (b, 0, 0)),
            pl.BlockSpec((1, IDX_HEADS, 1), lambda b, pt: (b, 0, 0)),
            pl.BlockSpec((1, 1, IDX_DIM), lambda b, pt: (b, 0, 0)),
            pl.BlockSpec(memory_space=pl.ANY),
        ],
        out_specs=pl.BlockSpec((1, 1, past + SCORE_PAD), lambda b, pt: (b, 0, 0)),
        scratch_shapes=[pltpu.VMEM((2, n_pages, IDX_DIM, page), F32), pltpu.SemaphoreType.DMA((2,))],
    )
    return pl.pallas_call(
        _idx_sample_kernel,
        grid_spec=grid_spec,
        out_shape=jax.ShapeDtypeStruct((n, 1, past + SCORE_PAD), F32),
        compiler_params=_cparams("arbitrary"),
        name="idx_sample",
    )(page_table, qi3, wi3, ki_new3, pool_t)


def _att_sample_kernel(q_ref, k_ref, v_ref, knew_ref, vnew_ref, bias_ref, o_ref):
    nb = q_ref.shape[0]
    rows = k_ref.shape[1]
    scale = HEAD_DIM ** -0.5
    head_group = lax.broadcasted_iota(jnp.int32, (N_HEADS, 1), 0) // KV_REP

    def per_head(new):
        out = jnp.zeros((N_HEADS, HEAD_DIM), F32)
        for g in range(N_KV_HEADS):
            out = jnp.where(head_group == g, new[g:g + 1, :], out)
        return out.astype(BF16).astype(F32)

    for i in range(nb):
        qb = q_ref[i].astype(BF16)
        s = _dot_nt(qb, k_ref[i].astype(BF16)) * scale + bias_ref[i, :, :rows]
        s_new = (jnp.sum(qb.astype(F32) * per_head(knew_ref[i]), axis=-1, keepdims=True) * scale
                 + bias_ref[i, :, rows:rows + 1])
        m = jnp.maximum(jnp.max(s, axis=-1, keepdims=True), s_new)
        p, p_new = jnp.exp(s - m), jnp.exp(s_new - m)
        inv = 1.0 / (jnp.sum(p, axis=-1, keepdims=True) + p_new)
        o_ref[i] = (jnp.dot((p * inv).astype(BF16), v_ref[i].astype(BF16), preferred_element_type=F32)
                    + (p_new * inv).astype(BF16).astype(F32) * per_head(vnew_ref[i]))


def _att_sample(q3, k_rows, v_rows, k_new, v_new, bias):
    n, rows = k_rows.shape[:2]
    nb = SUBLANES
    gathered = pl.BlockSpec((nb, rows, HEAD_DIM), lambda i: (i, 0, 0))
    new3 = pl.BlockSpec((nb, N_KV_HEADS, HEAD_DIM), lambda i: (i, 0, 0))
    return pl.pallas_call(
        _att_sample_kernel,
        grid=(n // nb,),
        in_specs=[
            pl.BlockSpec((nb, N_HEADS, HEAD_DIM), lambda i: (i, 0, 0)),
            gathered, gathered, new3, new3,
            pl.BlockSpec((nb, N_HEADS, rows + LANES), lambda i: (i, 0, 0)),
        ],
        out_specs=pl.BlockSpec((nb, N_HEADS, HEAD_DIM), lambda i: (i, 0, 0)),
        out_shape=jax.ShapeDtypeStruct((n, N_HEADS, HEAD_DIM), F32),
        compiler_params=_cparams("parallel"),
        name="att_sample",
    )(q3, k_rows, v_rows, k_new, v_new, bias)


def _sel_sample_kernel(k_keep, n_valid, sc_ref, o_ref):
    rows, n = sc_ref.shape
    ch = ATT_CHUNK
    span = 2 * ch

    def scan(fn, op, init):
        def body(j, acc):
            r = pl.ds(pl.multiple_of(j * span, span), span)
            return op(jnp.stack([acc, _fold(fn(sc_ref[r, :]), SCAN_FOLD, op)]), axis=0)
        return op(lax.fori_loop(0, rows // span, body, jnp.full((SCAN_FOLD, n), init, F32)), axis=0, keepdims=True)

    k_row = jnp.full((1, n), float(k_keep), F32)
    thr = _kth_largest(scan, k_row, jnp.full((1, n), n_valid <= k_keep))
    need = k_row - scan(lambda t: jnp.where(t > thr, 1.0, 0.0), jnp.sum, 0.0)
    lower = jnp.where(lax.broadcasted_iota(jnp.int32, (ch, ch), 1) <= lax.broadcasted_iota(jnp.int32, (ch, ch), 0),
                      1.0, 0.0).astype(BF16)
    ones_r = jnp.ones((SUBLANES, ch), BF16)
    key_local = lax.broadcasted_iota(jnp.int32, (ch, n), 0).astype(F32)
    slot_iota = lax.broadcasted_iota(jnp.int32, (k_keep, n), 0).astype(F32)
    none = float(rows)

    def chunk_body(c, carry):
        run, slot, out = carry
        t = sc_ref[pl.ds(pl.multiple_of(c * ch, ch), ch), :]
        eq = t == thr
        eqb = jnp.where(eq, 1.0, 0.0).astype(BF16)
        rank = jnp.dot(lower, eqb, preferred_element_type=F32) + run
        cand = jnp.where(t > thr, key_local, jnp.where(eq, jnp.where(rank <= need, key_local, none), none))
        left = _fold(jnp.where(cand < none, 1.0, 0.0), SUBLANES, jnp.sum).sum(axis=0, keepdims=True)
        base = jnp.asarray(c * ch, F32)

        def extract(_, s):
            cand, out, slot = s
            cur = jnp.min(cand, axis=0, keepdims=True)
            has = cur < none
            out = jnp.where(slot_iota == slot, jnp.where(has, cur + base, out), out)
            return jnp.where(cand == cur, none, cand), out, slot + jnp.where(has, 1.0, 0.0)

        _, out, slot = lax.fori_loop(0, jnp.max(left).astype(jnp.int32), extract, (cand, out, slot))
        return run + jnp.dot(ones_r, eqb, preferred_element_type=F32)[0:1, :], slot, out

    zeros_row = jnp.zeros((1, n), F32)
    _, _, out = lax.fori_loop(0, rows // ch, chunk_body, (zeros_row, zeros_row, jnp.zeros((k_keep, n), F32)))
    o_ref[...] = out.astype(jnp.int32)


def _sel_sample(scores_t, k_keep, n_valid):
    rows, n = scores_t.shape
    assert rows % (2 * ATT_CHUNK) == 0
    return pl.pallas_call(
        functools.partial(_sel_sample_kernel, k_keep, n_valid),
        in_specs=[pl.BlockSpec((rows, n), lambda: (0, 0))],
        out_specs=pl.BlockSpec((k_keep, n), lambda: (0, 0)),
        out_shape=jax.ShapeDtypeStruct((k_keep, n), jnp.int32),
        compiler_params=pltpu.CompilerParams(vmem_limit_bytes=VMEM_LIMIT),
        name="sel_sample",
    )(scores_t)


def _pack_w_in(w_in):
    offs = np.cumsum((0,) + IN_SPLITS)
    parts = [w_in[:, offs[i]:offs[i + 1]] for i in range(len(IN_SPLITS))]
    pad = lambda a: jnp.pad(a, ((0, 0), (0, LANES - a.shape[1])))
    q, k, v, qi, ki, wi, zg, xbc, dt = parts
    return jnp.concatenate([q, k, v, qi, pad(ki), pad(wi), zg, xbc, pad(dt)], axis=1).astype(BF16)


def _t5_bucket(dist):
    dist = jnp.maximum(dist, 0)
    max_exact = N_BUCKETS // 2
    d = jnp.maximum(dist, 1).astype(F32)
    large = max_exact + (jnp.log(d / max_exact) / math.log(MAX_DISTANCE / max_exact) * (N_BUCKETS - max_exact)).astype(jnp.int32)
    large = jnp.minimum(large, N_BUCKETS - 1)
    return jnp.where(dist < max_exact, dist, large)


def _rel_bias_tables(rel_bias, tq):
    assert ATT_CHUNK == tq and tq >= MAX_DISTANCE
    n = 2 * tq
    idx = jnp.arange(n, dtype=jnp.int32)
    query_minus_key = jnp.where(idx < tq, idx, idx - n)
    far = rel_bias[_t5_bucket(jnp.int32(MAX_DISTANCE))]
    tabs = []
    for d0 in (0, tq):
        f = (rel_bias[_t5_bucket(d0 + query_minus_key)] - far[None, :]) * LOG2_E
        flat = jnp.tile(f.T, (1, ATT_CHUNK))
        tabs.append(flat[:, :ATT_CHUNK * (n - 1)].reshape(-1, ATT_CHUNK, n - 1)[:, :, :tq])
    return jnp.stack(tabs)


def _head_expand():
    h = jnp.arange(LANES, dtype=jnp.int32)[:, None]
    c = jnp.arange(D_SSM, dtype=jnp.int32)[None, :]
    ex = (c // SSM_HEAD_DIM == h).astype(BF16)
    return ex, ex.T


def _pad_lanes(v):
    return jnp.pad(v.astype(F32), (0, LANES - v.shape[0])).reshape(1, LANES)


def kernel(x_prompt, x_sample, cache_k, cache_v, cache_kidx, state_ssm, state_conv_ssd, state_conv_ffn, page_table,
           c_prompt, c_sample, rel_bias, norm1_g, w_ada, b_ada, w_in, q_norm_g, k_norm_g, conv_ssd_w, conv_ssd_b,
           dt_bias, a_log, d_skip, ssd_norm_g, w_out, norm2_g, w_up, conv_ffn_w, conv_ffn_b, w_down):
    depth = w_in.shape[0]
    assert depth == 1
    bp, s_, _ = x_prompt.shape
    ns = x_sample.shape[0]
    assert x_sample.shape[1] == 1
    lyr = 0
    n_pages = page_table.shape[1]
    page = cache_k.shape[2]
    past = n_pages * page
    topk_p = min(TOPK_MAX, s_ // TOPK_DIV)
    topk_s = min(TOPK_MAX, (past + 1) // TOPK_DIV)

    w_pack = _pack_w_in(w_in[lyr])
    wo_b = w_out[lyr].astype(BF16)
    wu_b = w_up[lyr].astype(BF16)
    wd_b = w_down[lyr].astype(BF16)
    ex, ext = _head_expand()
    dtb_pad = _pad_lanes(dt_bias[lyr])
    alog_pad = _pad_lanes(a_log[lyr])
    dsk_x = jnp.repeat(d_skip[lyr].astype(F32), SSM_HEAD_DIM).reshape(1, D_SSM)
    ssd_g = ssd_norm_g[lyr].reshape(1, D_SSM)
    cw_ssd, cb_ssd = conv_ssd_w[lyr], conv_ssd_b[lyr].reshape(1, CONV_DIM)
    cw_ffn, cb_ffn = conv_ffn_w[lyr], conv_ffn_b[lyr].reshape(1, 2 * D_FF)
    tz = _rel_bias_tables(rel_bias.astype(F32), min(Q_BLOCK, s_))

    n_c = bp + ns
    c_all = jnp.pad(jnp.concatenate([c_prompt, c_sample], axis=0), ((0, (-n_c) % SUBLANES), (0, 0)))
    mod = _ada(c_all, w_ada[lyr], b_ada[lyr])
    mods_p = [mod[:bp, i * D_MODEL:(i + 1) * D_MODEL].reshape(bp, 1, D_MODEL) for i in range(6)]
    mods_s = [mod[bp:bp + ns, i * D_MODEL:(i + 1) * D_MODEL].reshape(1, ns, D_MODEL) for i in range(6)]

    sh1, sc1, g1, sh2, sc2, g2 = mods_p
    (q_t, k_f, v_f, k_b, v_t, qi_t, ki_f, ki_b, wi_t, zg, xbc, dt) = _inproj(
        x_prompt, norm1_g[lyr], sc1, sh1, w_pack, q_norm_g[lyr], k_norm_g[lyr], rows_per_mod=1, queries_on_lanes=True)
    att_p = _dsa_prompt(qi_t, wi_t, ki_b, q_t, k_b, v_t, tz, topk_p)
    ssd_p, ssm_p = _ssd_prompt(xbc, dt, zg, cw_ssd, cb_ssd, dtb_pad, alog_pad, dsk_x, ssd_g, ex, ext)
    y_p, tail_p = _ffn_prompt(att_p, ssd_p, x_prompt, g1, sc2, sh2, g2, norm2_g[lyr], wo_b, wu_b, cw_ffn, cb_ffn, wd_b)

    k_p = k_f.reshape(1, bp, s_, N_KV_HEADS, HEAD_DIM)
    v_p = v_f.reshape(1, bp, s_, N_KV_HEADS, HEAD_DIM)
    kidx_p = ki_f.reshape(1, bp, s_, IDX_DIM)
    ssm_p = ssm_p.reshape(1, bp, SSM_HEADS, SSM_HEAD_DIM, SSM_STATE)
    cssd_p = xbc[:, s_ - (SSM_CONV - 1):, :].reshape(1, bp, SSM_CONV - 1, CONV_DIM)
    cffn_p = tail_p[:, SUBLANES - (FFN_CONV - 1):, :].reshape(1, bp, FFN_CONV - 1, 2 * D_FF)

    sh1, sc1, g1, sh2, sc2, g2 = mods_s
    xs3 = x_sample.reshape(1, ns, D_MODEL)
    (q_b, k_f, v_f, _, _, qi_b, ki_f, _, wi_f, zg, xbc, dt) = _inproj(
        xs3, norm1_g[lyr], sc1, sh1, w_pack, q_norm_g[lyr], k_norm_g[lyr], rows_per_mod=ns, queries_on_lanes=False)
    k_new, v_new, ki_new = k_f[0], v_f[0], ki_f[0]

    scores = _idx_sample(page_table, qi_b[0].reshape(ns, IDX_HEADS, IDX_DIM), wi_f[0].reshape(ns, IDX_HEADS, 1),
                         ki_new.reshape(ns, 1, IDX_DIM), jnp.swapaxes(cache_kidx[lyr], 1, 2))[:, 0, :]
    idx = _sel_sample(scores.T, topk_s, past + 1).T
    n_rows = topk_s * N_KV_HEADS
    is_new = idx >= past
    pidx = jnp.minimum(idx, past - 1)
    page_hit = (pidx // page)[:, :, None] == jnp.arange(n_pages, dtype=jnp.int32)
    phys = jnp.sum(jnp.where(page_hit, page_table[:, None, :], 0), axis=-1)
    off = pidx % page
    bucket_hit = _t5_bucket(past - idx)[:, None, :, None] == jnp.arange(N_BUCKETS, dtype=jnp.int32)
    bias = jnp.sum(jnp.where(bucket_hit, rel_bias.astype(F32).T[None, :, None, :], 0.0), axis=-1)
    own_group = jnp.arange(N_KV_HEADS)[None, :] == (jnp.arange(N_HEADS) // KV_REP)[:, None]
    use_row = jnp.logical_and(own_group[None, :, None, :], jnp.logical_not(is_new)[:, None, :, None])
    bias_rows = jnp.where(use_row, bias[:, :, :, None], NEG_BIG).reshape(ns, N_HEADS, n_rows)
    bias_new = jnp.where(jnp.any(is_new, axis=1)[:, None], rel_bias[_t5_bucket(jnp.int32(0))].astype(F32)[None, :], NEG_BIG)
    tail = jnp.where(jnp.arange(LANES)[None, None, :] == 0, bias_new[:, :, None], NEG_BIG)
    att_s = _att_sample(q_b[0].astype(F32).reshape(ns, N_HEADS, HEAD_DIM),
                        cache_k[lyr][phys, off].reshape(ns, n_rows, HEAD_DIM),
                        cache_v[lyr][phys, off].reshape(ns, n_rows, HEAD_DIM),
                        k_new.reshape(ns, N_KV_HEADS, HEAD_DIM), v_new.reshape(ns, N_KV_HEADS, HEAD_DIM),
                        jnp.concatenate([bias_rows, tail], axis=-1)).reshape(1, ns, D_ATT)

    hist_ssd = state_conv_ssd[lyr]
    ssd_s, ssm_s = _ssd_sample(xbc[0], [hist_ssd[:, j, :] for j in range(SSM_CONV - 1)], dt[0], zg[0], cw_ssd, cb_ssd,
                               dtb_pad, alog_pad, dsk_x, ssd_g, ex,
                               state_ssm[lyr].reshape(ns, D_SSM, SSM_STATE))
    hist_ffn = state_conv_ffn[lyr]
    y_s, u_s = _ffn_sample(att_s, ssd_s.reshape(1, ns, D_SSM), xs3, g1, sc2, sh2, g2, norm2_g[lyr], wo_b, wu_b,
                           cw_ffn, cb_ffn, wd_b, hist_ffn[:, 0, :], hist_ffn[:, 1, :])

    k_s = k_new.reshape(1, ns, 1, N_KV_HEADS, HEAD_DIM)
    v_s = v_new.reshape(1, ns, 1, N_KV_HEADS, HEAD_DIM)
    kidx_s = ki_new.reshape(1, ns, 1, IDX_DIM)
    ssm_s = ssm_s.reshape(1, ns, SSM_HEADS, SSM_HEAD_DIM, SSM_STATE)
    cssd_s = jnp.concatenate([hist_ssd[:, 1:, :], xbc[0][:, None, :]], axis=1)[None]
    cffn_s = jnp.stack([hist_ffn[:, 1, :], u_s], axis=1)[None]

    return (y_p, y_s.reshape(ns, 1, D_MODEL), k_p, v_p, kidx_p, ssm_p, cssd_p, cffn_p,
            k_s, v_s, kidx_s, ssm_s, cssd_s, cffn_s)
```

```python
import functools
import math

import jax
import jax.numpy as jnp
import numpy as np
from jax import lax
from jax.experimental import pallas as pl
from jax.experimental.pallas import tpu as pltpu

F32 = jnp.float32
BF16 = jnp.bfloat16

D_MODEL = 1024
N_HEADS = 8
N_KV_HEADS = 4
HEAD_DIM = 128
KV_REP = N_HEADS // N_KV_HEADS
IDX_HEADS = 8
IDX_DIM = 64
TOPK_MAX = 256
TOPK_DIV = 4
N_BUCKETS = 32
MAX_DISTANCE = 128
SSM_HEADS = 16
SSM_HEAD_DIM = 64
SSM_GROUPS = 4
SSM_STATE = 128
SSM_CONV = 4
SSD_CHUNK = 128
D_ATT = N_HEADS * HEAD_DIM
D_KV = N_KV_HEADS * HEAD_DIM
D_SSM = SSM_HEADS * SSM_HEAD_DIM
D_MIX = D_ATT + D_SSM
CONV_DIM = D_SSM + 2 * SSM_GROUPS * SSM_STATE
D_FF = 2816
FFN_CONV = 3
EPS = 1e-6
IN_SPLITS = (D_ATT, D_KV, D_KV, IDX_HEADS * IDX_DIM, IDX_DIM, IDX_HEADS, D_SSM, CONV_DIM, SSM_HEADS)

LANES = 128
SUBLANES = 8
VMEM_LIMIT = 52 * 1024 * 1024

OFF_Q = 0
OFF_K = OFF_Q + D_ATT
OFF_V = OFF_K + D_KV
OFF_QI = OFF_V + D_KV
OFF_KI = OFF_QI + IDX_HEADS * IDX_DIM
OFF_WI = OFF_KI + LANES
OFF_ZG = OFF_WI + LANES
OFF_XBC = OFF_ZG + D_SSM
OFF_DT = OFF_XBC + CONV_DIM
N_PACK = OFF_DT + LANES

NEG_BIG = -1e30
LOG2_E = math.log2(math.e)
ATT_CHUNK = 256
FF_CHUNK = 256
ROW_TILE = 256
Q_BLOCK = 256
PAGES_PER_DOT = 4
SCORE_PAD = 2 * ATT_CHUNK
IDX_SLOTS = 4
VALUE_STEPS = 14
SNAP_ONLY_TRIPS = 4
SCAN_FOLD = 64
ATT_FOLD = 16
TRIP_WIDTHS = (4, 2, 1)


def _cparams(*sem):
    return pltpu.CompilerParams(dimension_semantics=sem, vmem_limit_bytes=VMEM_LIMIT)


def _resident(shape):
    nd = len(shape)
    return pl.BlockSpec(shape, lambda *_: (0,) * nd, pipeline_mode=pl.Buffered(1))


def _silu(x):
    return x * (1.0 / (1.0 + jnp.exp(-x)))


def _softplus(x):
    return jnp.maximum(x, 0.0) + jnp.log1p(jnp.exp(-jnp.abs(x)))


def _rms(x):
    return x * lax.rsqrt(jnp.mean(x * x, axis=-1, keepdims=True) + EPS)


def _split3(x):
    a = x.astype(BF16)
    r = x - a.astype(F32)
    b = r.astype(BF16)
    c = (r - b.astype(F32)).astype(BF16)
    return a, b, c


def _dot_sel_rhs(x, sel):
    return sum(jnp.dot(p, sel, preferred_element_type=F32) for p in _split3(x))


def _dot_sel_lhs(sel, x):
    return sum(jnp.dot(sel, p, preferred_element_type=F32) for p in _split3(x))


def _dot_nt(a, b):
    return lax.dot_general(a, b, (((1,), (1,)), ((), ())), preferred_element_type=F32)


def _dot_tn(a, b):
    return lax.dot_general(a, b, (((0,), (0,)), ((), ())), preferred_element_type=F32)


def _ada_kernel(c_ref, w_ref, b_ref, o_ref):
    s = _silu(c_ref[...]).astype(BF16)
    o_ref[...] = jnp.dot(s, w_ref[...].astype(BF16), preferred_element_type=F32) + b_ref[...]


def _ada(c_all, w_ada, b_ada):
    rows = c_all.shape[0]
    n = w_ada.shape[1]
    return pl.pallas_call(
        _ada_kernel,
        grid=(n // D_MODEL,),
        in_specs=[
            pl.BlockSpec((rows, D_MODEL), lambda j: (0, 0)),
            pl.BlockSpec((D_MODEL, D_MODEL), lambda j: (0, j)),
            pl.BlockSpec((1, D_MODEL), lambda j: (0, j)),
        ],
        out_specs=pl.BlockSpec((rows, D_MODEL), lambda j: (0, j)),
        out_shape=jax.ShapeDtypeStruct((rows, n), F32),
        compiler_params=_cparams("arbitrary"),
        name="ada_mod",
    )(c_all, w_ada, b_ada.reshape(1, n))


def _inproj_kernel(queries_on_lanes, x_ref, g_ref, sc_ref, sh_ref, w_ref, qg_ref, kg_ref,
                   q_o, k_o, v_o, kb_o, vb_o, qi_o, ki_o, kib_o, wi_o, zg_o, xbc_o, dt_o):
    x = x_ref[0]
    h = _rms(x) * g_ref[...]
    hb = (h * (1.0 + sc_ref[0]) + sh_ref[0]).astype(BF16)

    def sec(a, b):
        return jnp.dot(hb, w_ref[:, a:b], preferred_element_type=F32)

    zq = sec(OFF_Q, OFF_K)
    for i in range(N_HEADS):
        qn = _rms(zq[:, i * HEAD_DIM:(i + 1) * HEAD_DIM]) * qg_ref[...]
        if queries_on_lanes:
            q_o[0, i * HEAD_DIM:(i + 1) * HEAD_DIM, :] = qn.T.astype(BF16)
        else:
            q_o[0, :, i * HEAD_DIM:(i + 1) * HEAD_DIM] = qn.astype(BF16)
    zk = sec(OFF_K, OFF_V)
    for i in range(N_KV_HEADS):
        zh = zk[:, i * HEAD_DIM:(i + 1) * HEAD_DIM]
        kn = _rms(zh) * kg_ref[...]
        k_o[0, :, i, :] = kn
        kb_o[0, :, i * HEAD_DIM:(i + 1) * HEAD_DIM] = kn.astype(BF16)
    zv = sec(OFF_V, OFF_QI)
    for i in range(N_KV_HEADS):
        v_o[0, :, i, :] = zv[:, i * HEAD_DIM:(i + 1) * HEAD_DIM]
    zqi = sec(OFF_QI, OFF_KI)
    zwi = sec(OFF_WI, OFF_ZG)
    if queries_on_lanes:
        vb_o[0] = zv.T.astype(BF16)
        qi_o[0] = zqi.T.astype(BF16)
        wi_o[0] = zwi.T[:IDX_HEADS, :]
    else:
        vb_o[0] = zv.astype(BF16)
        qi_o[0] = zqi.astype(BF16)
        wi_o[0] = zwi[:, :IDX_HEADS]
    zki = sec(OFF_KI, OFF_WI)[:, :IDX_DIM]
    ki_o[0] = zki
    kib_o[0] = zki.astype(BF16)
    zg_o[0] = sec(OFF_ZG, OFF_XBC)
    xbc_o[0] = sec(OFF_XBC, OFF_DT)
    dt_o[0] = sec(OFF_DT, N_PACK)


def _inproj(x, norm_g, sc, sh, w_pack, qg, kg, rows_per_mod, queries_on_lanes):
    g_, r_, _ = x.shape
    tm = min(ROW_TILE, r_)
    nt = r_ // tm
    mod_rows = 1 if rows_per_mod == 1 else tm
    mod_map = (lambda b, i: (b, 0, 0)) if rows_per_mod == 1 else (lambda b, i: (b, i, 0))
    outs = [(D_ATT, BF16, True), (D_KV, F32, False), (D_KV, F32, False), (D_KV, BF16, False), (D_KV, BF16, True),
            (IDX_HEADS * IDX_DIM, BF16, True), (IDX_DIM, F32, False), (IDX_DIM, BF16, False), (IDX_HEADS, F32, True),
            (D_SSM, F32, False), (CONV_DIM, F32, False), (LANES, F32, False)]
    out_specs, out_shape = [], []
    for n_out, (w, dt, tr) in enumerate(outs):
        if n_out in (1, 2):
            out_specs.append(pl.BlockSpec((1, tm, N_KV_HEADS, HEAD_DIM), lambda b, i: (b, i, 0, 0)))
            out_shape.append(jax.ShapeDtypeStruct((g_, r_, N_KV_HEADS, HEAD_DIM), dt))
        elif tr and queries_on_lanes:
            out_specs.append(pl.BlockSpec((1, w, tm), lambda b, i: (b, 0, i)))
            out_shape.append(jax.ShapeDtypeStruct((g_, w, r_), dt))
        else:
            out_specs.append(pl.BlockSpec((1, tm, w), lambda b, i: (b, i, 0)))
            out_shape.append(jax.ShapeDtypeStruct((g_, r_, w), dt))
    return pl.pallas_call(
        functools.partial(_inproj_kernel, queries_on_lanes),
        grid=(g_, nt),
        in_specs=[
            pl.BlockSpec((1, tm, D_MODEL), lambda b, i: (b, i, 0)),
            _resident((1, D_MODEL)),
            pl.BlockSpec((1, mod_rows, D_MODEL), mod_map),
            pl.BlockSpec((1, mod_rows, D_MODEL), mod_map),
            _resident((D_MODEL, N_PACK)),
            _resident((1, HEAD_DIM)),
            _resident((1, HEAD_DIM)),
        ],
        out_specs=out_specs,
        out_shape=out_shape,
        compiler_params=_cparams("parallel", "parallel"),
        name="in_proj",
    )(x, norm_g.reshape(1, D_MODEL), sc, sh, w_pack, qg.reshape(1, HEAD_DIM), kg.reshape(1, HEAD_DIM))


def _ssd_prompt_kernel(xbc_ref, dt_ref, zg_ref, cw_ref, cb_ref, dtb_ref, alog_ref, dsk_ref, ng_ref,
                       ex_ref, ext_ref, y_o, st_o, xe_scr, st_scr):
    c = pl.program_id(1)
    q_ = SSD_CHUNK
    halo = SUBLANES

    @pl.when(c == 0)
    def _():
        xe_scr[0:halo, :] = jnp.zeros((halo, CONV_DIM), F32)
        st_scr[...] = jnp.zeros_like(st_scr)

    xe_scr[halo:halo + q_, :] = xbc_ref[0]
    xe = xe_scr[...]
    taps = cw_ref[0:1, :] * xe
    for j in range(1, SSM_CONV):
        taps = cw_ref[j:j + 1, :] * xe + pltpu.roll(taps, 1, axis=0)
    conv = cb_ref[...] + taps[halo:halo + q_, :]
    xe_scr[0:halo, :] = xe_scr[q_:q_ + halo, :]
    xc = _silu(conv)
    xs = xc[:, :D_SSM]
    bm = xc[:, D_SSM:D_SSM + SSM_GROUPS * SSM_STATE]
    cm = xc[:, D_SSM + SSM_GROUPS * SSM_STATE:]

    dt = _softplus(dt_ref[0] + dtb_ref[...])
    da = dt * (-jnp.exp(alog_ref[...]))
    row = lax.broadcasted_iota(jnp.int32, (q_, q_), 0)
    col = lax.broadcasted_iota(jnp.int32, (q_, q_), 1)
    causal = row >= col
    tri = jnp.where(causal, 1.0, 0.0).astype(BF16)
    acs = _dot_sel_lhs(tri, da)
    acs_t = acs.T
    acs_last = acs[q_ - 1:q_, :]
    ex = ex_ref[...]
    dt_x = _dot_sel_rhs(dt, ex)
    eacs_x = _dot_sel_rhs(jnp.exp(acs), ex)
    dte_x = _dot_sel_rhs(jnp.exp(acs_last - acs), ex)
    cdec = jnp.exp(jnp.broadcast_to(acs_t[:, q_ - 1:q_], (LANES, SSM_STATE)))
    cdec_rows = _dot_sel_lhs(ext_ref[...], cdec)

    xdt = xs * dt_x
    xdtd = xdt * dte_x
    gw = SSM_HEADS // SSM_GROUPS * SSM_HEAD_DIM
    gated = []
    for g in range(SSM_GROUPS):
        bg = bm[:, g * SSM_STATE:(g + 1) * SSM_STATE].astype(BF16)
        cg = cm[:, g * SSM_STATE:(g + 1) * SSM_STATE].astype(BF16)
        cb = _dot_nt(cg, bg)
        yd = []
        for hl in range(SSM_HEADS // SSM_GROUPS):
            h = g * (SSM_HEADS // SSM_GROUPS) + hl
            seg = acs[:, h:h + 1] - acs_t[h:h + 1, :]
            lmat = jnp.exp(jnp.where(causal, seg, -jnp.inf))
            yd.append(jnp.dot((cb * lmat).astype(BF16),
                              xdt[:, h * SSM_HEAD_DIM:(h + 1) * SSM_HEAD_DIM].astype(BF16),
                              preferred_element_type=F32))
        y_diag = jnp.concatenate(yd, axis=-1)
        st_old = st_scr[g * gw:(g + 1) * gw, :]
        y_off = _dot_nt(cg, st_old.astype(BF16)) * eacs_x[:, g * gw:(g + 1) * gw]
        st_new = _dot_tn(xdtd[:, g * gw:(g + 1) * gw].astype(BF16), bg)
        st_scr[g * gw:(g + 1) * gw, :] = st_old * cdec_rows[g * gw:(g + 1) * gw, :] + st_new
        yg = y_diag + y_off + dsk_ref[:, g * gw:(g + 1) * gw] * xs[:, g * gw:(g + 1) * gw]
        gated.append(yg * _silu(zg_ref[0, :, g * gw:(g + 1) * gw]))

    y_o[0] = (_rms(jnp.concatenate(gated, axis=-1)) * ng_ref[...]).astype(y_o.dtype)

    @pl.when(c == pl.num_programs(1) - 1)
    def _():
        st_o[0] = st_scr[...]


def _ssd_prompt(xbc, dt, zg, conv_w, conv_b, dtb_pad, alog_pad, dsk_x, norm_g, ex, ext):
    b_, s_, _ = xbc.shape
    nc = s_ // SSD_CHUNK
    return pl.pallas_call(
        _ssd_prompt_kernel,
        grid=(b_, nc),
        in_specs=[
            pl.BlockSpec((1, SSD_CHUNK, CONV_DIM), lambda b, c: (b, c, 0)),
            pl.BlockSpec((1, SSD_CHUNK, LANES), lambda b, c: (b, c, 0)),
            pl.BlockSpec((1, SSD_CHUNK, D_SSM), lambda b, c: (b, c, 0)),
            _resident((SSM_CONV, CONV_DIM)),
            _resident((1, CONV_DIM)),
            _resident((1, LANES)),
            _resident((1, LANES)),
            _resident((1, D_SSM)),
            _resident((1, D_SSM)),
            _resident((LANES, D_SSM)),
            _resident((D_SSM, LANES)),
        ],
        out_specs=[
            pl.BlockSpec((1, SSD_CHUNK, D_SSM), lambda b, c: (b, c, 0)),
            pl.BlockSpec((1, D_SSM, SSM_STATE), lambda b, c: (b, 0, 0)),
        ],
        out_shape=[
            jax.ShapeDtypeStruct((b_, s_, D_SSM), BF16),
            jax.ShapeDtypeStruct((b_, D_SSM, SSM_STATE), F32),
        ],
        scratch_shapes=[
            pltpu.VMEM((SSD_CHUNK + SUBLANES, CONV_DIM), F32),
            pltpu.VMEM((D_SSM, SSM_STATE), F32),
        ],
        compiler_params=_cparams("parallel", "arbitrary"),
        name="ssd_prompt",
    )(xbc, dt, zg, conv_w, conv_b, dtb_pad, alog_pad, dsk_x, norm_g, ex, ext)


def _f32_key(x):
    b = lax.bitcast_convert_type(x, jnp.int32)
    return b ^ ((b >> 31) & 0x7FFFFFFF)


def _key_f32(k):
    return lax.bitcast_convert_type(k ^ ((k >> 31) & 0x7FFFFFFF), F32)


def _fold(x, rows, op):
    return op(x.reshape(x.shape[0] // rows, rows, x.shape[1]), axis=0)


def _chunk_trips(fn, lo, hi, carry, widths):
    assert widths[-1] == 1
    start = lo
    for w in widths:
        trips = (hi - start) // w
        carry = lax.fori_loop(0, trips, lambda t, c, s=start, w=w: fn(s + t * w, c, w), carry)
        start = start + trips * w
    return carry


def _kth_largest(scan, k_row, take_all, mn=None, mx=None):
    def count_ge(x):
        return scan(lambda t: jnp.where(t >= x, 1.0, 0.0), jnp.sum, 0.0)

    if mx is None:
        mx = scan(lambda t: t, jnp.max, -jnp.inf)
        mn = scan(lambda t: jnp.where(t == -jnp.inf, jnp.inf, t), jnp.min, jnp.inf)
    above = mx + jnp.maximum(jnp.abs(mx) * 2.0 ** -20, 1e-30)
    state = (mn, above, mn, jnp.where(take_all, 1, 0).astype(jnp.int32))

    def value_step(_, s):
        lo, hi, thr, done = s
        mid = 0.5 * lo + 0.5 * hi
        inside = jnp.where(mid > lo, jnp.where(mid < hi, 1, 0), 0)
        cnt = count_ge(mid)
        ge = cnt >= k_row
        finished = jnp.where(cnt == k_row, inside, 0) * (1 - done)
        lo = jnp.where(inside == 1, jnp.where(ge, mid, lo), lo)
        hi = jnp.where(inside == 1, jnp.where(ge, hi, mid), hi)
        return lo, hi, jnp.where(finished == 1, mid, thr), done + finished

    def snap_step(s):
        lo, hi, thr, done = s
        cand = scan(lambda t: jnp.where(t < hi, t, -jnp.inf), jnp.max, -jnp.inf)
        ge = count_ge(cand) >= k_row
        finished = jnp.where(ge, 1, 0) * (1 - done)
        return lo, jnp.where(ge, hi, cand), jnp.where(finished == 1, cand, thr), done + finished

    def key_step(s):
        lo, hi, thr, done = s
        klo, khi = _f32_key(lo), _f32_key(hi)
        mid_k = (klo >> 1) + (khi >> 1) + (klo & khi & 1)
        mid = _key_f32(mid_k)
        cnt = count_ge(mid)
        ge = cnt >= k_row
        collapsed = mid_k == klo
        finished = jnp.where(collapsed, 1, jnp.where(cnt == k_row, 1, 0)) * (1 - done)
        thr = jnp.where(finished == 1, jnp.where(collapsed, lo, mid), thr)
        return jnp.where(ge, mid, lo), jnp.where(ge, hi, mid), thr, done + finished

    state = lax.fori_loop(0, VALUE_STEPS, value_step, state)

    def refine(s):
        it = s[4]
        st = snap_step(s[:4])
        st = lax.cond(it >= SNAP_ONLY_TRIPS, key_step, lambda x: x, st)
        return (*st, it + 1)

    return lax.while_loop(lambda s: jnp.logical_and(jnp.min(s[3]) == 0, s[4] < SNAP_ONLY_TRIPS + 40), refine,
                          (*state, jnp.int32(0)))[2]


def _dsa_prompt_kernel(topk, qi_ref, wi_ref, ki_ref, q_ref, k_ref, v_ref, tz_ref, o_ref,
                       sc_scr, s_scr, acc_scr):
    qb = pl.program_id(1)
    tq = q_ref.shape[2]
    ch = ATT_CHUNK
    n_ch = qb + 1

    def chunk_at(j):
        return pl.ds(pl.multiple_of(j * ch, ch), ch)

    qi_all = jnp.concatenate([qi_ref[0, h * IDX_DIM:(h + 1) * IDX_DIM, :] for h in range(IDX_HEADS)], axis=1)
    w_rows = [wi_ref[0, h:h + 1, :] for h in range(IDX_HEADS)]

    def score_chunk(j, diagonal):
        d = jnp.dot(ki_ref[0, chunk_at(j), :], qi_all, preferred_element_type=F32)
        acc = w_rows[0] * jnp.maximum(d[:, 0:tq], 0.0)
        for h in range(1, IDX_HEADS):
            acc = acc + w_rows[h] * jnp.maximum(d[:, h * tq:(h + 1) * tq], 0.0)
        lo_src = acc
        if diagonal:
            causal = (lax.broadcasted_iota(jnp.int32, (ch, tq), 0) <= lax.broadcasted_iota(jnp.int32, (ch, tq), 1))
            acc, lo_src = jnp.where(causal, acc, -jnp.inf), jnp.where(causal, acc, jnp.inf)
        sc_scr[chunk_at(j), :] = acc
        return _fold(acc, SCAN_FOLD, jnp.max), _fold(lo_src, SCAN_FOLD, jnp.min)

    def score_chunks(jj, c, width):
        hi_acc, lo_acc = c
        for j in [jj * width + i for i in range(width)]:
            hi_j, lo_j = score_chunk(j, False)
            hi_acc, lo_acc = jnp.maximum(hi_acc, hi_j), jnp.minimum(lo_acc, lo_j)
        return hi_acc, lo_acc

    ext = (jnp.full((SCAN_FOLD, tq), -jnp.inf, F32), jnp.full((SCAN_FOLD, tq), jnp.inf, F32))
    ext = lax.fori_loop(0, qb // 2, functools.partial(score_chunks, width=2), ext)
    ext = lax.fori_loop(qb // 2 * 2, qb, functools.partial(score_chunks, width=1), ext)
    hi_d, lo_d = score_chunk(qb, True)
    score_max = jnp.max(jnp.maximum(ext[0], hi_d), axis=0, keepdims=True)
    score_min = jnp.min(jnp.minimum(ext[1], lo_d), axis=0, keepdims=True)

    def chunks_reduce(fn, op, init):
        def body(j, acc):
            rows = pl.ds(pl.multiple_of(j * (2 * ch), 2 * ch), 2 * ch)
            return op(jnp.stack([acc, _fold(fn(sc_scr[rows, :]), SCAN_FOLD, op)]), axis=0)
        return op(lax.fori_loop(0, (n_ch + 1) // 2, body, jnp.full((SCAN_FOLD, tq), init, F32)), axis=0, keepdims=True)

    @pl.when(n_ch % 2 == 1)
    def _():
        sc_scr[chunk_at(n_ch), :] = jnp.full((ch, tq), -jnp.inf, F32)

    def count(cmp, thr):
        return chunks_reduce(lambda t: jnp.where(cmp(t, thr), 1.0, 0.0), jnp.sum, 0.0)

    count_ge = functools.partial(count, lambda t, x: t >= x)

    n_valid = qb * tq + lax.broadcasted_iota(jnp.int32, (1, tq), 1) + 1
    k_row = jnp.minimum(n_valid, topk).astype(F32)
    thr = _kth_largest(chunks_reduce, k_row, n_valid <= topk, score_min, score_max)

    surplus = jnp.max(count_ge(thr) - k_row)

    @pl.when(surplus <= 0.0)
    def _():
        def select_chunk(j, c):
            sc_scr[chunk_at(j), :] = jnp.where(sc_scr[chunk_at(j), :] >= thr, 0.0, NEG_BIG)
            return c
        lax.fori_loop(0, n_ch, select_chunk, 0)

    @pl.when(surplus > 0.0)
    def _():
        need = k_row - count(lambda t, x: t > x, thr)
        lower = jnp.where(lax.broadcasted_iota(jnp.int32, (ch, ch), 1) <= lax.broadcasted_iota(jnp.int32, (ch, ch), 0),
                          1.0, 0.0).astype(BF16)
        ones_r = jnp.ones((SUBLANES, ch), BF16)

        def select_chunk(j, run):
            t = sc_scr[chunk_at(j), :]
            eq = t == thr
            eqb = jnp.where(eq, 1.0, 0.0).astype(BF16)
            rank = jnp.dot(lower, eqb, preferred_element_type=F32) + run
            tie_bias = jnp.where(rank <= need, 0.0, NEG_BIG)
            sc_scr[chunk_at(j), :] = jnp.where(t > thr, 0.0, jnp.where(eq, tie_bias, NEG_BIG))
            return run + jnp.dot(ones_r, eqb, preferred_element_type=F32)[0:1, :]

        lax.fori_loop(0, n_ch, select_chunk, jnp.zeros((1, tq), F32))

    scale = HEAD_DIM ** -0.5 * LOG2_E
    cols = KV_REP * tq
    n_far = jnp.maximum(qb - 1, 0)
    for gp in range(N_KV_HEADS // 2):
        groups = (2 * gp, 2 * gp + 1)
        q_cols = [jnp.concatenate([q_ref[0, (g * KV_REP + r) * HEAD_DIM:(g * KV_REP + r + 1) * HEAD_DIM, :]
                                   for r in range(KV_REP)], axis=1) for g in groups]

        def logits_chunks(j0, mx, width, near):
            mx = list(mx)
            for j in [j0 + i for i in range(width)]:
                selb = sc_scr[chunk_at(j), :]
                for gi, g in enumerate(groups):
                    s = jnp.dot(k_ref[0, chunk_at(j), g * HEAD_DIM:(g + 1) * HEAD_DIM], q_cols[gi],
                                preferred_element_type=F32) * scale
                    if near:
                        bias = jnp.concatenate([tz_ref[qb - j, g * KV_REP + r] + selb for r in range(KV_REP)], axis=1)
                    else:
                        bias = jnp.concatenate([selb] * KV_REP, axis=1)
                    s = s + bias
                    s_scr[gi, chunk_at(j), :] = s
                    mx[gi] = jnp.maximum(mx[gi], _fold(s, ATT_FOLD, jnp.max))
            return tuple(mx)

        mx = (jnp.full((ATT_FOLD, cols), NEG_BIG, F32),) * 2
        mx = _chunk_trips(functools.partial(logits_chunks, near=False), 0, n_far, mx, TRIP_WIDTHS)
        mx = _chunk_trips(functools.partial(logits_chunks, near=True), n_far, n_ch, mx, (1,))
        m_row = [jnp.max(m, axis=0, keepdims=True) for m in mx]
        acc_scr[...] = jnp.zeros_like(acc_scr)

        def weights_chunks(j0, l8, width):
            l8 = list(l8)
            for j in [j0 + i for i in range(width)]:
                for gi, g in enumerate(groups):
                    p = jnp.exp2(s_scr[gi, chunk_at(j), :] - m_row[gi])
                    l8[gi] = l8[gi] + _fold(p, ATT_FOLD, jnp.sum)
                    acc_scr[gi] += jnp.dot(v_ref[0, g * HEAD_DIM:(g + 1) * HEAD_DIM, chunk_at(j)], p.astype(BF16),
                                           preferred_element_type=F32)
            return tuple(l8)

        l8 = (jnp.zeros((ATT_FOLD, cols), F32),) * 2
        l8 = _chunk_trips(weights_chunks, 0, n_ch, l8, TRIP_WIDTHS)
        for gi, g in enumerate(groups):
            out = acc_scr[gi] * (1.0 / jnp.sum(l8[gi], axis=0, keepdims=True))
            for r in range(KV_REP):
                h = g * KV_REP + r
                o_ref[0, :, h * HEAD_DIM:(h + 1) * HEAD_DIM] = out[:, r * tq:(r + 1) * tq].T.astype(o_ref.dtype)


def _dsa_prompt(qi_t, wi_t, ki_b, q_t, k_b, v_t, tz, topk):
    b_, s_, _ = k_b.shape
    tq = min(Q_BLOCK, s_)
    assert tq == ATT_CHUNK and s_ % (2 * tq) == 0
    once = pl.Buffered(1)
    q_tile = lambda rows: pl.BlockSpec((1, rows, tq), lambda b, i: (b, 0, i))
    return pl.pallas_call(
        functools.partial(_dsa_prompt_kernel, topk),
        grid=(b_, s_ // tq),
        in_specs=[
            q_tile(IDX_HEADS * IDX_DIM),
            q_tile(IDX_HEADS),
            pl.BlockSpec((1, s_, IDX_DIM), lambda b, i: (b, 0, 0), pipeline_mode=once),
            q_tile(D_ATT),
            pl.BlockSpec((1, s_, D_KV), lambda b, i: (b, 0, 0), pipeline_mode=once),
            pl.BlockSpec((1, D_KV, s_), lambda b, i: (b, 0, 0), pipeline_mode=once),
            _resident(tz.shape),
        ],
        out_specs=pl.BlockSpec((1, tq, D_ATT), lambda b, i: (b, i, 0)),
        out_shape=jax.ShapeDtypeStruct((b_, s_, D_ATT), BF16),
        scratch_shapes=[
            pltpu.VMEM((s_, tq), F32),
            pltpu.VMEM((2, s_, KV_REP * tq), F32),
            pltpu.VMEM((2, HEAD_DIM, KV_REP * tq), F32),
        ],
        compiler_params=_cparams("parallel", "arbitrary"),
        name="dsa_prompt",
    )(qi_t, wi_t, ki_b, q_t, k_b, v_t, tz)


def _mix_and_norm(att_ref, ssd_ref, x_ref, g1_ref, sc2_ref, sh2_ref, n2_ref, wo_ref):
    o = (jnp.dot(att_ref[0].astype(BF16), wo_ref[:D_ATT, :], preferred_element_type=F32)
         + jnp.dot(ssd_ref[0].astype(BF16), wo_ref[D_ATT:, :], preferred_element_type=F32))
    x1 = x_ref[0] + g1_ref[0] * o
    h2 = ((_rms(x1) * n2_ref[...]) * (1.0 + sc2_ref[0]) + sh2_ref[0]).astype(BF16)
    return x1, h2


def _ffn_prompt_kernel(att_ref, ssd_ref, x_ref, g1_ref, sc2_ref, sh2_ref, g2_ref, n2_ref, wo_ref, wu_ref,
                       cw_ref, cb_ref, wd_ref, y_o, tail_o, a_scr, ug_scr, uv_scr, carry_scr):
    i = pl.program_id(1)
    tm = x_ref.shape[1]
    halo = SUBLANES

    @pl.when(i == 0)
    def _():
        carry_scr[...] = jnp.zeros_like(carry_scr)

    x1, h2 = _mix_and_norm(att_ref, ssd_ref, x_ref, g1_ref, sc2_ref, sh2_ref, n2_ref, wo_ref)

    def conv_cols(scr, c0):
        u = jnp.dot(h2, wu_ref[:, c0:c0 + FF_CHUNK], preferred_element_type=F32)
        scr[0:halo, :] = carry_scr[:, c0:c0 + FF_CHUNK]
        scr[halo:halo + tm, :] = u
        carry_scr[:, c0:c0 + FF_CHUNK] = u[tm - halo:, :]
        ue = scr[...]
        taps = cw_ref[0:1, c0:c0 + FF_CHUNK] * ue
        for j in range(1, FFN_CONV):
            taps = cw_ref[j:j + 1, c0:c0 + FF_CHUNK] * ue + pltpu.roll(taps, 1, axis=0)
        return cb_ref[:, c0:c0 + FF_CHUNK] + taps[halo:, :]

    for jc in range(D_FF // FF_CHUNK):
        gate = conv_cols(ug_scr.at[jc % 2], jc * FF_CHUNK)
        val = conv_cols(uv_scr.at[jc % 2], D_FF + jc * FF_CHUNK)
        a_scr[:, jc * FF_CHUNK:(jc + 1) * FF_CHUNK] = (_silu(gate) * val).astype(BF16)

    y_o[0] = x1 + g2_ref[0] * jnp.dot(a_scr[...], wd_ref[...], preferred_element_type=F32)
    tail_o[0] = carry_scr[...]


def _ffn_prompt(att, ssd, x, g1, sc2, sh2, g2, norm2_g, wo_b, wu_b, cw, cb, wd_b):
    b_, s_, _ = x.shape
    tm = min(ROW_TILE, s_)
    nt = s_ // tm
    row = lambda w: pl.BlockSpec((1, tm, w), lambda b, i: (b, i, 0))
    mod = pl.BlockSpec((1, 1, D_MODEL), lambda b, i: (b, 0, 0))
    return pl.pallas_call(
        _ffn_prompt_kernel,
        grid=(b_, nt),
        in_specs=[row(D_ATT), row(D_SSM), row(D_MODEL), mod, mod, mod, mod, _resident((1, D_MODEL)),
                  _resident((D_MIX, D_MODEL)), _resident((D_MODEL, 2 * D_FF)), _resident((FFN_CONV, 2 * D_FF)),
                  _resident((1, 2 * D_FF)), _resident((D_FF, D_MODEL))],
        out_specs=[row(D_MODEL), pl.BlockSpec((1, SUBLANES, 2 * D_FF), lambda b, i: (b, 0, 0))],
        out_shape=[jax.ShapeDtypeStruct((b_, s_, D_MODEL), F32),
                   jax.ShapeDtypeStruct((b_, SUBLANES, 2 * D_FF), F32)],
        scratch_shapes=[pltpu.VMEM((tm, D_FF), BF16), pltpu.VMEM((2, tm + SUBLANES, FF_CHUNK), F32),
                        pltpu.VMEM((2, tm + SUBLANES, FF_CHUNK), F32), pltpu.VMEM((SUBLANES, 2 * D_FF), F32)],
        compiler_params=_cparams("parallel", "arbitrary"),
        name="ffn_prompt",
    )(att, ssd, x, g1, sc2, sh2, g2, norm2_g.reshape(1, D_MODEL), wo_b, wu_b, cw, cb, wd_b)


def _ffn_sample_kernel(att_ref, ssd_ref, x_ref, g1_ref, sc2_ref, sh2_ref, g2_ref, n2_ref, wo_ref, wu_ref,
                       cw_ref, cb_ref, wd_ref, h0_ref, h1_ref, y_o, u_o, acc_scr):
    x1, h2 = _mix_and_norm(att_ref, ssd_ref, x_ref, g1_ref, sc2_ref, sh2_ref, n2_ref, wo_ref)
    acc_scr[...] = jnp.zeros_like(acc_scr)

    def conv_cols(c0):
        u = jnp.dot(h2, wu_ref[:, c0:c0 + FF_CHUNK], preferred_element_type=F32)
        u_o[:, c0:c0 + FF_CHUNK] = u
        sl = slice(c0, c0 + FF_CHUNK)
        return (cb_ref[:, sl] + cw_ref[0:1, sl] * h0_ref[:, sl] + cw_ref[1:2, sl] * h1_ref[:, sl]
                + cw_ref[2:3, sl] * u)

    for jc in range(D_FF // FF_CHUNK):
        gate = conv_cols(jc * FF_CHUNK)
        val = conv_cols(D_FF + jc * FF_CHUNK)
        a = (_silu(gate) * val).astype(BF16)
        acc_scr[...] += jnp.dot(a, wd_ref[jc * FF_CHUNK:(jc + 1) * FF_CHUNK, :], preferred_element_type=F32)

    y_o[0] = x1 + g2_ref[0] * acc_scr[...]


def _ffn_sample(att, ssd, x, g1, sc2, sh2, g2, norm2_g, wo_b, wu_b, cw, cb, wd_b, hist0, hist1):
    n = x.shape[1]
    full = lambda *shape: pl.BlockSpec(shape, lambda: (0,) * len(shape))
    r3 = lambda w: full(1, n, w)
    return pl.pallas_call(
        _ffn_sample_kernel,
        in_specs=[r3(D_ATT), r3(D_SSM), r3(D_MODEL), r3(D_MODEL), r3(D_MODEL), r3(D_MODEL), r3(D_MODEL),
                  full(1, D_MODEL), full(D_MIX, D_MODEL), full(D_MODEL, 2 * D_FF), full(FFN_CONV, 2 * D_FF),
                  full(1, 2 * D_FF), full(D_FF, D_MODEL), full(n, 2 * D_FF), full(n, 2 * D_FF)],
        out_specs=[r3(D_MODEL), full(n, 2 * D_FF)],
        out_shape=[jax.ShapeDtypeStruct((1, n, D_MODEL), F32), jax.ShapeDtypeStruct((n, 2 * D_FF), F32)],
        scratch_shapes=[pltpu.VMEM((n, D_MODEL), F32)],
        compiler_params=pltpu.CompilerParams(vmem_limit_bytes=VMEM_LIMIT),
        name="ffn_sample",
    )(att, ssd, x, g1, sc2, sh2, g2, norm2_g.reshape(1, D_MODEL), wo_b, wu_b, cw, cb, wd_b, hist0, hist1)


def _ssd_sample_kernel(xbc_ref, h0_ref, h1_ref, h2_ref, dt_ref, zg_ref, cw_ref, cb_ref, dtb_ref, alog_ref,
                       dsk_ref, ng_ref, ex_ref, st_ref, y_o, st_o):
    nb = xbc_ref.shape[0]
    conv = (cb_ref[...] + cw_ref[0:1, :] * h0_ref[...] + cw_ref[1:2, :] * h1_ref[...]
            + cw_ref[2:3, :] * h2_ref[...] + cw_ref[3:4, :] * xbc_ref[...])
    xc = _silu(conv)
    xs = xc[:, :D_SSM]
    bm = xc[:, D_SSM:D_SSM + SSM_GROUPS * SSM_STATE]
    cm = xc[:, D_SSM + SSM_GROUPS * SSM_STATE:]
    dt = _softplus(dt_ref[...] + dtb_ref[...])
    dec = jnp.exp(dt * (-jnp.exp(alog_ref[...])))
    ex = ex_ref[...]
    xdt = xs * _dot_sel_rhs(dt, ex)
    dec_x = _dot_sel_rhs(dec, ex)
    gw = SSM_HEADS // SSM_GROUPS * SSM_HEAD_DIM
    ones_n = jnp.ones((SUBLANES, SSM_STATE), BF16)
    row_id = lax.broadcasted_iota(jnp.int32, (SUBLANES, 1), 0)

    def stack(rows):
        out = jnp.zeros((SUBLANES, rows[0].shape[1]), F32)
        for r, v in enumerate(rows):
            out = jnp.where(row_id == r, v.astype(F32), out)
        return out.astype(BF16)

    ys = []
    for i in range(nb):
        yrow = []
        for g in range(SSM_GROUPS):
            hs = st_ref[i, g * gw:(g + 1) * gw, :]
            d0, d1, d2 = _split3(dec_x[i:i + 1, g * gw:(g + 1) * gw])
            x0, x1, _ = _split3(xdt[i:i + 1, g * gw:(g + 1) * gw])
            b0, b1, _ = _split3(bm[i:i + 1, g * SSM_STATE:(g + 1) * SSM_STATE])
            c0, c1, _ = _split3(cm[i:i + 1, g * SSM_STATE:(g + 1) * SSM_STATE])
            dec_b = _dot_tn(stack([d0, d1, d2]), ones_n)
            upd = _dot_tn(stack([x0, x0, x1, x1]), stack([b0, b1, b0, b1]))
            hn = hs * dec_b + upd
            st_o[i, g * gw:(g + 1) * gw, :] = hn
            h0, h1, _ = _split3(hn)
            c_rows = stack([c0, c1])
            yg = _dot_nt(c_rows, h0) + _dot_nt(c_rows, h1)
            yrow.append(yg[0:1, :] + yg[1:2, :])
        ys.append(jnp.concatenate(yrow, axis=-1))
    y = jnp.concatenate(ys, axis=0) + dsk_ref[...] * xs
    y = y * _silu(zg_ref[...])
    y_o[...] = _rms(y) * ng_ref[...]


def _ssd_sample(xbc, hist, dt, zg, conv_w, conv_b, dtb_pad, alog_pad, dsk_x, norm_g, ex, state):
    n = xbc.shape[0]
    nb = SUBLANES
    rows = lambda w: pl.BlockSpec((nb, w), lambda i: (i, 0))
    st = pl.BlockSpec((nb, D_SSM, SSM_STATE), lambda i: (i, 0, 0))
    return pl.pallas_call(
        _ssd_sample_kernel,
        grid=(n // nb,),
        in_specs=[rows(CONV_DIM), rows(CONV_DIM), rows(CONV_DIM), rows(CONV_DIM), rows(LANES), rows(D_SSM),
                  _resident((SSM_CONV, CONV_DIM)), _resident((1, CONV_DIM)), _resident((1, LANES)),
                  _resident((1, LANES)), _resident((1, D_SSM)), _resident((1, D_SSM)), _resident((LANES, D_SSM)), st],
        out_specs=[rows(D_SSM), st],
        out_shape=[jax.ShapeDtypeStruct((n, D_SSM), F32), jax.ShapeDtypeStruct((n, D_SSM, SSM_STATE), F32)],
        compiler_params=_cparams("parallel"),
        name="ssd_sample",
    )(xbc, hist[0], hist[1], hist[2], dt, zg, conv_w, conv_b, dtb_pad, alog_pad, dsk_x, norm_g, ex, state)


def _idx_sample_kernel(pt_ref, qi_ref, wi_ref, kin_ref, pool_ref, o_ref, kbuf, sem):
    b = pl.program_id(0)
    nb = pl.num_programs(0)
    n_pages = kbuf.shape[1]
    page = kbuf.shape[3]

    def page_copy(seq, slot, j):
        return pltpu.make_async_copy(pool_ref.at[pt_ref[seq, j]], kbuf.at[slot, j], sem.at[slot])

    def fetch(seq, slot):
        def body(j, c):
            page_copy(seq, slot, j).start()
            return c
        lax.fori_loop(0, n_pages, body, 0, unroll=True)

    slots = kbuf.shape[0]
    depth = slots - 1

    @pl.when(b == 0)
    def _():
        for s in range(depth):
            @pl.when(s < nb)
            def _():
                fetch(s, s)

    @pl.when(b + depth < nb)
    def _():
        fetch(b + depth, (b + depth) % slots)

    slot = b % slots

    def wait_body(j, c):
        page_copy(b, slot, j).wait()
        return c
    lax.fori_loop(0, n_pages, wait_body, 0, unroll=True)

    qi = qi_ref[0]
    w = wi_ref[0]

    span = PAGES_PER_DOT * page

    def page_scores(j, c):
        keys = jnp.concatenate([kbuf[slot, PAGES_PER_DOT * j + i] for i in range(PAGES_PER_DOT)], axis=1)
        d = jnp.dot(qi, keys.astype(BF16), preferred_element_type=F32)
        o_ref[0, :, pl.ds(pl.multiple_of(j * span, span), span)] = jnp.sum(w * jnp.maximum(d, 0.0), axis=0, keepdims=True)
        return c

    lax.fori_loop(0, n_pages // PAGES_PER_DOT, page_scores, 0, unroll=True)
    knew = kin_ref[0].astype(BF16).astype(F32)
    dn = jnp.sum(qi.astype(F32) * knew, axis=-1, keepdims=True)
    s_new = jnp.sum(w * jnp.maximum(dn, 0.0), axis=0, keepdims=True)
    lane = lax.broadcasted_iota(jnp.int32, (1, SCORE_PAD), 1)
    o_ref[0, :, n_pages * page:] = jnp.where(lane == 0, s_new, -jnp.inf)


def _idx_sample(page_table, qi3, wi3, ki_new3, pool_t):
    n, n_pages = page_table.shape
    page = pool_t.shape[2]
    assert n_pages % (8 * PAGES_PER_DOT) == 0
    past = n_pages * page
    grid_spec = pltpu.PrefetchScalarGridSpec(
        num_scalar_prefetch=1,
        grid=(n,),
        in_specs=[
            pl.BlockSpec((1, IDX_HEADS, IDX_DIM), lambda b, pt: (b, 0, 0)),
            pl.BlockSpec((1, IDX_HEADS, 1), lambda b, pt: (b, 0, 0)),
            pl.BlockSpec((1, 1, IDX_DIM), lambda b, pt: (b, 0, 0)),
            pl.BlockSpec(memory_space=pl.ANY),
        ],
        out_specs=pl.BlockSpec((1, 1, past + SCORE_PAD), lambda b, pt: (b, 0, 0)),
        scratch_shapes=[pltpu.VMEM((IDX_SLOTS, n_pages, IDX_DIM, page), F32), pltpu.SemaphoreType.DMA((IDX_SLOTS,))],
    )
    return pl.pallas_call(
        _idx_sample_kernel,
        grid_spec=grid_spec,
        out_shape=jax.ShapeDtypeStruct((n, 1, past + SCORE_PAD), F32),
        compiler_params=_cparams("arbitrary"),
        name="idx_sample",
    )(page_table, qi3, wi3, ki_new3, pool_t)


def _att_sample_kernel(q_ref, k_ref, v_ref, knew_ref, vnew_ref, bias_ref, o_ref):
    nb = q_ref.shape[0]
    rows = k_ref.shape[1]
    scale = HEAD_DIM ** -0.5
    head_group = lax.broadcasted_iota(jnp.int32, (N_HEADS, 1), 0) // KV_REP

    def per_head(new):
        out = jnp.zeros((N_HEADS, HEAD_DIM), F32)
        for g in range(N_KV_HEADS):
            out = jnp.where(head_group == g, new[g:g + 1, :], out)
        return out.astype(BF16).astype(F32)

    for i in range(nb):
        qb = q_ref[i].astype(BF16)
        s = _dot_nt(qb, k_ref[i].astype(BF16)) * scale + bias_ref[i, :, :rows]
        s_new = (jnp.sum(qb.astype(F32) * per_head(knew_ref[i]), axis=-1, keepdims=True) * scale
                 + bias_ref[i, :, rows:rows + 1])
        m = jnp.maximum(jnp.max(s, axis=-1, keepdims=True), s_new)
        p, p_new = jnp.exp(s - m), jnp.exp(s_new - m)
        inv = 1.0 / (jnp.sum(p, axis=-1, keepdims=True) + p_new)
        o_ref[i] = (jnp.dot((p * inv).astype(BF16), v_ref[i].astype(BF16), preferred_element_type=F32)
                    + (p_new * inv).astype(BF16).astype(F32) * per_head(vnew_ref[i]))


def _att_sample(q3, k_rows, v_rows, k_new, v_new, bias):
    n, rows = k_rows.shape[:2]
    nb = SUBLANES
    gathered = pl.BlockSpec((nb, rows, HEAD_DIM), lambda i: (i, 0, 0))
    new3 = pl.BlockSpec((nb, N_KV_HEADS, HEAD_DIM), lambda i: (i, 0, 0))
    return pl.pallas_call(
        _att_sample_kernel,
        grid=(n // nb,),
        in_specs=[
            pl.BlockSpec((nb, N_HEADS, HEAD_DIM), lambda i: (i, 0, 0)),
            gathered, gathered, new3, new3,
            pl.BlockSpec((nb, N_HEADS, rows + LANES), lambda i: (i, 0, 0)),
        ],
        out_specs=pl.BlockSpec((nb, N_HEADS, HEAD_DIM), lambda i: (i, 0, 0)),
        out_shape=jax.ShapeDtypeStruct((n, N_HEADS, HEAD_DIM), F32),
        compiler_params=_cparams("parallel"),
        name="att_sample",
    )(q3, k_rows, v_rows, k_new, v_new, bias)


def _sel_sample_kernel(k_keep, n_valid, sc_ref, o_ref):
    rows, n = sc_ref.shape
    ch = ATT_CHUNK
    span = 2 * ch

    def scan(fn, op, init):
        def body(j, acc):
            r = pl.ds(pl.multiple_of(j * span, span), span)
            return op(jnp.stack([acc, _fold(fn(sc_ref[r, :]), SCAN_FOLD, op)]), axis=0)
        return op(lax.fori_loop(0, rows // span, body, jnp.full((SCAN_FOLD, n), init, F32)), axis=0, keepdims=True)

    k_row = jnp.full((1, n), float(k_keep), F32)
    thr = _kth_largest(scan, k_row, jnp.full((1, n), n_valid <= k_keep))
    need = k_row - scan(lambda t: jnp.where(t > thr, 1.0, 0.0), jnp.sum, 0.0)
    lower = jnp.where(lax.broadcasted_iota(jnp.int32, (ch, ch), 1) <= lax.broadcasted_iota(jnp.int32, (ch, ch), 0),
                      1.0, 0.0).astype(BF16)
    ones_r = jnp.ones((SUBLANES, ch), BF16)
    key_local = lax.broadcasted_iota(jnp.int32, (ch, n), 0).astype(F32)
    slot_iota = lax.broadcasted_iota(jnp.int32, (k_keep, n), 0).astype(F32)
    none = float(rows)

    def chunk_body(c, carry):
        run, slot, out = carry
        t = sc_ref[pl.ds(pl.multiple_of(c * ch, ch), ch), :]
        eq = t == thr
        eqb = jnp.where(eq, 1.0, 0.0).astype(BF16)
        rank = jnp.dot(lower, eqb, preferred_element_type=F32) + run
        cand = jnp.where(t > thr, key_local, jnp.where(eq, jnp.where(rank <= need, key_local, none), none))
        left = _fold(jnp.where(cand < none, 1.0, 0.0), SUBLANES, jnp.sum).sum(axis=0, keepdims=True)
        base = jnp.asarray(c * ch, F32)

        def extract(_, s):
            cand, out, slot = s
            cur = jnp.min(cand, axis=0, keepdims=True)
            has = cur < none
            out = jnp.where(slot_iota == slot, jnp.where(has, cur + base, out), out)
            return jnp.where(cand == cur, none, cand), out, slot + jnp.where(has, 1.0, 0.0)

        _, out, slot = lax.fori_loop(0, jnp.max(left).astype(jnp.int32), extract, (cand, out, slot))
        return run + jnp.dot(ones_r, eqb, preferred_element_type=F32)[0:1, :], slot, out

    zeros_row = jnp.zeros((1, n), F32)
    _, _, out = lax.fori_loop(0, rows // ch, chunk_body, (zeros_row, zeros_row, jnp.zeros((k_keep, n), F32)))
    o_ref[...] = out.astype(jnp.int32)


def _sel_sample(scores_t, k_keep, n_valid):
    rows, n = scores_t.shape
    assert rows % (2 * ATT_CHUNK) == 0
    return pl.pallas_call(
        functools.partial(_sel_sample_kernel, k_keep, n_valid),
        in_specs=[pl.BlockSpec((rows, n), lambda: (0, 0))],
        out_specs=pl.BlockSpec((k_keep, n), lambda: (0, 0)),
        out_shape=jax.ShapeDtypeStruct((k_keep, n), jnp.int32),
        compiler_params=pltpu.CompilerParams(vmem_limit_bytes=VMEM_LIMIT),
        name="sel_sample",
    )(scores_t)


def _pack_w_in(w_in):
    offs = np.cumsum((0,) + IN_SPLITS)
    parts = [w_in[:, offs[i]:offs[i + 1]] for i in range(len(IN_SPLITS))]
    pad = lambda a: jnp.pad(a, ((0, 0), (0, LANES - a.shape[1])))
    q, k, v, qi, ki, wi, zg, xbc, dt = parts
    return jnp.concatenate([q, k, v, qi, pad(ki), pad(wi), zg, xbc, pad(dt)], axis=1).astype(BF16)


def _t5_bucket(dist):
    dist = jnp.maximum(dist, 0)
    max_exact = N_BUCKETS // 2
    d = jnp.maximum(dist, 1).astype(F32)
    large = max_exact + (jnp.log(d / max_exact) / math.log(MAX_DISTANCE / max_exact) * (N_BUCKETS - max_exact)).astype(jnp.int32)
    large = jnp.minimum(large, N_BUCKETS - 1)
    return jnp.where(dist < max_exact, dist, large)


def _rel_bias_tables(rel_bias, tq):
    assert ATT_CHUNK == tq and tq >= MAX_DISTANCE
    n = 2 * tq
    idx = jnp.arange(n, dtype=jnp.int32)
    query_minus_key = jnp.where(idx < tq, idx, idx - n)
    far = rel_bias[_t5_bucket(jnp.int32(MAX_DISTANCE))]
    tabs = []
    for d0 in (0, tq):
        f = (rel_bias[_t5_bucket(d0 + query_minus_key)] - far[None, :]) * LOG2_E
        flat = jnp.tile(f.T, (1, ATT_CHUNK))
        tabs.append(flat[:, :ATT_CHUNK * (n - 1)].reshape(-1, ATT_CHUNK, n - 1)[:, :, :tq])
    return jnp.stack(tabs)


def _head_expand():
    h = jnp.arange(LANES, dtype=jnp.int32)[:, None]
    c = jnp.arange(D_SSM, dtype=jnp.int32)[None, :]
    ex = (c // SSM_HEAD_DIM == h).astype(BF16)
    return ex, ex.T


def _pad_lanes(v):
    return jnp.pad(v.astype(F32), (0, LANES - v.shape[0])).reshape(1, LANES)


def kernel(x_prompt, x_sample, cache_k, cache_v, cache_kidx, state_ssm, state_conv_ssd, state_conv_ffn, page_table,
           c_prompt, c_sample, rel_bias, norm1_g, w_ada, b_ada, w_in, q_norm_g, k_norm_g, conv_ssd_w, conv_ssd_b,
           dt_bias, a_log, d_skip, ssd_norm_g, w_out, norm2_g, w_up, conv_ffn_w, conv_ffn_b, w_down):
    depth = w_in.shape[0]
    assert depth == 1
    bp, s_, _ = x_prompt.shape
    ns = x_sample.shape[0]
    assert x_sample.shape[1] == 1
    lyr = 0
    n_pages = page_table.shape[1]
    page = cache_k.shape[2]
    past = n_pages * page
    topk_p = min(TOPK_MAX, s_ // TOPK_DIV)
    topk_s = min(TOPK_MAX, (past + 1) // TOPK_DIV)

    w_pack = _pack_w_in(w_in[lyr])
    wo_b = w_out[lyr].astype(BF16)
    wu_b = w_up[lyr].astype(BF16)
    wd_b = w_down[lyr].astype(BF16)
    ex, ext = _head_expand()
    dtb_pad = _pad_lanes(dt_bias[lyr])
    alog_pad = _pad_lanes(a_log[lyr])
    dsk_x = jnp.repeat(d_skip[lyr].astype(F32), SSM_HEAD_DIM).reshape(1, D_SSM)
    ssd_g = ssd_norm_g[lyr].reshape(1, D_SSM)
    cw_ssd, cb_ssd = conv_ssd_w[lyr], conv_ssd_b[lyr].reshape(1, CONV_DIM)
    cw_ffn, cb_ffn = conv_ffn_w[lyr], conv_ffn_b[lyr].reshape(1, 2 * D_FF)
    tz = _rel_bias_tables(rel_bias.astype(F32), min(Q_BLOCK, s_))

    n_c = bp + ns
    c_all = jnp.pad(jnp.concatenate([c_prompt, c_sample], axis=0), ((0, (-n_c) % SUBLANES), (0, 0)))
    mod = _ada(c_all, w_ada[lyr], b_ada[lyr])
    mods_p = [mod[:bp, i * D_MODEL:(i + 1) * D_MODEL].reshape(bp, 1, D_MODEL) for i in range(6)]
    mods_s = [mod[bp:bp + ns, i * D_MODEL:(i + 1) * D_MODEL].reshape(1, ns, D_MODEL) for i in range(6)]

    sh1, sc1, g1, sh2, sc2, g2 = mods_p
    (q_t, k_f, v_f, k_b, v_t, qi_t, ki_f, ki_b, wi_t, zg, xbc, dt) = _inproj(
        x_prompt, norm1_g[lyr], sc1, sh1, w_pack, q_norm_g[lyr], k_norm_g[lyr], rows_per_mod=1, queries_on_lanes=True)
    att_p = _dsa_prompt(qi_t, wi_t, ki_b, q_t, k_b, v_t, tz, topk_p)
    ssd_p, ssm_p = _ssd_prompt(xbc, dt, zg, cw_ssd, cb_ssd, dtb_pad, alog_pad, dsk_x, ssd_g, ex, ext)
    y_p, tail_p = _ffn_prompt(att_p, ssd_p, x_prompt, g1, sc2, sh2, g2, norm2_g[lyr], wo_b, wu_b, cw_ffn, cb_ffn, wd_b)

    k_p = k_f.reshape(1, bp, s_, N_KV_HEADS, HEAD_DIM)
    v_p = v_f.reshape(1, bp, s_, N_KV_HEADS, HEAD_DIM)
    kidx_p = ki_f.reshape(1, bp, s_, IDX_DIM)
    ssm_p = ssm_p.reshape(1, bp, SSM_HEADS, SSM_HEAD_DIM, SSM_STATE)
    cssd_p = xbc[:, s_ - (SSM_CONV - 1):, :].reshape(1, bp, SSM_CONV - 1, CONV_DIM)
    cffn_p = tail_p[:, SUBLANES - (FFN_CONV - 1):, :].reshape(1, bp, FFN_CONV - 1, 2 * D_FF)

    sh1, sc1, g1, sh2, sc2, g2 = mods_s
    xs3 = x_sample.reshape(1, ns, D_MODEL)
    (q_b, k_f, v_f, _, _, qi_b, ki_f, _, wi_f, zg, xbc, dt) = _inproj(
        xs3, norm1_g[lyr], sc1, sh1, w_pack, q_norm_g[lyr], k_norm_g[lyr], rows_per_mod=ns, queries_on_lanes=False)
    k_new, v_new, ki_new = k_f[0], v_f[0], ki_f[0]

    scores = _idx_sample(page_table, qi_b[0].reshape(ns, IDX_HEADS, IDX_DIM), wi_f[0].reshape(ns, IDX_HEADS, 1),
                         ki_new.reshape(ns, 1, IDX_DIM), jnp.swapaxes(cache_kidx[lyr], 1, 2))[:, 0, :]
    idx = _sel_sample(scores.T, topk_s, past + 1).T
    n_rows = topk_s * N_KV_HEADS
    is_new = idx >= past
    pidx = jnp.minimum(idx, past - 1)
    page_hit = (pidx // page)[:, :, None] == jnp.arange(n_pages, dtype=jnp.int32)
    phys = jnp.sum(jnp.where(page_hit, page_table[:, None, :], 0), axis=-1)
    off = pidx % page
    bucket_hit = _t5_bucket(past - idx)[:, None, :, None] == jnp.arange(N_BUCKETS, dtype=jnp.int32)
    bias = jnp.sum(jnp.where(bucket_hit, rel_bias.astype(F32).T[None, :, None, :], 0.0), axis=-1)
    own_group = jnp.arange(N_KV_HEADS)[None, :] == (jnp.arange(N_HEADS) // KV_REP)[:, None]
    use_row = jnp.logical_and(own_group[None, :, None, :], jnp.logical_not(is_new)[:, None, :, None])
    bias_rows = jnp.where(use_row, bias[:, :, :, None], NEG_BIG).reshape(ns, N_HEADS, n_rows)
    bias_new = jnp.where(jnp.any(is_new, axis=1)[:, None], rel_bias[_t5_bucket(jnp.int32(0))].astype(F32)[None, :], NEG_BIG)
    tail = jnp.where(jnp.arange(LANES)[None, None, :] == 0, bias_new[:, :, None], NEG_BIG)
    att_s = _att_sample(q_b[0].astype(F32).reshape(ns, N_HEADS, HEAD_DIM),
                        cache_k[lyr][phys, off].reshape(ns, n_rows, HEAD_DIM),
                        cache_v[lyr][phys, off].reshape(ns, n_rows, HEAD_DIM),
                        k_new.reshape(ns, N_KV_HEADS, HEAD_DIM), v_new.reshape(ns, N_KV_HEADS, HEAD_DIM),
                        jnp.concatenate([bias_rows, tail], axis=-1)).reshape(1, ns, D_ATT)

    hist_ssd = state_conv_ssd[lyr]
    ssd_s, ssm_s = _ssd_sample(xbc[0], [hist_ssd[:, j, :] for j in range(SSM_CONV - 1)], dt[0], zg[0], cw_ssd, cb_ssd,
                               dtb_pad, alog_pad, dsk_x, ssd_g, ex,
                               state_ssm[lyr].reshape(ns, D_SSM, SSM_STATE))
    hist_ffn = state_conv_ffn[lyr]
    y_s, u_s = _ffn_sample(att_s, ssd_s.reshape(1, ns, D_SSM), xs3, g1, sc2, sh2, g2, norm2_g[lyr], wo_b, wu_b,
                           cw_ffn, cb_ffn, wd_b, hist_ffn[:, 0, :], hist_ffn[:, 1, :])

    k_s = k_new.reshape(1, ns, 1, N_KV_HEADS, HEAD_DIM)
    v_s = v_new.reshape(1, ns, 1, N_KV_HEADS, HEAD_DIM)
    kidx_s = ki_new.reshape(1, ns, 1, IDX_DIM)
    ssm_s = ssm_s.reshape(1, ns, SSM_HEADS, SSM_HEAD_DIM, SSM_STATE)
    cssd_s = jnp.concatenate([hist_ssd[:, 1:, :], xbc[0][:, None, :]], axis=1)[None]
    cffn_s = jnp.stack([hist_ffn[:, 1, :], u_s], axis=1)[None]

    return (y_p, y_s.reshape(ns, 1, D_MODEL), k_p, v_p, kidx_p, ssm_p, cssd_p, cffn_p,
            k_s, v_s, kidx_s, ssm_s, cssd_s, cffn_s)
```
